```python
import jax
import jax.numpy as jnp
from jax import lax
import numpy as np

D_MODEL = 1024
BATCH = 8
SEQ = 2048
DEPTH = 1
DEC_BATCH = 32
DEC_SEQ = 1
PAST_LEN = 16384
PAGE_SIZE = 128

HEAD_DIM = 64
W_MIX = D_MODEL
W_ATT = W_MIX // 2
W_MLSTM = W_MIX - W_ATT
H_ATT = W_ATT // HEAD_DIM
H_MLSTM = W_MLSTM // HEAD_DIM
DILATIONS = ((128, 1), (512, 4), (2048, 16))
MAX_WINDOW = 2048
ATT_BLOCK = 128
ROPE_THETA = 10000.0
CONV_W = 4
MLSTM_CHUNK = 64
PEER_HEADS = 8
PEER_NKEYS = 128
PEER_EXPERTS = PEER_NKEYS * PEER_NKEYS
PEER_DKEY = 256
PEER_TOPK = 16
PEER_BLOCK = 256
PROJ_SIZES = (W_ATT, W_ATT, W_ATT, 2 * W_MLSTM, W_MLSTM, 2 * H_MLSTM, W_MLSTM)
PROJ = sum(PROJ_SIZES)
EPS = 1e-6
NEG = -1e30

kernel_name = 'hymba_dilated_mlstm_peer_step'


def rms_norm(x, g):
    xf = x.astype(jnp.float32)
    y = xf * lax.rsqrt(jnp.mean(xf * xf, axis=-1, keepdims=True) + EPS)
    return (y * g.astype(jnp.float32)).astype(x.dtype)


def rope(x, pos):
    half = HEAD_DIM // 2
    inv = ROPE_THETA ** (-jnp.arange(half, dtype=jnp.float32) / half)
    ang = pos.astype(jnp.float32)[:, None] * inv[None, :]
    cos = jnp.cos(ang)[None, :, None, :]
    sin = jnp.sin(ang)[None, :, None, :]
    xf = x.astype(jnp.float32)
    x1, x2 = xf[..., :half], xf[..., half:]
    return jnp.concatenate([x1 * cos - x2 * sin, x1 * sin + x2 * cos], axis=-1).astype(x.dtype)


def in_projection(x, norm_g, w_in):
    z = jnp.einsum('btd,dp->btp', rms_norm(x, norm_g), w_in)
    offs = [int(o) for o in np.cumsum((0,) + PROJ_SIZES)]
    return tuple(z[..., offs[i]:offs[i + 1]] for i in range(len(PROJ_SIZES)))


def attn_qkv(aq, ak, av, pos, qn_g, kn_g):
    B, T = aq.shape[:2]
    q = rope(rms_norm(aq.reshape(B, T, H_ATT, HEAD_DIM), qn_g), pos)
    k = rope(rms_norm(ak.reshape(B, T, H_ATT, HEAD_DIM), kn_g), pos)
    v = av.reshape(B, T, H_ATT, HEAD_DIM)
    return q, k, v


def dilated_branch_prompt(q, k, v, window, dilation):
    B, S, H, Dh = q.shape
    span = window // dilation
    L = S // dilation
    nb = -(-L // ATT_BLOCK)
    Lp = nb * ATT_BLOCK

    def to_blocks(t):
        t = t.reshape(B, L, dilation, H, Dh)
        t = jnp.pad(t, ((0, 0), (0, Lp - L), (0, 0), (0, 0), (0, 0)))
        return t.reshape(B, nb, ATT_BLOCK, dilation, H, Dh)

    def with_prev(t):
        prev = jnp.pad(t[:, :-1], ((0, 0), (1, 0), (0, 0), (0, 0), (0, 0), (0, 0)))
        return jnp.concatenate([prev, t], axis=2)

    qb = to_blocks(q)
    kb = with_prev(to_blocks(k))
    vb = with_prev(to_blocks(v))
    s = jnp.einsum('bnqrhd,bnkrhd->bnrhqk', qb, kb).astype(jnp.float32) * (HEAD_DIM ** -0.5)
    blk = jnp.arange(nb)[:, None]
    qi = blk * ATT_BLOCK + jnp.arange(ATT_BLOCK)[None, :]
    ki = (blk - 1) * ATT_BLOCK + jnp.arange(2 * ATT_BLOCK)[None, :]
    dist = qi[:, :, None] - ki[:, None, :]
    mask = (dist >= 0) & (dist <= span) & (ki[:, None, :] >= 0)
    s = jnp.where(mask[None, :, None, None], s, NEG)
    m = jnp.max(s, axis=-1, keepdims=True)
    p = jnp.exp(s - m)
    l = jnp.sum(p, axis=-1, keepdims=True)
    o = jnp.einsum('bnrhqk,bnkrhd->bnqrhd', p / l, vb.astype(jnp.float32))
    o = o.reshape(B, Lp, dilation, H, Dh)[:, :L].reshape(B, S, H, Dh)
    lse = (m + jnp.log(l))[..., 0]
    lse = jnp.transpose(lse, (0, 1, 4, 2, 3)).reshape(B, Lp, dilation, H)[:, :L].reshape(B, S, H)
    return o, lse


def dilated_branch_sample(q, k_all, v_all, base, window, dilation):
    span = window // dilation
    T = q.shape[1]
    qpos = PAST_LEN + jnp.arange(T)
    kpos = qpos[:, None] - dilation * jnp.arange(span + 1)[None, :]
    valid = kpos >= base
    idx = jnp.clip(kpos - base, 0, k_all.shape[1] - 1)
    kg = k_all[:, idx]
    vg = v_all[:, idx]
    s = jnp.einsum('bthd,btjhd->bthj', q, kg).astype(jnp.float32) * (HEAD_DIM ** -0.5)
    s = jnp.where(valid[None, :, None, :], s, NEG)
    m = jnp.max(s, axis=-1, keepdims=True)
    p = jnp.exp(s - m)
    l = jnp.sum(p, axis=-1, keepdims=True)
    o = jnp.einsum('bthj,btjhd->bthd', p / l, vg.astype(jnp.float32))
    return o, (m + jnp.log(l))[..., 0]


def merge_branches(outs, lses):
    w = jax.nn.softmax(jnp.stack(lses, axis=0), axis=0)
    return jnp.einsum('nbth,nbthd->bthd', w, jnp.stack(outs, axis=0))


def mlstm_scan(q, k, v, ig, lf, C0, n0, m0, chunk):
    B, T, H, Dh = q.shape
    nc = T // chunk

    def chunks(t):
        t = t.astype(jnp.float32)
        return jnp.moveaxis(t.reshape((B, nc, chunk) + t.shape[2:]), 1, 0)

    causal = jnp.tril(jnp.ones((chunk, chunk), dtype=bool))

    def step(carry, inp):
        C, n, m = carry
        qc, kc, vc, igc, lfc = inp
        b = jnp.cumsum(lfc, axis=1)
        log_d = b[:, :, None, :] - b[:, None, :, :] + igc[:, None, :, :]
        log_d = jnp.where(causal[None, :, :, None], log_d, NEG)
        m_inter = b + m[:, None, :]
        m_t = jnp.maximum(m_inter, jnp.max(log_d, axis=2))
        sd = jnp.einsum('bthd,bshd->btsh', qc, kc) * jnp.exp(log_d - m_t[:, :, None, :])
        scale_inter = jnp.exp(m_inter - m_t)
        num = scale_inter[..., None] * jnp.einsum('bhvk,bthk->bthv', C, qc) + jnp.einsum('btsh,bshv->bthv', sd, vc)
        den = scale_inter * jnp.einsum('bhk,bthk->bth', n, qc) + jnp.sum(sd, axis=2)
        h = num / jnp.maximum(jnp.abs(den), jnp.exp(-m_t))[..., None]
        m_new = m_t[:, -1]
        decay = jnp.exp(b[:, -1] + m - m_new)
        w_s = jnp.exp(b[:, -1:] - b + igc - m_new[:, None])
        C_new = decay[..., None, None] * C + jnp.einsum('bsh,bshv,bshk->bhvk', w_s, vc, kc)
        n_new = decay[..., None] * n + jnp.einsum('bsh,bshk->bhk', w_s, kc)
        return (C_new, n_new, m_new), h

    carry0 = (C0.astype(jnp.float32), n0.astype(jnp.float32), m0.astype(jnp.float32))
    (C, n, m), hs = lax.scan(step, carry0, (chunks(q), chunks(k), chunks(v), chunks(ig), chunks(lf)))
    h = jnp.moveaxis(hs, 0, 1).reshape(B, T, H, Dh)
    return h, C, n, m


def mlstm_mixer(mqk, mv, mg, mo, conv_buf, C0, n0, m0, conv_w, conv_b, b_gates, norm_g, chunk):
    B, T = mqk.shape[:2]
    xcat = jnp.concatenate([conv_buf.astype(mqk.dtype), mqk], axis=1)
    conv = conv_b + conv_w[0] * xcat[:, 0:T]
    for j in range(1, CONV_W):
        conv = conv + conv_w[j] * xcat[:, j:j + T]
    qk = jax.nn.silu(conv)
    q = qk[..., :W_MLSTM].reshape(B, T, H_MLSTM, HEAD_DIM)
    k = qk[..., W_MLSTM:].reshape(B, T, H_MLSTM, HEAD_DIM) * (HEAD_DIM ** -0.5)
    v = mv.reshape(B, T, H_MLSTM, HEAD_DIM)
    gates = mg.astype(jnp.float32) + b_gates.astype(jnp.float32)
    ig = gates[..., :H_MLSTM]
    lf = jax.nn.log_sigmoid(gates[..., H_MLSTM:])
    h, C, n, m = mlstm_scan(q, k, v, ig, lf, C0, n0, m0, chunk)
    h = rms_norm(h, norm_g.reshape(H_MLSTM, HEAD_DIM)).reshape(B, T, W_MLSTM)
    out = (jax.nn.sigmoid(mo.astype(jnp.float32)) * h).astype(mqk.dtype)
    return out, C, n, m, xcat[:, T:]


def peer_ffn(h, w_query, qnorm_g, subkeys, u, v):
    B, T, D = h.shape
    n = B * T
    nblk = -(-n // PEER_BLOCK)
    hb_all = jnp.pad(h.reshape(n, D), ((0, nblk * PEER_BLOCK - n), (0, 0))).reshape(nblk, PEER_BLOCK, D)
    half = PEER_DKEY // 2

    def block(hb):
        q = rms_norm(jnp.einsum('td,dq->tq', hb, w_query).reshape(PEER_BLOCK, PEER_HEADS, PEER_DKEY), qnorm_g)
        s1 = jnp.einsum('thc,hnc->thn', q[..., :half], subkeys[:, 0]).astype(jnp.float32)
        s2 = jnp.einsum('thc,hnc->thn', q[..., half:], subkeys[:, 1]).astype(jnp.float32)
        v1, i1 = lax.top_k(s1, PEER_TOPK)
        v2, i2 = lax.top_k(s2, PEER_TOPK)
        cand = (v1[..., :, None] + v2[..., None, :]).reshape(PEER_BLOCK, PEER_HEADS, PEER_TOPK * PEER_TOPK)
        sc, ci = lax.top_k(cand, PEER_TOPK)
        e = (jnp.take_along_axis(i1, ci // PEER_TOPK, axis=-1) * PEER_NKEYS
             + jnp.take_along_axis(i2, ci % PEER_TOPK, axis=-1))
        g = jax.nn.softmax(sc, axis=-1).reshape(PEER_BLOCK, PEER_HEADS * PEER_TOPK)
        e = e.reshape(PEER_BLOCK, PEER_HEADS * PEER_TOPK)
        a = jax.nn.gelu(jnp.einsum('td,ted->te', hb, u[e]).astype(jnp.float32), approximate=False)
        return jnp.einsum('te,ted->td', (g * a).astype(hb.dtype), v[e])

    out = lax.map(block, hb_all).reshape(nblk * PEER_BLOCK, D)[:n]
    return out.reshape(B, T, D)


def layer_output(x, att, mh, w_out, norm2_g, w_query, qnorm_g, subkeys, u, v):
    B, T = x.shape[:2]
    mix = jnp.concatenate([att.reshape(B, T, W_ATT).astype(x.dtype), mh], axis=-1)
    x = x + jnp.einsum('btm,md->btd', mix, w_out)
    return x + peer_ffn(rms_norm(x, norm2_g), w_query, qnorm_g, subkeys, u, v)


def setup_inputs(seed: int = 0) -> dict:
    key = jax.random.key(seed)
    ks = jax.random.split(key, 24)
    f32 = jnp.float32

    def nrm(k, shape, scale):
        return scale * jax.random.normal(k, shape, f32)

    wb = min(MAX_WINDOW, PAST_LEN)
    forget_bias = jnp.linspace(3.0, 6.0, H_MLSTM, dtype=f32)
    b_gates = jnp.concatenate([nrm(ks[8], (DEPTH, H_MLSTM), 0.1),
                               forget_bias[None, :] + nrm(ks[9], (DEPTH, H_MLSTM), 0.1)], axis=-1)
    return {
        'x_prompt': nrm(ks[0], (BATCH, SEQ, D_MODEL), 1.0),
        'x_sample': nrm(ks[1], (DEC_BATCH, DEC_SEQ, D_MODEL), 1.0),
        'cache_attn_k': nrm(ks[2], (DEPTH, DEC_BATCH, wb, H_ATT, HEAD_DIM), 1.0),
        'cache_attn_v': nrm(ks[3], (DEPTH, DEC_BATCH, wb, H_ATT, HEAD_DIM), 1.0),
        'state_mlstm_C': nrm(ks[4], (DEPTH, DEC_BATCH, H_MLSTM, HEAD_DIM, HEAD_DIM), 0.1),
        'state_mlstm_n': nrm(ks[5], (DEPTH, DEC_BATCH, H_MLSTM, HEAD_DIM), 0.1),
        'state_mlstm_m': nrm(ks[6], (DEPTH, DEC_BATCH, H_MLSTM), 0.5),
        'state_mlstm_conv': nrm(ks[7], (DEPTH, DEC_BATCH, CONV_W - 1, 2 * W_MLSTM), 1.0),
        'norm1_g': 1.0 + nrm(ks[10], (DEPTH, D_MODEL), 0.02),
        'w_in': nrm(ks[11], (DEPTH, D_MODEL, PROJ), D_MODEL ** -0.5),
        'att_qnorm_g': 1.0 + nrm(ks[12], (DEPTH, HEAD_DIM), 0.02),
        'att_knorm_g': 1.0 + nrm(ks[13], (DEPTH, HEAD_DIM), 0.02),
        'b_gates': b_gates,
        'mlstm_conv_w': nrm(ks[14], (DEPTH, CONV_W, 2 * W_MLSTM), CONV_W ** -0.5),
        'mlstm_conv_b': nrm(ks[15], (DEPTH, 2 * W_MLSTM), 0.02),
        'mlstm_norm_g': 1.0 + nrm(ks[16], (DEPTH, W_MLSTM), 0.02),
        'w_out': nrm(ks[17], (DEPTH, W_MIX, D_MODEL), W_MIX ** -0.5),
        'norm2_g': 1.0 + nrm(ks[18], (DEPTH, D_MODEL), 0.02),
        'peer_w_query': nrm(ks[19], (DEPTH, D_MODEL, PEER_HEADS * PEER_DKEY), D_MODEL ** -0.5),
        'peer_qnorm_g': 1.0 + nrm(ks[20], (DEPTH, PEER_DKEY), 0.02),
        'peer_subkeys': nrm(ks[21], (DEPTH, PEER_HEADS, 2, PEER_NKEYS, PEER_DKEY // 2), (PEER_DKEY // 2) ** -0.5),
        'peer_u': nrm(ks[22], (DEPTH, PEER_EXPERTS, D_MODEL), D_MODEL ** -0.5),
        'peer_v': nrm(ks[23], (DEPTH, PEER_EXPERTS, D_MODEL), PEER_HEADS ** -0.5),
    }


def reference(x_prompt, x_sample, cache_attn_k, cache_attn_v, state_mlstm_C, state_mlstm_n, state_mlstm_m,
              state_mlstm_conv, norm1_g, w_in, att_qnorm_g, att_knorm_g, b_gates, mlstm_conv_w, mlstm_conv_b,
              mlstm_norm_g, w_out, norm2_g, peer_w_query, peer_qnorm_g, peer_subkeys, peer_u, peer_v):
    B, S = x_prompt.shape[:2]
    DB, DS = x_sample.shape[:2]
    wb_prompt = min(MAX_WINDOW, S)
    base = PAST_LEN - cache_attn_k.shape[2]
    pos_p = jnp.arange(S, dtype=jnp.int32)
    pos_s = PAST_LEN + jnp.arange(DS, dtype=jnp.int32)
    chunk_p = min(MLSTM_CHUNK, S)
    dt = x_prompt.dtype
    xp, xs = x_prompt, x_sample
    kp_l, vp_l, ks_l, vs_l = [], [], [], []
    cp_l, n_p_l, mp_l, bp_l = [], [], [], []
    cs_l, n_s_l, ms_l, bs_l = [], [], [], []
    for li in range(DEPTH):
        pw = (peer_w_query[li], peer_qnorm_g[li], peer_subkeys[li], peer_u[li], peer_v[li])
        mw = (mlstm_conv_w[li], mlstm_conv_b[li], b_gates[li], mlstm_norm_g[li])

        aq, ak, av, mqk, mv, mg, mo = in_projection(xp, norm1_g[li], w_in[li])
        q, k, v = attn_qkv(aq, ak, av, pos_p, att_qnorm_g[li], att_knorm_g[li])
        res = [dilated_branch_prompt(q, k, v, w, d) for (w, d) in DILATIONS]
        att = merge_branches([r[0] for r in res], [r[1] for r in res])
        zC = jnp.zeros((B, H_MLSTM, HEAD_DIM, HEAD_DIM), dt)
        zn = jnp.zeros((B, H_MLSTM, HEAD_DIM), dt)
        zm = jnp.zeros((B, H_MLSTM), dt)
        zconv = jnp.zeros((B, CONV_W - 1, 2 * W_MLSTM), dt)
        mh, C, n, m, cb = mlstm_mixer(mqk, mv, mg, mo, zconv, zC, zn, zm, *mw, chunk_p)
        xp = layer_output(xp, att, mh, w_out[li], norm2_g[li], *pw)
        kp_l.append(k[:, S - wb_prompt:])
        vp_l.append(v[:, S - wb_prompt:])
        cp_l.append(C.astype(dt))
        n_p_l.append(n.astype(dt))
        mp_l.append(m.astype(dt))
        bp_l.append(cb)

        aq, ak, av, mqk, mv, mg, mo = in_projection(xs, norm1_g[li], w_in[li])
        q, k, v = attn_qkv(aq, ak, av, pos_s, att_qnorm_g[li], att_knorm_g[li])
        k_all = jnp.concatenate([cache_attn_k[li].astype(k.dtype), k], axis=1)
        v_all = jnp.concatenate([cache_attn_v[li].astype(v.dtype), v], axis=1)
        res = [dilated_branch_sample(q, k_all, v_all, base, w, d) for (w, d) in DILATIONS]
        att = merge_branches([r[0] for r in res], [r[1] for r in res])
        mh, C, n, m, cb = mlstm_mixer(mqk, mv, mg, mo, state_mlstm_conv[li], state_mlstm_C[li],
                                      state_mlstm_n[li], state_mlstm_m[li], *mw, DS)
        xs = layer_output(xs, att, mh, w_out[li], norm2_g[li], *pw)
        ks_l.append(k)
        vs_l.append(v)
        cs_l.append(C.astype(dt))
        n_s_l.append(n.astype(dt))
        ms_l.append(m.astype(dt))
        bs_l.append(cb)

    new_k_prompt = jnp.stack(kp_l)
    new_v_prompt = jnp.stack(vp_l)
    new_k_sample = jnp.stack(ks_l)
    new_v_sample = jnp.stack(vs_l)
    new_C_prompt = jnp.stack(cp_l)
    new_n_prompt = jnp.stack(n_p_l)
    new_m_prompt = jnp.stack(mp_l)
    new_conv_prompt = jnp.stack(bp_l)
    new_C_sample = jnp.stack(cs_l)
    new_n_sample = jnp.stack(n_s_l)
    new_m_sample = jnp.stack(ms_l)
    new_conv_sample = jnp.stack(bs_l)
    return (xp, xs, new_k_prompt, new_v_prompt, new_k_sample, new_v_sample,
            new_C_prompt, new_n_prompt, new_m_prompt, new_conv_prompt,
            new_C_sample, new_n_sample, new_m_sample, new_conv_sample)
```

```python
import functools

import numpy as np
import jax
import jax.numpy as jnp
from jax import lax
from jax.experimental import pallas as pl
from jax.experimental.pallas import tpu as pltpu

F32 = jnp.float32
BF16 = jnp.bfloat16

D_MODEL = 1024
HEAD_DIM = 64
W_ATT = 512
W_MLSTM = 512
H_ATT = 8
H_MLSTM = 8
DILATIONS = ((128, 1), (512, 4), (2048, 16))
ATT_BLOCK = 128
ROPE_THETA = 10000.0
CONV_W = 4
PEER_HEADS = 8
PEER_NKEYS = 128
PEER_DKEY = 256
PEER_TOPK = 16
EPS = 1e-6
NEG = -1e30

LANES = 128
SUBLANES = 8
VMEM_LIMIT = 56 * 1024 * 1024


def _cparams(*sem):
    return pltpu.CompilerParams(dimension_semantics=sem, vmem_limit_bytes=VMEM_LIMIT)


def _split2(x):
    hi = x.astype(BF16)
    lo = (x - hi.astype(F32)).astype(BF16)
    return hi, lo


def _split3(x):
    hi = x.astype(BF16)
    r = x - hi.astype(F32)
    mid = r.astype(BF16)
    lo = (r - mid.astype(F32)).astype(BF16)
    return hi, mid, lo


def _dot(a, b):
    return jnp.dot(a, b, preferred_element_type=F32)


def _dot_nt(a, b):
    return lax.dot_general(a, b, (((1,), (1,)), ((), ())), preferred_element_type=F32)


def _dot_parts(parts, b):
    acc = _dot(parts[0], b)
    for p in parts[1:]:
        acc = acc + _dot(p, b)
    return acc


def _sigmoid(x):
    return 1.0 / (1.0 + jnp.exp(-x))


def _log_sigmoid(x):
    return jnp.minimum(x, 0.0) - jnp.log1p(jnp.exp(-jnp.abs(x)))


def _inproj_kernel(x_ref, g_ref, w_ref, wgh_ref, wgl_ref, qg_ref, kg_ref, cos_ref, sin_ref, bd_ref,
                   q_ref, k_ref, v_ref, mqk_ref, mv_ref, mo_ref, gates_ref):
    x = x_ref[...]
    ms = jnp.mean(x * x, axis=-1, keepdims=True)
    xn = x * lax.rsqrt(ms + EPS) * g_ref[...]
    xh, xl = _split2(xn)

    def seg(lo, hi):
        return _dot(xh, w_ref[:, lo:hi])

    bd = bd_ref[...]
    cos = cos_ref[...]
    sin = sin_ref[...]
    lane = lax.broadcasted_iota(jnp.int32, cos.shape, 1)
    first_half = (lane % HEAD_DIM) < (HEAD_DIM // 2)

    def head_norm_rope(a, g):
        sq = a * a
        hi, lo = _split2(sq)
        msq = _dot(hi, bd) + _dot(lo, bd)
        y = a * lax.rsqrt(msq + EPS) * g
        rot = jnp.where(first_half, pltpu.roll(y, W_ATT - HEAD_DIM // 2, 1), pltpu.roll(y, HEAD_DIM // 2, 1))
        return y * cos + rot * sin

    q_ref[...] = head_norm_rope(seg(0, 512), qg_ref[...]) * (HEAD_DIM ** -0.5)
    k_ref[...] = head_norm_rope(seg(512, 1024), kg_ref[...])
    v_ref[...] = seg(1024, 1536)
    mqk_ref[...] = seg(1536, 2560)
    mv_ref[...] = seg(2560, 3072)
    mo_ref[...] = seg(3072, 3584)
    gates_ref[...] = _dot(xh, wgh_ref[...]) + _dot(xl, wgh_ref[...]) + _dot(xh, wgl_ref[...])


def _rope_tables(pos):
    half = HEAD_DIM // 2
    inv = ROPE_THETA ** (-jnp.arange(half, dtype=F32) / half)
    ang = pos.astype(F32)[:, None] * inv[None, :]
    cos = jnp.cos(ang)
    sin = jnp.sin(ang)
    cos_h = jnp.concatenate([cos, cos], axis=-1)
    sin_h = jnp.concatenate([-sin, sin], axis=-1)
    return jnp.tile(cos_h, (1, H_ATT)), jnp.tile(sin_h, (1, H_ATT))


def _in_projection(x2d, pos, tm, norm_g, w_main, wg_hi, wg_lo, qg, kg, bd):
    n = x2d.shape[0]
    cos, sin = _rope_tables(pos)
    pblocks = pos.shape[0] // tm
    row = lambda i: (i, 0)
    fixed = lambda i: (0, 0)
    tab = lambda i: (i % pblocks, 0)
    widths = (512, 512, 512, 1024, 512, 512, LANES)
    return pl.pallas_call(
        _inproj_kernel,
        grid=(n // tm,),
        in_specs=[
            pl.BlockSpec((tm, D_MODEL), row),
            pl.BlockSpec((1, D_MODEL), fixed),
            pl.BlockSpec(w_main.shape, fixed),
            pl.BlockSpec(wg_hi.shape, fixed),
            pl.BlockSpec(wg_lo.shape, fixed),
            pl.BlockSpec((1, W_ATT), fixed),
            pl.BlockSpec((1, W_ATT), fixed),
            pl.BlockSpec((tm, W_ATT), tab),
            pl.BlockSpec((tm, W_ATT), tab),
            pl.BlockSpec((W_ATT, W_ATT), fixed),
        ],
        out_specs=[pl.BlockSpec((tm, w), row) for w in widths],
        out_shape=[jax.ShapeDtypeStruct((n, w), F32) for w in widths],
        compiler_params=_cparams("parallel"),
        name="in_projection",
    )(x2d, norm_g, w_main, wg_hi, wg_lo, qg, kg, cos, sin, bd)


def _attn_branch_kernel(q_ref, kp_ref, kc_ref, vp_ref, vc_ref, o_ref, l_ref):
    n = pl.program_id(2)
    q = q_ref[0]
    k2 = jnp.concatenate([kp_ref[0], kc_ref[0]], axis=0).astype(BF16)
    v2 = jnp.concatenate([vp_ref[0], vc_ref[0]], axis=0).astype(BF16)
    ii = lax.broadcasted_iota(jnp.int32, (ATT_BLOCK, 2 * ATT_BLOCK), 0)
    jj = lax.broadcasted_iota(jnp.int32, (ATT_BLOCK, 2 * ATT_BLOCK), 1)
    first_key = jnp.where(n > 0, 0, ATT_BLOCK)
    mask = jnp.minimum(jj - jnp.maximum(ii, first_key), ii + ATT_BLOCK - jj) >= 0
    lane = lax.broadcasted_iota(jnp.int32, (1, LANES), 1)
    for pair in range(W_ATT // LANES):
        sl = slice(pair * LANES, (pair + 1) * LANES)
        qs = q[:, sl]
        ks = k2[:, sl]
        vs = v2[:, sl]
        o_pair = None
        l_pair = None
        for sub in range(LANES // HEAD_DIM):
            in_head = (lane // HEAD_DIM) == sub
            qm = jnp.where(in_head, qs, 0.0).astype(BF16)
            s = _dot_nt(qm, ks)
            s = jnp.where(mask, s, NEG)
            m = jnp.max(s, axis=-1, keepdims=True)
            p = jnp.exp(s - m)
            l = jnp.sum(p, axis=-1, keepdims=True)
            o = _dot((p / l).astype(BF16), vs)
            lse = jnp.broadcast_to(m + jnp.log(l), o.shape)
            o_pair = o if o_pair is None else jnp.where(in_head, o, o_pair)
            l_pair = lse if l_pair is None else jnp.where(in_head, lse, l_pair)
        o_ref[0, :, sl] = o_pair
        l_ref[0, :, sl] = l_pair


def _attn_branch_prompt(q, k, v, batch, seq, dilation):
    length = seq // dilation
    nb = length // ATT_BLOCK
    shape3 = (batch, length, dilation * W_ATT)
    q3, k3, v3 = (t.reshape(shape3) for t in (q, k, v))
    cur = lambda b, r, n: (b, n, r)
    prev = lambda b, r, n: (b, jnp.maximum(n - 1, 0), r)
    blk = (1, ATT_BLOCK, W_ATT)
    o, l = pl.pallas_call(
        _attn_branch_kernel,
        grid=(batch, dilation, nb),
        in_specs=[pl.BlockSpec(blk, cur), pl.BlockSpec(blk, prev), pl.BlockSpec(blk, cur),
                  pl.BlockSpec(blk, prev), pl.BlockSpec(blk, cur)],
        out_specs=[pl.BlockSpec(blk, cur), pl.BlockSpec(blk, cur)],
        out_shape=[jax.ShapeDtypeStruct(shape3, F32)] * 2,
        compiler_params=_cparams("parallel", "parallel", "parallel"),
        name=f"attn_prompt_d{dilation}",
    )(q3, k3, k3, v3, v3)
    return o.reshape(batch * seq, W_ATT), l.reshape(batch * seq, W_ATT)


def _attn_sample_kernel(q_ref, kn_ref, vn_ref, k1_ref, v1_ref, k4_ref, v4_ref, k16_ref, v16_ref,
                        hsum_ref, hexp_ref, o1_ref, l1_ref, o4_ref, l4_ref, o16_ref, l16_ref):
    q = q_ref[0]
    kn = kn_ref[0]
    vn = vn_ref[0]
    hsum = hsum_ref[...]
    hexp = hexp_ref[...]

    def head_sum(a):
        return _dot_parts(_split3(a), hsum)

    def head_expand(a):
        return _dot_parts(_split3(a), hexp)

    qn8 = jnp.broadcast_to(q * kn, (SUBLANES, W_ATT))
    s_new = head_sum(qn8)[0:1]
    for kc_ref, vc_ref, o_ref, l_ref in ((k1_ref, v1_ref, o1_ref, l1_ref), (k4_ref, v4_ref, o4_ref, l4_ref),
                                         (k16_ref, v16_ref, o16_ref, l16_ref)):
        kc = kc_ref[0]
        vc = vc_ref[0]
        s = head_sum(kc * q)
        m = jnp.maximum(jnp.max(s, axis=0, keepdims=True), s_new)
        p = jnp.exp(s - m)
        p_new = jnp.exp(s_new - m)
        l = jnp.sum(p, axis=0, keepdims=True) + p_new
        w_full = head_expand(p / l)
        tail = jnp.concatenate([p_new / l, m + jnp.log(l)], axis=0)
        tail = head_expand(jnp.concatenate([tail, jnp.zeros((SUBLANES - 2, LANES), F32)], axis=0))
        o_ref[0] = jnp.sum(w_full * vc, axis=0, keepdims=True) + tail[0:1] * vn
        l_ref[0] = tail[1:2]


def _attn_sample(q, k_new, v_new, cache_k, cache_v, hsum, hexp):
    nb, wb = cache_k.shape[0], cache_k.shape[1]
    span = ATT_BLOCK
    q3, kn3, vn3 = (t.reshape(nb, 1, W_ATT) for t in (q, k_new, v_new))
    in_specs = [pl.BlockSpec((1, 1, W_ATT), lambda b: (b, 0, 0))] * 3
    args = [q3, kn3, vn3]
    for _, dil in DILATIONS:
        rows = wb // dil
        assert rows % span == 0
        last = rows // span - 1
        for c in (cache_k, cache_v):
            args.append(c.reshape(nb, rows, dil * W_ATT))
            in_specs.append(pl.BlockSpec((1, span, W_ATT), functools.partial(lambda b, blk: (b, blk, 0), blk=last)))
    in_specs += [pl.BlockSpec(hsum.shape, lambda b: (0, 0)), pl.BlockSpec(hexp.shape, lambda b: (0, 0))]
    args += [hsum, hexp]
    outs = pl.pallas_call(
        _attn_sample_kernel,
        grid=(nb,),
        in_specs=in_specs,
        out_specs=[pl.BlockSpec((1, 1, W_ATT), lambda b: (b, 0, 0))] * 6,
        out_shape=[jax.ShapeDtypeStruct((nb, 1, W_ATT), F32)] * 6,
        compiler_params=_cparams("parallel"),
        name="attn_sample",
    )(*args)
    return [o.reshape(nb, W_ATT) for o in outs]


MLSTM_CHUNK = 128


def _mlstm_prompt_kernel(mqk_ref, mv_ref, mo_ref, gates_ref, cw_ref, cb_ref, bg_ref, ng_ref, tril_ref, triu_ref,
                         mh_ref, c_out_ref, n_out_ref, m_out_ref,
                         conv_scr, c_scr, n_scr, m_scr):
    L = MLSTM_CHUNK
    c_idx = pl.program_id(1)

    @pl.when(c_idx == 0)
    def _():
        conv_scr[0:SUBLANES, :] = jnp.zeros((SUBLANES, 2 * W_MLSTM), F32)
        c_scr[...] = jnp.zeros_like(c_scr)
        n_scr[...] = jnp.zeros_like(n_scr)
        m_scr[...] = jnp.zeros_like(m_scr)

    x = mqk_ref[0]
    conv_scr[SUBLANES:SUBLANES + L, :] = x
    conv = cb_ref[...] + cw_ref[CONV_W - 1:CONV_W, :] * x
    for j in range(CONV_W - 1):
        shift = CONV_W - 1 - j
        conv = conv + cw_ref[j:j + 1, :] * conv_scr[SUBLANES - shift:SUBLANES - shift + L, :]
    conv_scr[0:SUBLANES, :] = x[L - SUBLANES:L, :]
    qk = conv * _sigmoid(conv)
    q_all = qk[:, :W_MLSTM]
    k_all = qk[:, W_MLSTM:] * (HEAD_DIM ** -0.5)
    v_all = mv_ref[0]
    vt_all = v_all.T
    mo = mo_ref[0]

    gb = gates_ref[0] + bg_ref[...]
    gbt = gb.T
    lf_col = _log_sigmoid(gb)
    lf_row = _log_sigmoid(gbt[H_MLSTM:2 * H_MLSTM, :])
    hi, mid, lo = _split3(lf_col)
    tril = tril_ref[...]
    b_col_all = _dot(tril, hi) + _dot(tril, mid) + _dot(tril, lo)
    b_row_all = _dot_parts(_split3(lf_row), triu_ref[...])

    ti = lax.broadcasted_iota(jnp.int32, (L, L), 0)
    si = lax.broadcasted_iota(jnp.int32, (L, L), 1)
    causal = si <= ti

    for h in range(H_MLSTM):
        hs = slice(h * HEAD_DIM, (h + 1) * HEAD_DIM)
        qh = q_all[:, hs]
        kh = k_all[:, hs]
        vh = v_all[:, hs]
        bc = b_col_all[:, H_MLSTM + h:H_MLSTM + h + 1]
        br = b_row_all[h:h + 1, :]
        igr = gbt[h:h + 1, :]
        igc = gb[:, h:h + 1]
        m_prev = m_scr[h:h + 1, 0:1]
        c_prev = c_scr[h]
        n_prev = n_scr[h:h + 1, :]

        log_d = jnp.where(causal, bc - br + igr, NEG)
        m_inter = bc + m_prev
        m_t = jnp.maximum(m_inter, jnp.max(log_d, axis=-1, keepdims=True))
        qh_b = qh.astype(BF16)
        kh_b = kh.astype(BF16)
        sd = _dot_nt(qh_b, kh_b) * jnp.exp(log_d - m_t)
        scale_inter = jnp.exp(m_inter - m_t)
        num = scale_inter * _dot_nt(qh_b, c_prev.astype(BF16)) + _dot(sd.astype(BF16), vh.astype(BF16))
        den = scale_inter * jnp.sum(qh * n_prev, axis=-1, keepdims=True) + jnp.sum(sd, axis=-1, keepdims=True)
        hh = num / jnp.maximum(jnp.abs(den), jnp.exp(-m_t))

        m_new = m_t[L - 1:L, :]
        b_last = bc[L - 1:L, :]
        decay = jnp.exp(b_last + m_prev - m_new)
        w_row = jnp.exp(b_last - br + igr - m_new)
        w_col = jnp.exp(b_last - bc + igc - m_new)
        vtw = (vt_all[hs, :] * w_row).astype(BF16)
        c_new = decay * c_prev + _dot(vtw, kh_b)
        n_new = decay * n_prev + jnp.sum(w_col * kh, axis=0, keepdims=True)
        c_scr[h] = c_new
        n_scr[h:h + 1, :] = n_new
        m_scr[h:h + 1, :] = jnp.broadcast_to(m_new, (1, LANES))
        c_out_ref[0, h] = c_new
        n_out_ref[0, h:h + 1, :] = n_new
        m_out_ref[0, h:h + 1, :] = jnp.broadcast_to(m_new, (1, LANES))

        y = hh * lax.rsqrt(jnp.mean(hh * hh, axis=-1, keepdims=True) + EPS) * ng_ref[:, hs]
        mh_ref[0, :, hs] = _sigmoid(mo[:, hs]) * y


def _mlstm_prompt(mqk, mv, mo, gates, batch, seq, conv_w, conv_b, bg, ng, tril, triu):
    L = MLSTM_CHUNK
    nchunk = seq // L
    chunk = lambda b, c: (b, c, 0)
    fixed = lambda b, c: (0, 0)
    return pl.pallas_call(
        _mlstm_prompt_kernel,
        grid=(batch, nchunk),
        in_specs=[
            pl.BlockSpec((1, L, 2 * W_MLSTM), chunk),
            pl.BlockSpec((1, L, W_MLSTM), chunk),
            pl.BlockSpec((1, L, W_MLSTM), chunk),
            pl.BlockSpec((1, L, LANES), chunk),
            pl.BlockSpec((CONV_W, 2 * W_MLSTM), fixed),
            pl.BlockSpec((1, 2 * W_MLSTM), fixed),
            pl.BlockSpec((1, LANES), fixed),
            pl.BlockSpec((1, W_MLSTM), fixed),
            pl.BlockSpec((L, L), fixed),
            pl.BlockSpec((L, L), fixed),
        ],
        out_specs=[
            pl.BlockSpec((1, L, W_MLSTM), chunk),
            pl.BlockSpec((1, H_MLSTM, HEAD_DIM, HEAD_DIM), lambda b, c: (b, 0, 0, 0)),
            pl.BlockSpec((1, H_MLSTM, HEAD_DIM), lambda b, c: (b, 0, 0)),
            pl.BlockSpec((1, H_MLSTM, LANES), lambda b, c: (b, 0, 0)),
        ],
        out_shape=[
            jax.ShapeDtypeStruct((batch, seq, W_MLSTM), F32),
            jax.ShapeDtypeStruct((batch, H_MLSTM, HEAD_DIM, HEAD_DIM), F32),
            jax.ShapeDtypeStruct((batch, H_MLSTM, HEAD_DIM), F32),
            jax.ShapeDtypeStruct((batch, H_MLSTM, LANES), F32),
        ],
        scratch_shapes=[
            pltpu.VMEM((L + SUBLANES, 2 * W_MLSTM), F32),
            pltpu.VMEM((H_MLSTM, HEAD_DIM, HEAD_DIM), F32),
            pltpu.VMEM((H_MLSTM, HEAD_DIM), F32),
            pltpu.VMEM((H_MLSTM, LANES), F32),
        ],
        compiler_params=_cparams("parallel", "arbitrary"),
        name="mlstm_prompt",
    )(mqk.reshape(batch, seq, -1), mv.reshape(batch, seq, -1), mo.reshape(batch, seq, -1),
      gates.reshape(batch, seq, -1), conv_w, conv_b, bg, ng, tril, triu)


def _mlstm_sample_pre_kernel(mqk_ref, buf_ref, gates_ref, cw_ref, cb_ref, bg_ref, qk_ref, g_ref):
    conv = cb_ref[...] + cw_ref[CONV_W - 1:CONV_W, :] * mqk_ref[...]
    for j in range(CONV_W - 1):
        conv = conv + cw_ref[j:j + 1, :] * buf_ref[j]
    qk = conv * _sigmoid(conv)
    lane = lax.broadcasted_iota(jnp.int32, qk.shape, 1)
    qk_ref[...] = jnp.where(lane < W_MLSTM, qk, qk * (HEAD_DIM ** -0.5))
    gb = gates_ref[...] + bg_ref[...]
    glane = lax.broadcasted_iota(jnp.int32, gb.shape, 1)
    g_ref[...] = jnp.where(glane < H_MLSTM, gb, _log_sigmoid(gb))


def _mlstm_sample_step_kernel(q_ref, k_ref, v_ref, mo_ref, ig_ref, lf_ref, c0_ref, n0_ref, m0_ref, ng_ref,
                              mh_ref, c_ref, n_ref, m_ref):
    for h in range(H_MLSTM):
        q = q_ref[0, h]
        k = k_ref[0, h]
        v = v_ref[0, h]
        ig = ig_ref[0, h]
        lf = lf_ref[0, h]
        c0 = c0_ref[0, h]
        n0 = n0_ref[0, h]
        m0 = m0_ref[0, h]
        m_inter = lf + m0
        m_t = jnp.maximum(m_inter, ig)
        w_in = jnp.exp(ig - m_t)
        sd = jnp.sum(q * k, axis=-1, keepdims=True) * w_in
        scale_inter = jnp.exp(m_inter - m_t)
        num = scale_inter * jnp.sum(c0 * q, axis=-1, keepdims=True) + sd * v
        den = scale_inter * jnp.sum(n0 * q, axis=-1, keepdims=True) + sd
        hh = num / jnp.maximum(jnp.abs(den), jnp.exp(-m_t))
        decay = jnp.exp(lf + m0 - m_t)
        c_ref[0, h] = decay * c0 + w_in * (v * k)
        n_ref[0, h] = decay * n0 + w_in * k
        m_ref[0, h] = m_t
        y = hh * lax.rsqrt(jnp.mean(hh * hh, axis=0, keepdims=True) + EPS) * ng_ref[h]
        mh_ref[0, h] = _sigmoid(mo_ref[0, h]) * y


def _mlstm_sample(mqk, mv, mo, gates, conv_buf, c0, n0, m0, conv_w, conv_b, bg, ng):
    nb = mqk.shape[0]
    full = lambda *shape: pl.BlockSpec(shape, lambda: (0,) * len(shape))
    qk, g = pl.pallas_call(
        _mlstm_sample_pre_kernel,
        in_specs=[full(nb, 2 * W_MLSTM), full(CONV_W - 1, nb, 2 * W_MLSTM), full(nb, LANES),
                  full(CONV_W, 2 * W_MLSTM), full(1, 2 * W_MLSTM), full(1, LANES)],
        out_specs=[full(nb, 2 * W_MLSTM), full(nb, LANES)],
        out_shape=[jax.ShapeDtypeStruct((nb, 2 * W_MLSTM), F32), jax.ShapeDtypeStruct((nb, LANES), F32)],
        name="mlstm_sample_pre",
    )(mqk, jnp.swapaxes(conv_buf, 0, 1), gates, conv_w, conv_b, bg)
    row = (nb, H_MLSTM, 1, HEAD_DIM)
    col = (nb, H_MLSTM, HEAD_DIM, 1)
    one = (nb, H_MLSTM, 1, 1)
    mat = (nb, H_MLSTM, HEAD_DIM, HEAD_DIM)
    spec = lambda shape: pl.BlockSpec((1,) + shape[1:], lambda b: (b, 0, 0, 0))
    ins = [
        (qk[:, :W_MLSTM].reshape(row), row), (qk[:, W_MLSTM:].reshape(row), row), (mv.reshape(col), col),
        (mo.reshape(col), col), (g[:, :H_MLSTM].reshape(one), one), (g[:, H_MLSTM:2 * H_MLSTM].reshape(one), one),
        (c0, mat), (n0.reshape(row), row), (m0.reshape(one), one),
    ]
    mh, c, n, m = pl.pallas_call(
        _mlstm_sample_step_kernel,
        grid=(nb,),
        in_specs=[spec(s) for _, s in ins] + [pl.BlockSpec((H_MLSTM, HEAD_DIM, 1), lambda b: (0, 0, 0))],
        out_specs=[spec(col), spec(mat), spec(row), spec(one)],
        out_shape=[jax.ShapeDtypeStruct(s, F32) for s in (col, mat, row, one)],
        compiler_params=_cparams("parallel"),
        name="mlstm_sample_step",
    )(*[a for a, _ in ins], ng.reshape(H_MLSTM, HEAD_DIM, 1))
    return mh.reshape(nb, W_MLSTM), c, n.reshape(nb, H_MLSTM, HEAD_DIM), m.reshape(nb, H_MLSTM)


def _outproj_kernel(x_ref, o1_ref, l1_ref, o2_ref, l2_ref, o3_ref, l3_ref, mh_ref, w_ref, g_ref, x1_ref, hn_ref):
    l1, l2, l3 = l1_ref[...], l2_ref[...], l3_ref[...]
    m = jnp.maximum(jnp.maximum(l1, l2), l3)
    e1, e2, e3 = jnp.exp(l1 - m), jnp.exp(l2 - m), jnp.exp(l3 - m)
    att = (e1 * o1_ref[...] + e2 * o2_ref[...] + e3 * o3_ref[...]) / (e1 + e2 + e3)
    x1 = (x_ref[...] + _dot(att.astype(BF16), w_ref[0:W_ATT, :])
          + _dot(mh_ref[...].astype(BF16), w_ref[W_ATT:, :]))
    x1_ref[...] = x1
    ms = jnp.mean(x1 * x1, axis=-1, keepdims=True)
    hn_ref[...] = (x1 * lax.rsqrt(ms + EPS) * g_ref[...]).astype(BF16)


def _out_projection(x2d, branches, mh, w_out, g2, tm):
    n = x2d.shape[0]
    row = lambda i: (i, 0)
    fixed = lambda i: (0, 0)
    half = pl.BlockSpec((tm, W_ATT), row)
    return pl.pallas_call(
        _outproj_kernel,
        grid=(n // tm,),
        in_specs=[pl.BlockSpec((tm, D_MODEL), row)] + [half] * 7
                 + [pl.BlockSpec((D_MODEL, D_MODEL), fixed), pl.BlockSpec((1, D_MODEL), fixed)],
        out_specs=[pl.BlockSpec((tm, D_MODEL), row), pl.BlockSpec((tm, D_MODEL), row)],
        out_shape=[jax.ShapeDtypeStruct((n, D_MODEL), F32), jax.ShapeDtypeStruct((n, D_MODEL), BF16)],
        compiler_params=_cparams("parallel"),
        name="out_projection",
    )(x2d, *branches, mh, w_out, g2)


def _peer_scores_kernel(hn_ref, wq_ref, g_ref, sk_ref, s_ref):
    q = _dot(hn_ref[...], wq_ref[...])
    half = PEER_DKEY // 2
    for h in range(PEER_HEADS):
        qh = q[:, h * PEER_DKEY:(h + 1) * PEER_DKEY]
        qn = qh * lax.rsqrt(jnp.mean(qh * qh, axis=-1, keepdims=True) + EPS) * g_ref[...]
        for part in range(2):
            s = _dot_nt(qn[:, part * half:(part + 1) * half].astype(BF16), sk_ref[2 * h + part])
            r0 = (2 * h + part) * PEER_NKEYS
            s_ref[r0:r0 + PEER_NKEYS, :] = s.T


def _peer_scores(hn, wq, g, sk, tm):
    n = hn.shape[0]
    rows = PEER_HEADS * 2 * PEER_NKEYS
    return pl.pallas_call(
        _peer_scores_kernel,
        grid=(n // tm,),
        in_specs=[pl.BlockSpec((tm, D_MODEL), lambda i: (i, 0)),
                  pl.BlockSpec(wq.shape, lambda i: (0, 0)),
                  pl.BlockSpec((1, PEER_DKEY), lambda i: (0, 0)),
                  pl.BlockSpec(sk.shape, lambda i: (0, 0, 0))],
        out_specs=pl.BlockSpec((rows, tm), lambda i: (0, i)),
        out_shape=jax.ShapeDtypeStruct((rows, n), F32),
        compiler_params=_cparams("parallel"),
        name="peer_scores",
    )(hn, wq, g, sk)


_STAIR = [(a, b) for a in range(PEER_TOPK) for b in range(PEER_TOPK) if (a + 1) * (b + 1) <= PEER_TOPK]


def _peer_select_kernel(s1_ref, s2_ref, rank2_ref, cnt1_ref, e1_ref, e2_ref,
                        work_ref, rank1_ref, vals1_ref, vals2_ref, cnt_ref):
    shape = work_ref.shape
    keyf = lax.broadcasted_iota(jnp.int32, shape, 0).astype(F32)
    unranked = float(PEER_TOPK)

    def extract(src_ref, rank_ref, vals_ref):
        work_ref[...] = src_ref[...]
        rank_ref[...] = jnp.full(shape, unranked, F32)

        def body(a, carry):
            w = work_ref[...]
            mx = jnp.max(w, axis=0, keepdims=True)
            first = jnp.min(jnp.where(w == mx, keyf, float(PEER_NKEYS)), axis=0, keepdims=True)
            sel = keyf == first
            work_ref[...] = jnp.where(sel, -jnp.inf, w)
            rank_ref[...] = jnp.where(sel, a.astype(F32), rank_ref[...])
            vals_ref[pl.ds(a, 1)] = mx
            return carry

        lax.fori_loop(0, PEER_TOPK, body, 0)

    extract(s1_ref, rank1_ref, vals1_ref)
    extract(s2_ref, rank2_ref, vals2_ref)

    v1 = [vals1_ref[a] for a in range(PEER_TOPK)]
    v2 = [vals2_ref[b] for b in range(PEER_TOPK)]
    cand = [v1[a] + v2[b] for a, b in _STAIR]
    pos = []
    for i, (a, b) in enumerate(_STAIR):
        static = sum(1 for (a2, b2) in _STAIR if a2 <= a and b2 <= b and (a2, b2) != (a, b))
        pos.append(jnp.full(cand[0].shape, float(static), F32))
    for i, (ai, bi) in enumerate(_STAIR):
        for j in range(i + 1, len(_STAIR)):
            aj, bj = _STAIR[j]
            if ai <= aj and bi <= bj:
                continue
            i_first = cand[i] >= cand[j]
            pos[j] = pos[j] + jnp.where(i_first, 1.0, 0.0)
            pos[i] = pos[i] + jnp.where(i_first, 0.0, 1.0)
    e1s = [jnp.exp(v1[a] - v1[0]) for a in range(PEER_TOPK)]
    e2s = [jnp.exp(v2[b] - v2[0]) for b in range(PEER_TOPK)]
    z = jnp.zeros_like(cand[0])
    for a in range(PEER_TOPK):
        cnt_a = jnp.zeros_like(z)
        za = jnp.zeros_like(z)
        for i, (a2, b) in enumerate(_STAIR):
            if a2 != a:
                continue
            chosen = pos[i] < float(PEER_TOPK)
            cnt_a = cnt_a + jnp.where(chosen, 1.0, 0.0)
            za = za + jnp.where(chosen, e2s[b], 0.0)
        cnt_ref[a] = cnt_a
        z = z + e1s[a] * za

    cnt1_ref[...] = jnp.zeros(shape, F32)

    def spread(a, carry):
        cnt1_ref[...] = jnp.where(rank1_ref[...] == a.astype(F32), cnt_ref[pl.ds(a, 1)], cnt1_ref[...])
        return carry

    lax.fori_loop(0, PEER_TOPK, spread, 0)
    e1_ref[...] = jnp.exp(s1_ref[...] - vals1_ref[0:1]) / z[None]
    e2_ref[...] = jnp.exp(s2_ref[...] - vals2_ref[0:1])


def _peer_select(scores_t):
    rows, n = scores_t.shape
    groups = n // LANES
    s3 = scores_t.reshape(rows, groups, LANES)
    blk = (PEER_NKEYS, SUBLANES, LANES)
    out_rows = PEER_HEADS * PEER_NKEYS
    outs = pl.pallas_call(
        _peer_select_kernel,
        grid=(PEER_HEADS, groups // SUBLANES),
        in_specs=[pl.BlockSpec(blk, lambda h, g: (2 * h, g, 0)), pl.BlockSpec(blk, lambda h, g: (2 * h + 1, g, 0))],
        out_specs=[pl.BlockSpec(blk, lambda h, g: (h, g, 0))] * 4,
        out_shape=[jax.ShapeDtypeStruct((out_rows, groups, LANES), F32)] * 4,
        scratch_shapes=[pltpu.VMEM(blk, F32), pltpu.VMEM(blk, F32),
                        pltpu.VMEM((PEER_TOPK, SUBLANES, LANES), F32), pltpu.VMEM((PEER_TOPK, SUBLANES, LANES), F32),
                        pltpu.VMEM((PEER_TOPK, SUBLANES, LANES), F32)],
        compiler_params=_cparams("parallel", "parallel"),
        name="peer_select",
    )(s3, s3)
    return [o.reshape(out_rows, n) for o in outs]


PEER_TOK_TILE = 512
PEER_I1_PER_STEP = 2


def _peer_dense_kernel(ht_ref, u_ref, vt_ref, rank2_ref, e2_ref, cnt1_ref, e1_ref, x1_ref, y_ref, acc_ref):
    j = pl.program_id(1)

    @pl.when(j == 0)
    def _():
        acc_ref[...] = jnp.zeros_like(acc_ref)

    a = _dot(u_ref[...], ht_ref[...])
    act = 0.5 * a * (1.0 + lax.erf(a * (2.0 ** -0.5)))
    gates = []
    for s in range(PEER_I1_PER_STEP):
        i1 = j * PEER_I1_PER_STEP + s
        g = None
        for h in range(PEER_HEADS):
            rows = slice(h * PEER_NKEYS, (h + 1) * PEER_NKEYS)
            cnt = cnt1_ref[pl.ds(h * PEER_NKEYS + i1, 1), :]
            e1 = e1_ref[pl.ds(h * PEER_NKEYS + i1, 1), :]
            w = jnp.where(rank2_ref[rows, :] < cnt, e2_ref[rows, :] * e1, 0.0)
            g = w if g is None else g + w
        gates.append(g)
    p = (jnp.concatenate(gates, axis=0) * act).astype(BF16)
    acc_ref[...] += _dot(vt_ref[...], p)

    @pl.when(j == pl.num_programs(1) - 1)
    def _():
        y_ref[...] = x1_ref[...] + acc_ref[...].T


def _peer_dense(ht, u, vt, rank2, e2, cnt1, e1, x1):
    n = ht.shape[1]
    T = PEER_TOK_TILE
    eb = PEER_I1_PER_STEP * PEER_NKEYS
    nexp = u.shape[0]
    sel_rows = PEER_HEADS * PEER_NKEYS
    tok = lambda t, j: (0, t)
    return pl.pallas_call(
        _peer_dense_kernel,
        grid=(n // T, nexp // eb),
        in_specs=[
            pl.BlockSpec((D_MODEL, T), tok),
            pl.BlockSpec((eb, D_MODEL), lambda t, j: (j, 0)),
            pl.BlockSpec((D_MODEL, eb), lambda t, j: (0, j)),
            pl.BlockSpec((sel_rows, T), tok),
            pl.BlockSpec((sel_rows, T), tok),
            pl.BlockSpec((sel_rows, T), tok),
            pl.BlockSpec((sel_rows, T), tok),
            pl.BlockSpec((T, D_MODEL), lambda t, j: (t, 0)),
        ],
        out_specs=pl.BlockSpec((T, D_MODEL), lambda t, j: (t, 0)),
        out_shape=jax.ShapeDtypeStruct((n, D_MODEL), F32),
        scratch_shapes=[pltpu.VMEM((D_MODEL, T), F32)],
        compiler_params=_cparams("parallel", "arbitrary"),
        name="peer_dense",
    )(ht, u, vt, rank2, e2, cnt1, e1, x1)


def _tri_constants():
    L = MLSTM_CHUNK
    tril = np.tril(np.ones((L, L), np.float32))
    return jnp.asarray(tril, BF16), jnp.asarray(tril.T, BF16)


def _head_constants():
    bd = np.kron(np.eye(H_ATT, dtype=np.float32), np.full((HEAD_DIM, HEAD_DIM), 1.0 / HEAD_DIM, np.float32))
    hsum = np.zeros((W_ATT, LANES), np.float32)
    hsum[np.arange(W_ATT), np.arange(W_ATT) // HEAD_DIM] = 1.0
    return jnp.asarray(bd, BF16), jnp.asarray(hsum, BF16), jnp.asarray(hsum.T, BF16)


def kernel(x_prompt, x_sample, cache_attn_k, cache_attn_v, state_mlstm_C, state_mlstm_n, state_mlstm_m,
           state_mlstm_conv, norm1_g, w_in, att_qnorm_g, att_knorm_g, b_gates, mlstm_conv_w, mlstm_conv_b,
           mlstm_norm_g, w_out, norm2_g, peer_w_query, peer_qnorm_g, peer_subkeys, peer_u, peer_v):
    batch, seq = x_prompt.shape[:2]
    nsamp = x_sample.shape[0]
    past_len = 16384
    assert norm1_g.shape[0] == 1 and x_sample.shape[1] == 1 and seq % MLSTM_CHUNK == 0
    wb = cache_attn_k.shape[2]
    li = 0
    bd, hsum, hexp = _head_constants()
    tril, triu = _tri_constants()

    w = w_in[li]
    gate_lo, gate_hi = 3072, 3072 + 2 * H_MLSTM
    w_main = jnp.concatenate([w[:, :gate_lo], w[:, gate_hi:]], axis=1).astype(BF16)
    wg = jnp.pad(w[:, gate_lo:gate_hi], ((0, 0), (0, LANES - 2 * H_MLSTM)))
    wg_hi = wg.astype(BF16)
    wg_lo = (wg - wg_hi.astype(F32)).astype(BF16)
    g1 = norm1_g[li][None, :]
    qg = jnp.tile(att_qnorm_g[li], H_ATT)[None, :]
    kg = jnp.tile(att_knorm_g[li], H_ATT)[None, :]
    bg = jnp.pad(b_gates[li], (0, LANES - 2 * H_MLSTM))[None, :]
    conv_w = mlstm_conv_w[li]
    conv_b = mlstm_conv_b[li][None, :]
    ng = mlstm_norm_g[li][None, :]
    wo = w_out[li].astype(BF16)
    g2 = norm2_g[li][None, :]
    wq = peer_w_query[li].astype(BF16)
    pqg = peer_qnorm_g[li][None, :]
    sk = peer_subkeys[li].reshape(PEER_HEADS * 2, PEER_NKEYS, PEER_DKEY // 2).astype(BF16)
    u_b = peer_u[li].astype(BF16)
    vt_b = peer_v[li].astype(BF16).T

    n_p = batch * seq
    xp2 = x_prompt.reshape(n_p, D_MODEL)
    pos_p = jnp.arange(seq, dtype=jnp.int32)
    q, k, v, mqk, mv, mo, gates = _in_projection(xp2, pos_p, 256, g1, w_main, wg_hi, wg_lo, qg, kg, bd)
    branches_p = []
    for _, dil in DILATIONS:
        branches_p.extend(_attn_branch_prompt(q, k, v, batch, seq, dil))
    mh_p, c_p, n_p_state, m_p = _mlstm_prompt(mqk, mv, mo, gates, batch, seq, conv_w, conv_b, bg, ng, tril, triu)
    x1_p, hn_p = _out_projection(xp2, branches_p, mh_p.reshape(n_p, W_MLSTM), wo, g2, 256)
    wbp = min(wb, seq)
    new_k_prompt = k.reshape(batch, seq, H_ATT, HEAD_DIM)[None, :, seq - wbp:]
    new_v_prompt = v.reshape(batch, seq, H_ATT, HEAD_DIM)[None, :, seq - wbp:]
    new_conv_prompt = mqk.reshape(batch, seq, -1)[None, :, seq - (CONV_W - 1):]

    xs2 = x_sample.reshape(nsamp, D_MODEL)
    pos_s = jnp.full((nsamp,), past_len, dtype=jnp.int32)
    qs, ks, vs, mqk_s, mv_s, mo_s, gates_s = _in_projection(xs2, pos_s, nsamp, g1, w_main, wg_hi, wg_lo, qg, kg, bd)
    ck = cache_attn_k[li].reshape(nsamp, wb, W_ATT)
    cv = cache_attn_v[li].reshape(nsamp, wb, W_ATT)
    branches_s = _attn_sample(qs, ks, vs, ck, cv, hsum, hexp)
    mh_s, c_s, n_s, m_s = _mlstm_sample(mqk_s, mv_s, mo_s, gates_s, state_mlstm_conv[li], state_mlstm_C[li],
                                        state_mlstm_n[li], state_mlstm_m[li], conv_w, conv_b, bg, ng)
    x1_s, hn_s = _out_projection(xs2, branches_s, mh_s, wo, g2, nsamp)
    new_conv_sample = jnp.concatenate([state_mlstm_conv[li][:, 1:], mqk_s[:, None, :]], axis=1)[None]

    n_all = n_p + nsamp
    group = LANES * SUBLANES
    n_pad = -(-n_all // group) * group
    pad = n_pad - n_all
    hn_all = jnp.concatenate([hn_p, hn_s, jnp.zeros((pad, D_MODEL), BF16)], axis=0)
    x1_all = jnp.concatenate([x1_p, x1_s, jnp.zeros((pad, D_MODEL), F32)], axis=0)
    scores_t = _peer_scores(hn_all, wq, pqg, sk, 256)
    rank2, cnt1, e1, e2 = _peer_select(scores_t)
    y_all = _peer_dense(hn_all.T, u_b, vt_b, rank2, e2, cnt1, e1, x1_all)
    y_prompt = y_all[:n_p].reshape(batch, seq, D_MODEL)
    y_sample = y_all[n_p:n_all].reshape(nsamp, 1, D_MODEL)

    return (y_prompt, y_sample, new_k_prompt, new_v_prompt,
            ks.reshape(1, nsamp, 1, H_ATT, HEAD_DIM), vs.reshape(1, nsamp, 1, H_ATT, HEAD_DIM),
            c_p[None], n_p_state[None], m_p[None, :, :, 0], new_conv_prompt,
            c_s[None], n_s[None], m_s[None], new_conv_sample)
```

```python
import functools

import numpy as np
import jax
import jax.numpy as jnp
from jax import lax
from jax.experimental import pallas as pl
from jax.experimental.pallas import tpu as pltpu

F32 = jnp.float32
BF16 = jnp.bfloat16

D_MODEL = 1024
HEAD_DIM = 64
W_ATT = 512
W_MLSTM = 512
H_ATT = 8
H_MLSTM = 8
DILATIONS = ((128, 1), (512, 4), (2048, 16))
ATT_BLOCK = 128
ROPE_THETA = 10000.0
CONV_W = 4
PEER_HEADS = 8
PEER_NKEYS = 128
PEER_DKEY = 256
PEER_TOPK = 16
EPS = 1e-6
NEG = -1e30

LANES = 128
SUBLANES = 8
VMEM_LIMIT = 56 * 1024 * 1024


def _cparams(*sem):
    return pltpu.CompilerParams(dimension_semantics=sem, vmem_limit_bytes=VMEM_LIMIT)


def _split2(x):
    hi = x.astype(BF16)
    lo = (x - hi.astype(F32)).astype(BF16)
    return hi, lo


def _split3(x):
    hi = x.astype(BF16)
    r = x - hi.astype(F32)
    mid = r.astype(BF16)
    lo = (r - mid.astype(F32)).astype(BF16)
    return hi, mid, lo


def _dot(a, b):
    return jnp.dot(a, b, preferred_element_type=F32)


def _dot_nt(a, b):
    return lax.dot_general(a, b, (((1,), (1,)), ((), ())), preferred_element_type=F32)


def _dot_parts(parts, b):
    acc = _dot(parts[0], b)
    for p in parts[1:]:
        acc = acc + _dot(p, b)
    return acc


def _sigmoid(x):
    return 1.0 / (1.0 + jnp.exp(-x))


def _log_sigmoid(x):
    return jnp.minimum(x, 0.0) - jnp.log1p(jnp.exp(-jnp.abs(x)))


def _inproj_kernel(x_ref, g_ref, w_ref, wgh_ref, wgl_ref, qg_ref, kg_ref, cos_ref, sin_ref, bd_ref,
                   q_ref, k_ref, v_ref, mqk_ref, mv_ref, mo_ref, gates_ref):
    x = x_ref[...]
    ms = jnp.mean(x * x, axis=-1, keepdims=True)
    xn = x * lax.rsqrt(ms + EPS) * g_ref[...]
    xh, xl = _split2(xn)

    def seg(lo, hi):
        return _dot(xh, w_ref[:, lo:hi])

    bd = bd_ref[...]
    cos = cos_ref[...]
    sin = sin_ref[...]
    lane = lax.broadcasted_iota(jnp.int32, cos.shape, 1)
    first_half = (lane % HEAD_DIM) < (HEAD_DIM // 2)

    def head_norm_rope(a, g):
        sq = a * a
        hi, lo = _split2(sq)
        msq = _dot(hi, bd) + _dot(lo, bd)
        y = a * lax.rsqrt(msq + EPS) * g
        rot = jnp.where(first_half, pltpu.roll(y, W_ATT - HEAD_DIM // 2, 1), pltpu.roll(y, HEAD_DIM // 2, 1))
        return y * cos + rot * sin

    q_ref[...] = head_norm_rope(seg(0, 512), qg_ref[...]) * (HEAD_DIM ** -0.5)
    k_ref[...] = head_norm_rope(seg(512, 1024), kg_ref[...])
    v_ref[...] = seg(1024, 1536)
    mqk_ref[...] = seg(1536, 2560)
    mv_ref[...] = seg(2560, 3072)
    mo_ref[...] = seg(3072, 3584)
    gates_ref[...] = _dot(xh, wgh_ref[...]) + _dot(xl, wgh_ref[...]) + _dot(xh, wgl_ref[...])


def _rope_tables(pos):
    half = HEAD_DIM // 2
    inv = ROPE_THETA ** (-jnp.arange(half, dtype=F32) / half)
    ang = pos.astype(F32)[:, None] * inv[None, :]
    cos = jnp.cos(ang)
    sin = jnp.sin(ang)
    cos_h = jnp.concatenate([cos, cos], axis=-1)
    sin_h = jnp.concatenate([-sin, sin], axis=-1)
    return jnp.tile(cos_h, (1, H_ATT)), jnp.tile(sin_h, (1, H_ATT))


def _in_projection(x2d, pos, tm, norm_g, w_main, wg_hi, wg_lo, qg, kg, bd):
    n = x2d.shape[0]
    cos, sin = _rope_tables(pos)
    pblocks = pos.shape[0] // tm
    row = lambda i: (i, 0)
    fixed = lambda i: (0, 0)
    tab = lambda i: (i % pblocks, 0)
    widths = (512, 512, 512, 1024, 512, 512, LANES)
    return pl.pallas_call(
        _inproj_kernel,
        grid=(n // tm,),
        in_specs=[
            pl.BlockSpec((tm, D_MODEL), row),
            pl.BlockSpec((1, D_MODEL), fixed),
            pl.BlockSpec(w_main.shape, fixed),
            pl.BlockSpec(wg_hi.shape, fixed),
            pl.BlockSpec(wg_lo.shape, fixed),
            pl.BlockSpec((1, W_ATT), fixed),
            pl.BlockSpec((1, W_ATT), fixed),
            pl.BlockSpec((tm, W_ATT), tab),
            pl.BlockSpec((tm, W_ATT), tab),
            pl.BlockSpec((W_ATT, W_ATT), fixed),
        ],
        out_specs=[pl.BlockSpec((tm, w), row) for w in widths],
        out_shape=[jax.ShapeDtypeStruct((n, w), F32) for w in widths],
        compiler_params=_cparams("parallel"),
        name="in_projection",
    )(x2d, norm_g, w_main, wg_hi, wg_lo, qg, kg, cos, sin, bd)


def _attn_branch_kernel(q_ref, kp_ref, kc_ref, vp_ref, vc_ref, o_ref, l_ref):
    n = pl.program_id(2)
    q = q_ref[0]
    k2 = jnp.concatenate([kp_ref[0], kc_ref[0]], axis=0).astype(BF16)
    v2 = jnp.concatenate([vp_ref[0], vc_ref[0]], axis=0).astype(BF16)
    ii = lax.broadcasted_iota(jnp.int32, (ATT_BLOCK, 2 * ATT_BLOCK), 0)
    jj = lax.broadcasted_iota(jnp.int32, (ATT_BLOCK, 2 * ATT_BLOCK), 1)
    first_key = jnp.where(n > 0, 0, ATT_BLOCK)
    mask = jnp.minimum(jj - jnp.maximum(ii, first_key), ii + ATT_BLOCK - jj) >= 0
    lane = lax.broadcasted_iota(jnp.int32, (1, LANES), 1)
    for pair in range(W_ATT // LANES):
        sl = slice(pair * LANES, (pair + 1) * LANES)
        qs = q[:, sl]
        ks = k2[:, sl]
        vs = v2[:, sl]
        o_pair = None
        l_pair = None
        for sub in range(LANES // HEAD_DIM):
            in_head = (lane // HEAD_DIM) == sub
            qm = jnp.where(in_head, qs, 0.0).astype(BF16)
            s = _dot_nt(qm, ks)
            s = jnp.where(mask, s, NEG)
            m = jnp.max(s, axis=-1, keepdims=True)
            p = jnp.exp(s - m)
            l = jnp.sum(p, axis=-1, keepdims=True)
            o = _dot((p / l).astype(BF16), vs)
            lse = jnp.broadcast_to(m + jnp.log(l), o.shape)
            o_pair = o if o_pair is None else jnp.where(in_head, o, o_pair)
            l_pair = lse if l_pair is None else jnp.where(in_head, lse, l_pair)
        o_ref[0, :, sl] = o_pair
        l_ref[0, :, sl] = l_pair


def _attn_branch_prompt(q, k, v, batch, seq, dilation):
    length = seq // dilation
    nb = length // ATT_BLOCK
    shape3 = (batch, length, dilation * W_ATT)
    q3, k3, v3 = (t.reshape(shape3) for t in (q, k, v))
    cur = lambda b, r, n: (b, n, r)
    prev = lambda b, r, n: (b, jnp.maximum(n - 1, 0), r)
    blk = (1, ATT_BLOCK, W_ATT)
    o, l = pl.pallas_call(
        _attn_branch_kernel,
        grid=(batch, dilation, nb),
        in_specs=[pl.BlockSpec(blk, cur), pl.BlockSpec(blk, prev), pl.BlockSpec(blk, cur),
                  pl.BlockSpec(blk, prev), pl.BlockSpec(blk, cur)],
        out_specs=[pl.BlockSpec(blk, cur), pl.BlockSpec(blk, cur)],
        out_shape=[jax.ShapeDtypeStruct(shape3, F32)] * 2,
        compiler_params=_cparams("parallel", "parallel", "parallel"),
        name=f"attn_prompt_d{dilation}",
    )(q3, k3, k3, v3, v3)
    return o.reshape(batch * seq, W_ATT), l.reshape(batch * seq, W_ATT)


def _attn_sample_kernel(q_ref, kn_ref, vn_ref, k1_ref, v1_ref, k4_ref, v4_ref, k16_ref, v16_ref,
                        hsum_ref, hexp_ref, o1_ref, l1_ref, o4_ref, l4_ref, o16_ref, l16_ref):
    q = q_ref[0]
    kn = kn_ref[0]
    vn = vn_ref[0]
    hsum = hsum_ref[...]
    hexp = hexp_ref[...]

    def head_sum(a):
        return _dot_parts(_split3(a), hsum)

    def head_expand(a):
        return _dot_parts(_split3(a), hexp)

    qn8 = jnp.broadcast_to(q * kn, (SUBLANES, W_ATT))
    s_new = head_sum(qn8)[0:1]
    for kc_ref, vc_ref, o_ref, l_ref in ((k1_ref, v1_ref, o1_ref, l1_ref), (k4_ref, v4_ref, o4_ref, l4_ref),
                                         (k16_ref, v16_ref, o16_ref, l16_ref)):
        kc = kc_ref[0]
        vc = vc_ref[0]
        s = head_sum(kc * q)
        m = jnp.maximum(jnp.max(s, axis=0, keepdims=True), s_new)
        p = jnp.exp(s - m)
        p_new = jnp.exp(s_new - m)
        l = jnp.sum(p, axis=0, keepdims=True) + p_new
        w_full = head_expand(p / l)
        tail = jnp.concatenate([p_new / l, m + jnp.log(l)], axis=0)
        tail = head_expand(jnp.concatenate([tail, jnp.zeros((SUBLANES - 2, LANES), F32)], axis=0))
        o_ref[0] = jnp.sum(w_full * vc, axis=0, keepdims=True) + tail[0:1] * vn
        l_ref[0] = tail[1:2]


def _attn_sample(q, k_new, v_new, cache_k, cache_v, hsum, hexp):
    nb, wb = cache_k.shape[0], cache_k.shape[1]
    span = ATT_BLOCK
    q3, kn3, vn3 = (t.reshape(nb, 1, W_ATT) for t in (q, k_new, v_new))
    in_specs = [pl.BlockSpec((1, 1, W_ATT), lambda b: (b, 0, 0))] * 3
    args = [q3, kn3, vn3]
    for _, dil in DILATIONS:
        rows = wb // dil
        assert rows % span == 0
        last = rows // span - 1
        for c in (cache_k, cache_v):
            args.append(c.reshape(nb, rows, dil * W_ATT))
            in_specs.append(pl.BlockSpec((1, span, W_ATT), functools.partial(lambda b, blk: (b, blk, 0), blk=last)))
    in_specs += [pl.BlockSpec(hsum.shape, lambda b: (0, 0)), pl.BlockSpec(hexp.shape, lambda b: (0, 0))]
    args += [hsum, hexp]
    outs = pl.pallas_call(
        _attn_sample_kernel,
        grid=(nb,),
        in_specs=in_specs,
        out_specs=[pl.BlockSpec((1, 1, W_ATT), lambda b: (b, 0, 0))] * 6,
        out_shape=[jax.ShapeDtypeStruct((nb, 1, W_ATT), F32)] * 6,
        compiler_params=_cparams("parallel"),
        name="attn_sample",
    )(*args)
    return [o.reshape(nb, W_ATT) for o in outs]


MLSTM_CHUNK = 128


def _mlstm_prompt_kernel(mqk_ref, mv_ref, mo_ref, gates_ref, cw_ref, cb_ref, bg_ref, ng_ref, tril_ref, triu_ref,
                         mh_ref, c_out_ref, n_out_ref, m_out_ref,
                         conv_scr, c_scr, n_scr, m_scr):
    L = MLSTM_CHUNK
    c_idx = pl.program_id(1)

    @pl.when(c_idx == 0)
    def _():
        conv_scr[0:SUBLANES, :] = jnp.zeros((SUBLANES, 2 * W_MLSTM), F32)
        c_scr[...] = jnp.zeros_like(c_scr)
        n_scr[...] = jnp.zeros_like(n_scr)
        m_scr[...] = jnp.zeros_like(m_scr)

    x = mqk_ref[0]
    conv_scr[SUBLANES:SUBLANES + L, :] = x
    conv = cb_ref[...] + cw_ref[CONV_W - 1:CONV_W, :] * x
    for j in range(CONV_W - 1):
        shift = CONV_W - 1 - j
        conv = conv + cw_ref[j:j + 1, :] * conv_scr[SUBLANES - shift:SUBLANES - shift + L, :]
    conv_scr[0:SUBLANES, :] = x[L - SUBLANES:L, :]
    qk = conv * _sigmoid(conv)
    q_all = qk[:, :W_MLSTM]
    k_all = qk[:, W_MLSTM:] * (HEAD_DIM ** -0.5)
    v_all = mv_ref[0]
    vt_all = v_all.T
    mo = mo_ref[0]

    gb = gates_ref[0] + bg_ref[...]
    gbt = gb.T
    lf_col = _log_sigmoid(gb)
    lf_row = _log_sigmoid(gbt[H_MLSTM:2 * H_MLSTM, :])
    hi, mid, lo = _split3(lf_col)
    tril = tril_ref[...]
    b_col_all = _dot(tril, hi) + _dot(tril, mid) + _dot(tril, lo)
    b_row_all = _dot_parts(_split3(lf_row), triu_ref[...])

    ti = lax.broadcasted_iota(jnp.int32, (L, L), 0)
    si = lax.broadcasted_iota(jnp.int32, (L, L), 1)
    causal = si <= ti

    for h in range(H_MLSTM):
        hs = slice(h * HEAD_DIM, (h + 1) * HEAD_DIM)
        qh = q_all[:, hs]
        kh = k_all[:, hs]
        vh = v_all[:, hs]
        bc = b_col_all[:, H_MLSTM + h:H_MLSTM + h + 1]
        br = b_row_all[h:h + 1, :]
        igr = gbt[h:h + 1, :]
        igc = gb[:, h:h + 1]
        m_prev = m_scr[h:h + 1, 0:1]
        c_prev = c_scr[h]
        n_prev = n_scr[h:h + 1, :]

        log_d = jnp.where(causal, bc - br + igr, NEG)
        m_inter = bc + m_prev
        m_t = jnp.maximum(m_inter, jnp.max(log_d, axis=-1, keepdims=True))
        qh_b = qh.astype(BF16)
        kh_b = kh.astype(BF16)
        sd = _dot_nt(qh_b, kh_b) * jnp.exp(log_d - m_t)
        scale_inter = jnp.exp(m_inter - m_t)
        num = scale_inter * _dot_nt(qh_b, c_prev.astype(BF16)) + _dot(sd.astype(BF16), vh.astype(BF16))
        den = scale_inter * jnp.sum(qh * n_prev, axis=-1, keepdims=True) + jnp.sum(sd, axis=-1, keepdims=True)
        hh = num / jnp.maximum(jnp.abs(den), jnp.exp(-m_t))

        m_new = m_t[L - 1:L, :]
        b_last = bc[L - 1:L, :]
        decay = jnp.exp(b_last + m_prev - m_new)
        w_row = jnp.exp(b_last - br + igr - m_new)
        w_col = jnp.exp(b_last - bc + igc - m_new)
        vtw = (vt_all[hs, :] * w_row).astype(BF16)
        c_new = decay * c_prev + _dot(vtw, kh_b)
        n_new = decay * n_prev + jnp.sum(w_col * kh, axis=0, keepdims=True)
        c_scr[h] = c_new
        n_scr[h:h + 1, :] = n_new
        m_scr[h:h + 1, :] = jnp.broadcast_to(m_new, (1, LANES))
        c_out_ref[0, h] = c_new
        n_out_ref[0, h:h + 1, :] = n_new
        m_out_ref[0, h:h + 1, :] = jnp.broadcast_to(m_new, (1, LANES))

        y = hh * lax.rsqrt(jnp.mean(hh * hh, axis=-1, keepdims=True) + EPS) * ng_ref[:, hs]
        mh_ref[0, :, hs] = _sigmoid(mo[:, hs]) * y


def _mlstm_prompt(mqk, mv, mo, gates, batch, seq, conv_w, conv_b, bg, ng, tril, triu):
    L = MLSTM_CHUNK
    nchunk = seq // L
    chunk = lambda b, c: (b, c, 0)
    fixed = lambda b, c: (0, 0)
    return pl.pallas_call(
        _mlstm_prompt_kernel,
        grid=(batch, nchunk),
        in_specs=[
            pl.BlockSpec((1, L, 2 * W_MLSTM), chunk),
            pl.BlockSpec((1, L, W_MLSTM), chunk),
            pl.BlockSpec((1, L, W_MLSTM), chunk),
            pl.BlockSpec((1, L, LANES), chunk),
            pl.BlockSpec((CONV_W, 2 * W_MLSTM), fixed),
            pl.BlockSpec((1, 2 * W_MLSTM), fixed),
            pl.BlockSpec((1, LANES), fixed),
            pl.BlockSpec((1, W_MLSTM), fixed),
            pl.BlockSpec((L, L), fixed),
            pl.BlockSpec((L, L), fixed),
        ],
        out_specs=[
            pl.BlockSpec((1, L, W_MLSTM), chunk),
            pl.BlockSpec((1, H_MLSTM, HEAD_DIM, HEAD_DIM), lambda b, c: (b, 0, 0, 0)),
            pl.BlockSpec((1, H_MLSTM, HEAD_DIM), lambda b, c: (b, 0, 0)),
            pl.BlockSpec((1, H_MLSTM, LANES), lambda b, c: (b, 0, 0)),
        ],
        out_shape=[
            jax.ShapeDtypeStruct((batch, seq, W_MLSTM), F32),
            jax.ShapeDtypeStruct((batch, H_MLSTM, HEAD_DIM, HEAD_DIM), F32),
            jax.ShapeDtypeStruct((batch, H_MLSTM, HEAD_DIM), F32),
            jax.ShapeDtypeStruct((batch, H_MLSTM, LANES), F32),
        ],
        scratch_shapes=[
            pltpu.VMEM((L + SUBLANES, 2 * W_MLSTM), F32),
            pltpu.VMEM((H_MLSTM, HEAD_DIM, HEAD_DIM), F32),
            pltpu.VMEM((H_MLSTM, HEAD_DIM), F32),
            pltpu.VMEM((H_MLSTM, LANES), F32),
        ],
        compiler_params=_cparams("parallel", "arbitrary"),
        name="mlstm_prompt",
    )(mqk.reshape(batch, seq, -1), mv.reshape(batch, seq, -1), mo.reshape(batch, seq, -1),
      gates.reshape(batch, seq, -1), conv_w, conv_b, bg, ng, tril, triu)


def _mlstm_sample_pre_kernel(mqk_ref, buf_ref, gates_ref, cw_ref, cb_ref, bg_ref, qk_ref, g_ref):
    conv = cb_ref[...] + cw_ref[CONV_W - 1:CONV_W, :] * mqk_ref[...]
    for j in range(CONV_W - 1):
        conv = conv + cw_ref[j:j + 1, :] * buf_ref[j]
    qk = conv * _sigmoid(conv)
    lane = lax.broadcasted_iota(jnp.int32, qk.shape, 1)
    qk_ref[...] = jnp.where(lane < W_MLSTM, qk, qk * (HEAD_DIM ** -0.5))
    gb = gates_ref[...] + bg_ref[...]
    glane = lax.broadcasted_iota(jnp.int32, gb.shape, 1)
    g_ref[...] = jnp.where(glane < H_MLSTM, gb, _log_sigmoid(gb))


def _mlstm_sample_step_kernel(q_ref, k_ref, v_ref, mo_ref, ig_ref, lf_ref, c0_ref, n0_ref, m0_ref, ng_ref,
                              mh_ref, c_ref, n_ref, m_ref):
    for h in range(H_MLSTM):
        q = q_ref[0, h]
        k = k_ref[0, h]
        v = v_ref[0, h]
        ig = ig_ref[0, h]
        lf = lf_ref[0, h]
        c0 = c0_ref[0, h]
        n0 = n0_ref[0, h]
        m0 = m0_ref[0, h]
        m_inter = lf + m0
        m_t = jnp.maximum(m_inter, ig)
        w_in = jnp.exp(ig - m_t)
        sd = jnp.sum(q * k, axis=-1, keepdims=True) * w_in
        scale_inter = jnp.exp(m_inter - m_t)
        num = scale_inter * jnp.sum(c0 * q, axis=-1, keepdims=True) + sd * v
        den = scale_inter * jnp.sum(n0 * q, axis=-1, keepdims=True) + sd
        hh = num / jnp.maximum(jnp.abs(den), jnp.exp(-m_t))
        decay = jnp.exp(lf + m0 - m_t)
        c_ref[0, h] = decay * c0 + w_in * (v * k)
        n_ref[0, h] = decay * n0 + w_in * k
        m_ref[0, h] = m_t
        y = hh * lax.rsqrt(jnp.mean(hh * hh, axis=0, keepdims=True) + EPS) * ng_ref[h]
        mh_ref[0, h] = _sigmoid(mo_ref[0, h]) * y


def _mlstm_sample(mqk, mv, mo, gates, conv_buf, c0, n0, m0, conv_w, conv_b, bg, ng):
    nb = mqk.shape[0]
    full = lambda *shape: pl.BlockSpec(shape, lambda: (0,) * len(shape))
    qk, g = pl.pallas_call(
        _mlstm_sample_pre_kernel,
        in_specs=[full(nb, 2 * W_MLSTM), full(CONV_W - 1, nb, 2 * W_MLSTM), full(nb, LANES),
                  full(CONV_W, 2 * W_MLSTM), full(1, 2 * W_MLSTM), full(1, LANES)],
        out_specs=[full(nb, 2 * W_MLSTM), full(nb, LANES)],
        out_shape=[jax.ShapeDtypeStruct((nb, 2 * W_MLSTM), F32), jax.ShapeDtypeStruct((nb, LANES), F32)],
        name="mlstm_sample_pre",
    )(mqk, jnp.swapaxes(conv_buf, 0, 1), gates, conv_w, conv_b, bg)
    row = (nb, H_MLSTM, 1, HEAD_DIM)
    col = (nb, H_MLSTM, HEAD_DIM, 1)
    one = (nb, H_MLSTM, 1, 1)
    mat = (nb, H_MLSTM, HEAD_DIM, HEAD_DIM)
    spec = lambda shape: pl.BlockSpec((1,) + shape[1:], lambda b: (b, 0, 0, 0))
    ins = [
        (qk[:, :W_MLSTM].reshape(row), row), (qk[:, W_MLSTM:].reshape(row), row), (mv.reshape(col), col),
        (mo.reshape(col), col), (g[:, :H_MLSTM].reshape(one), one), (g[:, H_MLSTM:2 * H_MLSTM].reshape(one), one),
        (c0, mat), (n0.reshape(row), row), (m0.reshape(one), one),
    ]
    mh, c, n, m = pl.pallas_call(
        _mlstm_sample_step_kernel,
        grid=(nb,),
        in_specs=[spec(s) for _, s in ins] + [pl.BlockSpec((H_MLSTM, HEAD_DIM, 1), lambda b: (0, 0, 0))],
        out_specs=[spec(col), spec(mat), spec(row), spec(one)],
        out_shape=[jax.ShapeDtypeStruct(s, F32) for s in (col, mat, row, one)],
        compiler_params=_cparams("parallel"),
        name="mlstm_sample_step",
    )(*[a for a, _ in ins], ng.reshape(H_MLSTM, HEAD_DIM, 1))
    return mh.reshape(nb, W_MLSTM), c, n.reshape(nb, H_MLSTM, HEAD_DIM), m.reshape(nb, H_MLSTM)


def _outproj_kernel(x_ref, o1_ref, l1_ref, o2_ref, l2_ref, o3_ref, l3_ref, mh_ref, w_ref, g_ref, x1_ref, hn_ref):
    l1, l2, l3 = l1_ref[...], l2_ref[...], l3_ref[...]
    m = jnp.maximum(jnp.maximum(l1, l2), l3)
    e1, e2, e3 = jnp.exp(l1 - m), jnp.exp(l2 - m), jnp.exp(l3 - m)
    att = (e1 * o1_ref[...] + e2 * o2_ref[...] + e3 * o3_ref[...]) / (e1 + e2 + e3)
    x1 = (x_ref[...] + _dot(att.astype(BF16), w_ref[0:W_ATT, :])
          + _dot(mh_ref[...].astype(BF16), w_ref[W_ATT:, :]))
    x1_ref[...] = x1
    ms = jnp.mean(x1 * x1, axis=-1, keepdims=True)
    hn_ref[...] = (x1 * lax.rsqrt(ms + EPS) * g_ref[...]).astype(BF16)


def _outproj_into_kernel(x1_any, hn_any, *refs):
    del x1_any, hn_any
    _outproj_kernel(*refs)


def _out_projection(x2d, branches, mh, w_out, g2, tm, n_out, out_rows, into=None):
    n_in = x2d.shape[0]
    in_blocks = n_in // tm
    first = out_rows[0] // tm
    row = lambda i: (i % in_blocks, 0)
    out_row = lambda i: (first + i, 0)
    fixed = lambda i: (0, 0)
    half = pl.BlockSpec((tm, W_ATT), row)
    in_specs = ([pl.BlockSpec((tm, D_MODEL), row)] + [half] * 7
                + [pl.BlockSpec((D_MODEL, D_MODEL), fixed), pl.BlockSpec((1, D_MODEL), fixed)])
    args = (x2d, *branches, mh, w_out, g2)
    body = _outproj_kernel
    aliases = {}
    if into is not None:
        in_specs = [pl.BlockSpec(memory_space=pl.ANY)] * 2 + in_specs
        args = (*into, *args)
        body = _outproj_into_kernel
        aliases = {0: 0, 1: 1}
    return pl.pallas_call(
        body,
        grid=((out_rows[1] - out_rows[0]) // tm,),
        in_specs=in_specs,
        out_specs=[pl.BlockSpec((tm, D_MODEL), out_row), pl.BlockSpec((tm, D_MODEL), out_row)],
        out_shape=[jax.ShapeDtypeStruct((n_out, D_MODEL), F32), jax.ShapeDtypeStruct((n_out, D_MODEL), BF16)],
        input_output_aliases=aliases,
        compiler_params=_cparams("parallel"),
        name="out_projection",
    )(*args)


def _peer_scores_kernel(hn_ref, wq_ref, g_ref, sk_ref, s_ref):
    q = _dot(hn_ref[...], wq_ref[...])
    half = PEER_DKEY // 2
    for h in range(PEER_HEADS):
        qh = q[:, h * PEER_DKEY:(h + 1) * PEER_DKEY]
        qn = qh * lax.rsqrt(jnp.mean(qh * qh, axis=-1, keepdims=True) + EPS) * g_ref[...]
        for part in range(2):
            s = _dot_nt(qn[:, part * half:(part + 1) * half].astype(BF16), sk_ref[2 * h + part])
            r0 = (2 * h + part) * PEER_NKEYS
            s_ref[r0:r0 + PEER_NKEYS, :] = s.T


def _peer_scores(hn, wq, g, sk, tm):
    n = hn.shape[0]
    rows = PEER_HEADS * 2 * PEER_NKEYS
    return pl.pallas_call(
        _peer_scores_kernel,
        grid=(n // tm,),
        in_specs=[pl.BlockSpec((tm, D_MODEL), lambda i: (i, 0)),
                  pl.BlockSpec(wq.shape, lambda i: (0, 0)),
                  pl.BlockSpec((1, PEER_DKEY), lambda i: (0, 0)),
                  pl.BlockSpec(sk.shape, lambda i: (0, 0, 0))],
        out_specs=pl.BlockSpec((rows, tm), lambda i: (0, i)),
        out_shape=jax.ShapeDtypeStruct((rows, n), F32),
        compiler_params=_cparams("parallel"),
        name="peer_scores",
    )(hn, wq, g, sk)


_STAIR = [(a, b) for a in range(PEER_TOPK) for b in range(PEER_TOPK) if (a + 1) * (b + 1) <= PEER_TOPK]


def _peer_select_kernel(s1_ref, s2_ref, rank2_ref, cnt1_ref, e1_ref, e2_ref,
                        work1_ref, work2_ref, rank1_ref, rank2s_ref, vals1_ref, vals2_ref, cnt_ref, cnt1s_ref):
    shape = work1_ref.shape
    keyf = lax.broadcasted_iota(jnp.int32, shape, 0).astype(F32)
    chunks = [slice(c * LANES, (c + 1) * LANES) for c in range(SUBLANES)]
    for c, cs in enumerate(chunks):
        work1_ref[:, c, :] = s1_ref[:, cs]
        work2_ref[:, c, :] = s2_ref[:, cs]
    rank1_ref[...] = jnp.full(shape, float(PEER_TOPK), F32)
    rank2s_ref[...] = jnp.full(shape, float(PEER_TOPK), F32)

    def extract(a, carry):
        for work_ref, rank_ref, vals_ref in ((work1_ref, rank1_ref, vals1_ref), (work2_ref, rank2s_ref, vals2_ref)):
            w = work_ref[...]
            mx = jnp.max(w, axis=0, keepdims=True)
            first = jnp.min(jnp.where(w == mx, keyf, float(PEER_NKEYS)), axis=0, keepdims=True)
            sel = keyf == first
            work_ref[...] = jnp.where(sel, -jnp.inf, w)
            rank_ref[...] = jnp.where(sel, a.astype(F32), rank_ref[...])
            vals_ref[pl.ds(a, 1)] = mx
        return carry

    lax.fori_loop(0, PEER_TOPK, extract, 0)

    v1 = [vals1_ref[a] for a in range(PEER_TOPK)]
    v2 = [vals2_ref[b] for b in range(PEER_TOPK)]
    cand = [v1[a] + v2[b] for a, b in _STAIR]
    pos = []
    for i, (a, b) in enumerate(_STAIR):
        static = sum(1 for (a2, b2) in _STAIR if a2 <= a and b2 <= b and (a2, b2) != (a, b))
        pos.append(jnp.full(cand[0].shape, float(static), F32))
    for i, (ai, bi) in enumerate(_STAIR):
        for j in range(i + 1, len(_STAIR)):
            aj, bj = _STAIR[j]
            if ai <= aj and bi <= bj:
                continue
            i_first = cand[i] >= cand[j]
            pos[j] = pos[j] + jnp.where(i_first, 1.0, 0.0)
            pos[i] = pos[i] + jnp.where(i_first, 0.0, 1.0)
    e1s = [jnp.exp(v1[a] - v1[0]) for a in range(PEER_TOPK)]
    e2s = [jnp.exp(v2[b] - v2[0]) for b in range(PEER_TOPK)]
    z = jnp.zeros_like(cand[0])
    for a in range(PEER_TOPK):
        cnt_a = jnp.zeros_like(z)
        za = jnp.zeros_like(z)
        for i, (a2, b) in enumerate(_STAIR):
            if a2 != a:
                continue
            chosen = pos[i] < float(PEER_TOPK)
            cnt_a = cnt_a + jnp.where(chosen, 1.0, 0.0)
            za = za + jnp.where(chosen, e2s[b], 0.0)
        cnt_ref[a] = cnt_a
        z = z + e1s[a] * za
    z_inv = 1.0 / z

    cnt1s_ref[...] = jnp.zeros(shape, F32)

    def spread(a, carry):
        cnt1s_ref[...] = jnp.where(rank1_ref[...] == a.astype(F32), cnt_ref[pl.ds(a, 1)], cnt1s_ref[...])
        return carry

    lax.fori_loop(0, PEER_TOPK, spread, 0)
    for c, cs in enumerate(chunks):
        rank2_ref[:, cs] = rank2s_ref[:, c, :].astype(BF16)
        cnt1_ref[:, cs] = cnt1s_ref[:, c, :]
        e1_ref[:, cs] = jnp.exp(s1_ref[:, cs] - v1[0][c:c + 1, :]) * z_inv[c:c + 1, :]
        e2_ref[:, cs] = jnp.exp(s2_ref[:, cs] - v2[0][c:c + 1, :]).astype(BF16)


def _peer_select(scores_t):
    n = scores_t.shape[1]
    tok = LANES * SUBLANES
    blk = (PEER_NKEYS, tok)
    blk3 = (PEER_NKEYS, SUBLANES, LANES)
    small = (PEER_TOPK, SUBLANES, LANES)
    out_rows = PEER_HEADS * PEER_NKEYS
    return pl.pallas_call(
        _peer_select_kernel,
        grid=(PEER_HEADS, n // tok),
        in_specs=[pl.BlockSpec(blk, lambda h, g: (2 * h, g)), pl.BlockSpec(blk, lambda h, g: (2 * h + 1, g))],
        out_specs=[pl.BlockSpec(blk, lambda h, g: (h, g))] * 4,
        out_shape=[jax.ShapeDtypeStruct((out_rows, n), dt) for dt in (BF16, F32, F32, BF16)],
        scratch_shapes=[pltpu.VMEM(blk3, F32)] * 4 + [pltpu.VMEM(small, F32)] * 3 + [pltpu.VMEM(blk3, F32)],
        compiler_params=_cparams("parallel", "parallel"),
        name="peer_select",
    )(scores_t, scores_t)


PEER_TOK_TILE = 512
PEER_I1_PER_STEP = 8
BF16_ROWS = 16


def _peer_dense_kernel(hn_ref, u_ref, vt_ref, rank2_ref, e2_ref, cnt1_ref, e1_ref, x1_ref, y_ref,
                       ht_ref, p_ref, acc_ref):
    j = pl.program_id(1)
    tokens = hn_ref.shape[0]
    reps = PEER_NKEYS // BF16_ROWS

    @pl.when(j == 0)
    def _():
        acc_ref[...] = jnp.zeros_like(acc_ref)
        ht_ref[...] = hn_ref[...].astype(F32).T.astype(BF16)

    a_all = _dot(u_ref[...], ht_ref[...])
    for s in range(PEER_I1_PER_STEP):
        i1 = j * PEER_I1_PER_STEP + s
        rows_s = slice(s * PEER_NKEYS, (s + 1) * PEER_NKEYS)
        a = a_all[rows_s, :]
        act = a * (0.5 + 0.5 * lax.erf(a * (2.0 ** -0.5)))
        g = None
        for h in range(PEER_HEADS):
            rows = slice(h * PEER_NKEYS, (h + 1) * PEER_NKEYS)
            row = pl.ds(h * PEER_NKEYS + i1, 1)
            cnt = jnp.broadcast_to(cnt1_ref[row, :], (BF16_ROWS, tokens)).astype(BF16)
            e1 = jnp.broadcast_to(e1_ref[row, :], (BF16_ROWS, tokens)).astype(BF16)
            cnt = jnp.concatenate([cnt] * reps, axis=0)
            e1 = jnp.concatenate([e1] * reps, axis=0)
            w = jnp.where(rank2_ref[rows, :] < cnt, e2_ref[rows, :] * e1, jnp.zeros((), BF16))
            g = w if g is None else g + w
        p_ref[rows_s, :] = g * act.astype(BF16)
    acc_ref[...] += _dot(vt_ref[...], p_ref[...])

    @pl.when(j == pl.num_programs(1) - 1)
    def _():
        y_ref[...] = x1_ref[...] + acc_ref[...].T


def _peer_dense(hn, u, vt, rank2, e2, cnt1, e1, x1):
    n = hn.shape[0]
    T = PEER_TOK_TILE
    eb = PEER_I1_PER_STEP * PEER_NKEYS
    nexp = u.shape[0]
    sel_rows = PEER_HEADS * PEER_NKEYS
    tok = lambda t, j: (0, t)
    return pl.pallas_call(
        _peer_dense_kernel,
        grid=(n // T, nexp // eb),
        in_specs=[
            pl.BlockSpec((T, D_MODEL), lambda t, j: (t, 0)),
            pl.BlockSpec((eb, D_MODEL), lambda t, j: (j, 0)),
            pl.BlockSpec((D_MODEL, eb), lambda t, j: (0, j)),
            pl.BlockSpec((sel_rows, T), tok),
            pl.BlockSpec((sel_rows, T), tok),
            pl.BlockSpec((sel_rows, T), tok),
            pl.BlockSpec((sel_rows, T), tok),
            pl.BlockSpec((T, D_MODEL), lambda t, j: (t, 0)),
        ],
        out_specs=pl.BlockSpec((T, D_MODEL), lambda t, j: (t, 0)),
        out_shape=jax.ShapeDtypeStruct((n, D_MODEL), F32),
        scratch_shapes=[pltpu.VMEM((D_MODEL, T), BF16), pltpu.VMEM((eb, T), BF16), pltpu.VMEM((D_MODEL, T), F32)],
        compiler_params=_cparams("parallel", "arbitrary"),
        name="peer_dense",
    )(hn, u, vt, rank2, e2, cnt1, e1, x1)


def _tri_constants():
    L = MLSTM_CHUNK
    tril = np.tril(np.ones((L, L), np.float32))
    return jnp.asarray(tril, BF16), jnp.asarray(tril.T, BF16)


def _head_constants():
    bd = np.kron(np.eye(H_ATT, dtype=np.float32), np.full((HEAD_DIM, HEAD_DIM), 1.0 / HEAD_DIM, np.float32))
    hsum = np.zeros((W_ATT, LANES), np.float32)
    hsum[np.arange(W_ATT), np.arange(W_ATT) // HEAD_DIM] = 1.0
    return jnp.asarray(bd, BF16), jnp.asarray(hsum, BF16), jnp.asarray(hsum.T, BF16)


def kernel(x_prompt, x_sample, cache_attn_k, cache_attn_v, state_mlstm_C, state_mlstm_n, state_mlstm_m,
           state_mlstm_conv, norm1_g, w_in, att_qnorm_g, att_knorm_g, b_gates, mlstm_conv_w, mlstm_conv_b,
           mlstm_norm_g, w_out, norm2_g, peer_w_query, peer_qnorm_g, peer_subkeys, peer_u, peer_v):
    batch, seq = x_prompt.shape[:2]
    nsamp = x_sample.shape[0]
    past_len = 16384
    assert norm1_g.shape[0] == 1 and x_sample.shape[1] == 1 and seq % MLSTM_CHUNK == 0
    wb = cache_attn_k.shape[2]
    li = 0
    bd, hsum, hexp = _head_constants()
    tril, triu = _tri_constants()

    w = w_in[li]
    gate_lo, gate_hi = 3072, 3072 + 2 * H_MLSTM
    w_main = jnp.concatenate([w[:, :gate_lo], w[:, gate_hi:]], axis=1).astype(BF16)
    wg = jnp.pad(w[:, gate_lo:gate_hi], ((0, 0), (0, LANES - 2 * H_MLSTM)))
    wg_hi = wg.astype(BF16)
    wg_lo = (wg - wg_hi.astype(F32)).astype(BF16)
    g1 = norm1_g[li][None, :]
    qg = jnp.tile(att_qnorm_g[li], H_ATT)[None, :]
    kg = jnp.tile(att_knorm_g[li], H_ATT)[None, :]
    bg = jnp.pad(b_gates[li], (0, LANES - 2 * H_MLSTM))[None, :]
    conv_w = mlstm_conv_w[li]
    conv_b = mlstm_conv_b[li][None, :]
    ng = mlstm_norm_g[li][None, :]
    wo = w_out[li].astype(BF16)
    g2 = norm2_g[li][None, :]
    wq = peer_w_query[li].astype(BF16)
    pqg = peer_qnorm_g[li][None, :]
    sk = peer_subkeys[li].reshape(PEER_HEADS * 2, PEER_NKEYS, PEER_DKEY // 2).astype(BF16)
    u_b = peer_u[li].astype(BF16)
    vt_b = peer_v[li].astype(BF16).T

    n_p = batch * seq
    xp2 = x_prompt.reshape(n_p, D_MODEL)
    pos_p = jnp.arange(seq, dtype=jnp.int32)
    q, k, v, mqk, mv, mo, gates = _in_projection(xp2, pos_p, 256, g1, w_main, wg_hi, wg_lo, qg, kg, bd)
    branches_p = []
    for _, dil in DILATIONS:
        branches_p.extend(_attn_branch_prompt(q, k, v, batch, seq, dil))
    mh_p, c_p, n_p_state, m_p = _mlstm_prompt(mqk, mv, mo, gates, batch, seq, conv_w, conv_b, bg, ng, tril, triu)
    n_all = n_p + nsamp
    group = LANES * SUBLANES
    n_pad = -(-n_all // group) * group
    x1_all, hn_all = _out_projection(xp2, branches_p, mh_p.reshape(n_p, W_MLSTM), wo, g2, 256, n_pad, (0, n_pad))
    wbp = min(wb, seq)
    new_k_prompt = k.reshape(batch, seq, H_ATT, HEAD_DIM)[None, :, seq - wbp:]
    new_v_prompt = v.reshape(batch, seq, H_ATT, HEAD_DIM)[None, :, seq - wbp:]
    new_conv_prompt = mqk.reshape(batch, seq, -1)[None, :, seq - (CONV_W - 1):]

    xs2 = x_sample.reshape(nsamp, D_MODEL)
    pos_s = jnp.full((nsamp,), past_len, dtype=jnp.int32)
    qs, ks, vs, mqk_s, mv_s, mo_s, gates_s = _in_projection(xs2, pos_s, nsamp, g1, w_main, wg_hi, wg_lo, qg, kg, bd)
    ck = cache_attn_k[li].reshape(nsamp, wb, W_ATT)
    cv = cache_attn_v[li].reshape(nsamp, wb, W_ATT)
    branches_s = _attn_sample(qs, ks, vs, ck, cv, hsum, hexp)
    mh_s, c_s, n_s, m_s = _mlstm_sample(mqk_s, mv_s, mo_s, gates_s, state_mlstm_conv[li], state_mlstm_C[li],
                                        state_mlstm_n[li], state_mlstm_m[li], conv_w, conv_b, bg, ng)
    x1_all, hn_all = _out_projection(xs2, branches_s, mh_s, wo, g2, nsamp, n_pad, (n_p, n_all),
                                     into=(x1_all, hn_all))
    new_conv_sample = jnp.concatenate([state_mlstm_conv[li][:, 1:], mqk_s[:, None, :]], axis=1)[None]

    scores_t = _peer_scores(hn_all, wq, pqg, sk, 256)
    rank2, cnt1, e1, e2 = _peer_select(scores_t)
    y_all = _peer_dense(hn_all, u_b, vt_b, rank2, e2, cnt1, e1, x1_all)
    y_prompt = y_all[:n_p].reshape(batch, seq, D_MODEL)
    y_sample = y_all[n_p:n_all].reshape(nsamp, 1, D_MODEL)

    return (y_prompt, y_sample, new_k_prompt, new_v_prompt,
            ks.reshape(1, nsamp, 1, H_ATT, HEAD_DIM), vs.reshape(1, nsamp, 1, H_ATT, HEAD_DIM),
            c_p[None], n_p_state[None], m_p[None, :, :, 0], new_conv_prompt,
            c_s[None], n_s[None], m_s[None], new_conv_sample)
```

```python
import functools

import numpy as np
import jax
import jax.numpy as jnp
from jax import lax
from jax.experimental import pallas as pl
from jax.experimental.pallas import tpu as pltpu

F32 = jnp.float32
BF16 = jnp.bfloat16

D_MODEL = 1024
HEAD_DIM = 64
W_ATT = 512
W_MLSTM = 512
H_ATT = 8
H_MLSTM = 8
DILATIONS = ((128, 1), (512, 4), (2048, 16))
ATT_BLOCK = 128
ROPE_THETA = 10000.0
CONV_W = 4
PEER_HEADS = 8
PEER_NKEYS = 128
PEER_DKEY = 256
PEER_TOPK = 16
EPS = 1e-6
NEG = -1e30

LANES = 128
SUBLANES = 8
VMEM_LIMIT = 56 * 1024 * 1024


def _cparams(*sem):
    return pltpu.CompilerParams(dimension_semantics=sem, vmem_limit_bytes=VMEM_LIMIT)


def _split2(x):
    hi = x.astype(BF16)
    lo = (x - hi.astype(F32)).astype(BF16)
    return hi, lo


def _split3(x):
    hi = x.astype(BF16)
    r = x - hi.astype(F32)
    mid = r.astype(BF16)
    lo = (r - mid.astype(F32)).astype(BF16)
    return hi, mid, lo


def _dot(a, b):
    return jnp.dot(a, b, preferred_element_type=F32)


def _dot_nt(a, b):
    return lax.dot_general(a, b, (((1,), (1,)), ((), ())), preferred_element_type=F32)


def _dot_parts(parts, b):
    acc = _dot(parts[0], b)
    for p in parts[1:]:
        acc = acc + _dot(p, b)
    return acc


def _sigmoid(x):
    return 1.0 / (1.0 + jnp.exp(-x))


def _log_sigmoid(x):
    return jnp.minimum(x, 0.0) - jnp.log1p(jnp.exp(-jnp.abs(x)))


def _inproj_kernel(x_ref, g_ref, w_ref, wgh_ref, wgl_ref, qg_ref, kg_ref, cos_ref, sin_ref, bd_ref,
                   q_ref, k_ref, v_ref, mqk_ref, mv_ref, mo_ref, gates_ref):
    x = x_ref[...]
    ms = jnp.mean(x * x, axis=-1, keepdims=True)
    xn = x * lax.rsqrt(ms + EPS) * g_ref[...]
    xh, xl = _split2(xn)

    def seg(lo, hi):
        return _dot(xh, w_ref[:, lo:hi])

    bd = bd_ref[...]
    cos = cos_ref[...]
    sin = sin_ref[...]
    lane = lax.broadcasted_iota(jnp.int32, cos.shape, 1)
    first_half = (lane % HEAD_DIM) < (HEAD_DIM // 2)

    def head_norm_rope(a, g):
        sq = a * a
        hi, lo = _split2(sq)
        msq = _dot(hi, bd) + _dot(lo, bd)
        y = a * lax.rsqrt(msq + EPS) * g
        rot = jnp.where(first_half, pltpu.roll(y, W_ATT - HEAD_DIM // 2, 1), pltpu.roll(y, HEAD_DIM // 2, 1))
        return y * cos + rot * sin

    q_ref[...] = head_norm_rope(seg(0, 512), qg_ref[...]) * (HEAD_DIM ** -0.5)
    k_ref[...] = head_norm_rope(seg(512, 1024), kg_ref[...])
    v_ref[...] = seg(1024, 1536)
    mqk_ref[...] = seg(1536, 2560)
    mv_ref[...] = seg(2560, 3072)
    mo_ref[...] = seg(3072, 3584)
    gates_ref[...] = _dot(xh, wgh_ref[...]) + _dot(xl, wgh_ref[...]) + _dot(xh, wgl_ref[...])


def _rope_tables(pos):
    half = HEAD_DIM // 2
    inv = ROPE_THETA ** (-jnp.arange(half, dtype=F32) / half)
    ang = pos.astype(F32)[:, None] * inv[None, :]
    cos = jnp.cos(ang)
    sin = jnp.sin(ang)
    cos_h = jnp.concatenate([cos, cos], axis=-1)
    sin_h = jnp.concatenate([-sin, sin], axis=-1)
    return jnp.tile(cos_h, (1, H_ATT)), jnp.tile(sin_h, (1, H_ATT))


def _in_projection(x2d, pos, tm, norm_g, w_main, wg_hi, wg_lo, qg, kg, bd):
    n = x2d.shape[0]
    cos, sin = _rope_tables(pos)
    pblocks = pos.shape[0] // tm
    row = lambda i: (i, 0)
    fixed = lambda i: (0, 0)
    tab = lambda i: (i % pblocks, 0)
    widths = (512, 512, 512, 1024, 512, 512, LANES)
    return pl.pallas_call(
        _inproj_kernel,
        grid=(n // tm,),
        in_specs=[
            pl.BlockSpec((tm, D_MODEL), row),
            pl.BlockSpec((1, D_MODEL), fixed),
            pl.BlockSpec(w_main.shape, fixed),
            pl.BlockSpec(wg_hi.shape, fixed),
            pl.BlockSpec(wg_lo.shape, fixed),
            pl.BlockSpec((1, W_ATT), fixed),
            pl.BlockSpec((1, W_ATT), fixed),
            pl.BlockSpec((tm, W_ATT), tab),
            pl.BlockSpec((tm, W_ATT), tab),
            pl.BlockSpec((W_ATT, W_ATT), fixed),
        ],
        out_specs=[pl.BlockSpec((tm, w), row) for w in widths],
        out_shape=[jax.ShapeDtypeStruct((n, w), F32) for w in widths],
        compiler_params=_cparams("parallel"),
        name="in_projection",
    )(x2d, norm_g, w_main, wg_hi, wg_lo, qg, kg, cos, sin, bd)


def _merge_branches(outs, lses):
    m = functools.reduce(jnp.maximum, lses)
    es = [jnp.exp(l - m) for l in lses]
    return sum(e * o for e, o in zip(es, outs)) / sum(es)


def _attn_prompt_kernel(q_ref, k_ref, v_ref, att_ref, o_scr, l_scr):
    seq = q_ref.shape[1]
    blk = ATT_BLOCK
    lane = lax.broadcasted_iota(jnp.int32, (1, LANES), 1)
    ii = lax.broadcasted_iota(jnp.int32, (blk, 2 * blk), 0)
    jj = lax.broadcasted_iota(jnp.int32, (blk, 2 * blk), 1)
    mask_prev = jnp.minimum(jj - ii, ii + blk - jj) >= 0
    mask_first = lax.broadcasted_iota(jnp.int32, (blk, blk), 1) <= lax.broadcasted_iota(jnp.int32, (blk, blk), 0)

    def block(branch, dil, start, has_prev):
        def rows(at):
            return pl.ds(at, blk, stride=dil) if dil > 1 else pl.ds(at, blk)

        q = q_ref[0, rows(start), :]
        k2 = k_ref[0, rows(start), :]
        v2 = v_ref[0, rows(start), :]
        mask = mask_first
        if has_prev:
            before = start - blk * dil
            k2 = jnp.concatenate([k_ref[0, rows(before), :], k2], axis=0)
            v2 = jnp.concatenate([v_ref[0, rows(before), :], v2], axis=0)
            mask = mask_prev
        k2 = k2.astype(BF16)
        v2 = v2.astype(BF16)
        o_pair = None
        l_pair = None
        for sub in range(LANES // HEAD_DIM):
            in_head = (lane // HEAD_DIM) == sub
            qm = jnp.where(in_head, q, 0.0).astype(BF16)
            s = jnp.where(mask, _dot_nt(qm, k2), NEG)
            m = jnp.max(s, axis=-1, keepdims=True)
            p = jnp.exp(s - m)
            l = jnp.sum(p, axis=-1, keepdims=True)
            o = _dot((p / l).astype(BF16), v2)
            lse = jnp.broadcast_to(m + jnp.log(l), o.shape)
            o_pair = o if o_pair is None else jnp.where(in_head, o, o_pair)
            l_pair = lse if l_pair is None else jnp.where(in_head, lse, l_pair)
        o_scr[branch, rows(start), :] = o_pair
        l_scr[branch, rows(start), :] = l_pair

    for branch, (_, dil) in enumerate(DILATIONS):
        nblocks = seq // dil // blk

        def residue(r, carry, branch=branch, dil=dil, nblocks=nblocks):
            block(branch, dil, r, False)

            def later(n, c):
                at = n * (blk * dil)
                block(branch, dil, pl.multiple_of(at, blk) if dil == 1 else r + at, True)
                return c

            if nblocks > 1:
                lax.fori_loop(1, nblocks, later, 0)
            return carry

        if dil == 1:
            residue(0, 0)
        else:
            lax.fori_loop(0, dil, residue, 0)

    nbr = len(DILATIONS)
    att_ref[0] = _merge_branches([o_scr[i] for i in range(nbr)], [l_scr[i] for i in range(nbr)])


def _attn_prompt(q, k, v, batch, seq):
    shape3 = (batch, seq, W_ATT)
    blk = pl.BlockSpec((1, seq, LANES), lambda b, p: (b, 0, p))
    att = pl.pallas_call(
        _attn_prompt_kernel,
        grid=(batch, W_ATT // LANES),
        in_specs=[blk, blk, blk],
        out_specs=blk,
        out_shape=jax.ShapeDtypeStruct(shape3, F32),
        scratch_shapes=[pltpu.VMEM((len(DILATIONS), seq, LANES), F32)] * 2,
        compiler_params=_cparams("parallel", "parallel"),
        name="attn_prompt",
    )(q.reshape(shape3), k.reshape(shape3), v.reshape(shape3))
    return att.reshape(batch * seq, W_ATT)


def _attn_sample_kernel(q_ref, kn_ref, vn_ref, k1_ref, v1_ref, k4_ref, v4_ref, k16_ref, v16_ref, ones_ref, att_ref):
    q = q_ref[0]
    kn = kn_ref[0]
    vn = vn_ref[0]
    ones = ones_ref[...]

    def lane_sum(a):
        return _dot_parts(_split3(a), ones)

    s_new = lane_sum(q * kn)
    outs, lses = [], []
    for kc_ref, vc_ref in ((k1_ref, v1_ref), (k4_ref, v4_ref), (k16_ref, v16_ref)):
        kc = kc_ref[...].reshape(ATT_BLOCK, H_ATT, HEAD_DIM)
        vc = vc_ref[...].reshape(ATT_BLOCK, H_ATT, HEAD_DIM)
        s = lane_sum((kc * q[None]).reshape(ATT_BLOCK * H_ATT, HEAD_DIM)).reshape(ATT_BLOCK, H_ATT, HEAD_DIM)
        m = jnp.maximum(jnp.max(s, axis=0), s_new)
        p = jnp.exp(s - m[None])
        p_new = jnp.exp(s_new - m)
        l = jnp.sum(p, axis=0) + p_new
        outs.append((jnp.sum(p * vc, axis=0) + p_new * vn) / l)
        lses.append(m + jnp.log(l))
    att_ref[0] = _merge_branches(outs, lses)


def _attn_sample(q, k_new, v_new, cache_k, cache_v, ones):
    nb, wb = cache_k.shape[0], cache_k.shape[1]
    span = ATT_BLOCK
    vec = (nb, H_ATT, HEAD_DIM)
    in_specs = [pl.BlockSpec((1, H_ATT, HEAD_DIM), lambda b: (b, 0, 0))] * 3
    args = [t.reshape(vec) for t in (q, k_new, v_new)]
    for _, dil in DILATIONS:
        groups = wb // dil
        assert groups % span == 0
        last = groups // span - 1
        for c in (cache_k, cache_v):
            args.append(c.reshape(nb, groups, dil, H_ATT, HEAD_DIM))
            in_specs.append(pl.BlockSpec((1, span, 1, H_ATT, HEAD_DIM),
                                         functools.partial(lambda b, blk: (b, blk, 0, 0, 0), blk=last)))
    in_specs.append(pl.BlockSpec(ones.shape, lambda b: (0, 0)))
    args.append(ones)
    att = pl.pallas_call(
        _attn_sample_kernel,
        grid=(nb,),
        in_specs=in_specs,
        out_specs=pl.BlockSpec((1, H_ATT, HEAD_DIM), lambda b: (b, 0, 0)),
        out_shape=jax.ShapeDtypeStruct(vec, F32),
        compiler_params=_cparams("parallel"),
        name="attn_sample",
    )(*args)
    return att.reshape(nb, W_ATT)


MLSTM_CHUNK = 128


def _mlstm_prompt_kernel(mqk_ref, mv_ref, mo_ref, gates_ref, cw_ref, cb_ref, bg_ref, ng_ref, tril_ref, triu_ref,
                         mh_ref, c_out_ref, n_out_ref, m_out_ref,
                         conv_scr, c_scr, n_scr, m_scr):
    L = MLSTM_CHUNK
    c_idx = pl.program_id(1)

    @pl.when(c_idx == 0)
    def _():
        conv_scr[0:SUBLANES, :] = jnp.zeros((SUBLANES, 2 * W_MLSTM), F32)
        c_scr[...] = jnp.zeros_like(c_scr)
        n_scr[...] = jnp.zeros_like(n_scr)
        m_scr[...] = jnp.zeros_like(m_scr)

    x = mqk_ref[0]
    conv_scr[SUBLANES:SUBLANES + L, :] = x
    conv = cb_ref[...] + cw_ref[CONV_W - 1:CONV_W, :] * x
    for j in range(CONV_W - 1):
        shift = CONV_W - 1 - j
        conv = conv + cw_ref[j:j + 1, :] * conv_scr[SUBLANES - shift:SUBLANES - shift + L, :]
    conv_scr[0:SUBLANES, :] = x[L - SUBLANES:L, :]
    qk = conv * _sigmoid(conv)
    q_all = qk[:, :W_MLSTM]
    k_all = qk[:, W_MLSTM:] * (HEAD_DIM ** -0.5)
    v_all = mv_ref[0]
    vt_all = v_all.T
    mo = mo_ref[0]

    gb = gates_ref[0] + bg_ref[...]
    gbt = gb.T
    lf_col = _log_sigmoid(gb)
    lf_row = _log_sigmoid(gbt[H_MLSTM:2 * H_MLSTM, :])
    hi, mid, lo = _split3(lf_col)
    tril = tril_ref[...]
    b_col_all = _dot(tril, hi) + _dot(tril, mid) + _dot(tril, lo)
    b_row_all = _dot_parts(_split3(lf_row), triu_ref[...])

    ti = lax.broadcasted_iota(jnp.int32, (L, L), 0)
    si = lax.broadcasted_iota(jnp.int32, (L, L), 1)
    causal = si <= ti

    for h in range(H_MLSTM):
        hs = slice(h * HEAD_DIM, (h + 1) * HEAD_DIM)
        qh = q_all[:, hs]
        kh = k_all[:, hs]
        vh = v_all[:, hs]
        bc = b_col_all[:, H_MLSTM + h:H_MLSTM + h + 1]
        br = b_row_all[h:h + 1, :]
        igr = gbt[h:h + 1, :]
        igc = gb[:, h:h + 1]
        m_prev = m_scr[h:h + 1, 0:1]
        c_prev = c_scr[h]
        n_prev = n_scr[h:h + 1, :]

        log_d = jnp.where(causal, bc - br + igr, NEG)
        m_inter = bc + m_prev
        m_t = jnp.maximum(m_inter, jnp.max(log_d, axis=-1, keepdims=True))
        qh_b = qh.astype(BF16)
        kh_b = kh.astype(BF16)
        sd = _dot_nt(qh_b, kh_b) * jnp.exp(log_d - m_t)
        scale_inter = jnp.exp(m_inter - m_t)
        num = scale_inter * _dot_nt(qh_b, c_prev.astype(BF16)) + _dot(sd.astype(BF16), vh.astype(BF16))
        den = scale_inter * jnp.sum(qh * n_prev, axis=-1, keepdims=True) + jnp.sum(sd, axis=-1, keepdims=True)
        hh = num / jnp.maximum(jnp.abs(den), jnp.exp(-m_t))

        m_new = m_t[L - 1:L, :]
        b_last = bc[L - 1:L, :]
        decay = jnp.exp(b_last + m_prev - m_new)
        w_row = jnp.exp(b_last - br + igr - m_new)
        w_col = jnp.exp(b_last - bc + igc - m_new)
        vtw = (vt_all[hs, :] * w_row).astype(BF16)
        c_new = decay * c_prev + _dot(vtw, kh_b)
        n_new = decay * n_prev + jnp.sum(w_col * kh, axis=0, keepdims=True)
        c_scr[h] = c_new
        n_scr[h:h + 1, :] = n_new
        m_scr[h:h + 1, :] = jnp.broadcast_to(m_new, (1, LANES))
        c_out_ref[0, h] = c_new
        n_out_ref[0, h:h + 1, :] = n_new
        m_out_ref[0, h:h + 1, :] = jnp.broadcast_to(m_new, (1, LANES))

        y = hh * lax.rsqrt(jnp.mean(hh * hh, axis=-1, keepdims=True) + EPS) * ng_ref[:, hs]
        mh_ref[0, :, hs] = _sigmoid(mo[:, hs]) * y


def _mlstm_prompt(mqk, mv, mo, gates, batch, seq, conv_w, conv_b, bg, ng, tril, triu):
    L = MLSTM_CHUNK
    nchunk = seq // L
    chunk = lambda b, c: (b, c, 0)
    fixed = lambda b, c: (0, 0)
    return pl.pallas_call(
        _mlstm_prompt_kernel,
        grid=(batch, nchunk),
        in_specs=[
            pl.BlockSpec((1, L, 2 * W_MLSTM), chunk),
            pl.BlockSpec((1, L, W_MLSTM), chunk),
            pl.BlockSpec((1, L, W_MLSTM), chunk),
            pl.BlockSpec((1, L, LANES), chunk),
            pl.BlockSpec((CONV_W, 2 * W_MLSTM), fixed),
            pl.BlockSpec((1, 2 * W_MLSTM), fixed),
            pl.BlockSpec((1, LANES), fixed),
            pl.BlockSpec((1, W_MLSTM), fixed),
            pl.BlockSpec((L, L), fixed),
            pl.BlockSpec((L, L), fixed),
        ],
        out_specs=[
            pl.BlockSpec((1, L, W_MLSTM), chunk),
            pl.BlockSpec((1, H_MLSTM, HEAD_DIM, HEAD_DIM), lambda b, c: (b, 0, 0, 0)),
            pl.BlockSpec((1, H_MLSTM, HEAD_DIM), lambda b, c: (b, 0, 0)),
            pl.BlockSpec((1, H_MLSTM, LANES), lambda b, c: (b, 0, 0)),
        ],
        out_shape=[
            jax.ShapeDtypeStruct((batch, seq, W_MLSTM), F32),
            jax.ShapeDtypeStruct((batch, H_MLSTM, HEAD_DIM, HEAD_DIM), F32),
            jax.ShapeDtypeStruct((batch, H_MLSTM, HEAD_DIM), F32),
            jax.ShapeDtypeStruct((batch, H_MLSTM, LANES), F32),
        ],
        scratch_shapes=[
            pltpu.VMEM((L + SUBLANES, 2 * W_MLSTM), F32),
            pltpu.VMEM((H_MLSTM, HEAD_DIM, HEAD_DIM), F32),
            pltpu.VMEM((H_MLSTM, HEAD_DIM), F32),
            pltpu.VMEM((H_MLSTM, LANES), F32),
        ],
        compiler_params=_cparams("parallel", "arbitrary"),
        name="mlstm_prompt",
    )(mqk.reshape(batch, seq, -1), mv.reshape(batch, seq, -1), mo.reshape(batch, seq, -1),
      gates.reshape(batch, seq, -1), conv_w, conv_b, bg, ng, tril, triu)


def _mlstm_sample_pre_kernel(mqk_ref, buf_ref, gates_ref, cw_ref, cb_ref, bg_ref, qk_ref, g_ref):
    conv = cb_ref[...] + cw_ref[CONV_W - 1:CONV_W, :] * mqk_ref[...]
    for j in range(CONV_W - 1):
        conv = conv + cw_ref[j:j + 1, :] * buf_ref[j]
    qk = conv * _sigmoid(conv)
    lane = lax.broadcasted_iota(jnp.int32, qk.shape, 1)
    qk_ref[...] = jnp.where(lane < W_MLSTM, qk, qk * (HEAD_DIM ** -0.5))
    gb = gates_ref[...] + bg_ref[...]
    glane = lax.broadcasted_iota(jnp.int32, gb.shape, 1)
    g_ref[...] = jnp.where(glane < H_MLSTM, gb, _log_sigmoid(gb))


def _mlstm_sample_step_kernel(q_ref, k_ref, v_ref, mo_ref, ig_ref, lf_ref, c0_ref, n0_ref, m0_ref, ng_ref,
                              mh_ref, c_ref, n_ref, m_ref):
    for h in range(H_MLSTM):
        q = q_ref[0, h]
        k = k_ref[0, h]
        v = v_ref[0, h]
        ig = ig_ref[0, h]
        lf = lf_ref[0, h]
        c0 = c0_ref[0, h]
        n0 = n0_ref[0, h]
        m0 = m0_ref[0, h]
        m_inter = lf + m0
        m_t = jnp.maximum(m_inter, ig)
        w_in = jnp.exp(ig - m_t)
        sd = jnp.sum(q * k, axis=-1, keepdims=True) * w_in
        scale_inter = jnp.exp(m_inter - m_t)
        num = scale_inter * jnp.sum(c0 * q, axis=-1, keepdims=True) + sd * v
        den = scale_inter * jnp.sum(n0 * q, axis=-1, keepdims=True) + sd
        hh = num / jnp.maximum(jnp.abs(den), jnp.exp(-m_t))
        decay = jnp.exp(lf + m0 - m_t)
        c_ref[0, h] = decay * c0 + w_in * (v * k)
        n_ref[0, h] = decay * n0 + w_in * k
        m_ref[0, h] = m_t
        y = hh * lax.rsqrt(jnp.mean(hh * hh, axis=0, keepdims=True) + EPS) * ng_ref[h]
        mh_ref[0, h] = _sigmoid(mo_ref[0, h]) * y


def _mlstm_sample(mqk, mv, mo, gates, conv_buf, c0, n0, m0, conv_w, conv_b, bg, ng):
    nb = mqk.shape[0]
    full = lambda *shape: pl.BlockSpec(shape, lambda: (0,) * len(shape))
    qk, g = pl.pallas_call(
        _mlstm_sample_pre_kernel,
        in_specs=[full(nb, 2 * W_MLSTM), full(CONV_W - 1, nb, 2 * W_MLSTM), full(nb, LANES),
                  full(CONV_W, 2 * W_MLSTM), full(1, 2 * W_MLSTM), full(1, LANES)],
        out_specs=[full(nb, 2 * W_MLSTM), full(nb, LANES)],
        out_shape=[jax.ShapeDtypeStruct((nb, 2 * W_MLSTM), F32), jax.ShapeDtypeStruct((nb, LANES), F32)],
        name="mlstm_sample_pre",
    )(mqk, jnp.swapaxes(conv_buf, 0, 1), gates, conv_w, conv_b, bg)
    row = (nb, H_MLSTM, 1, HEAD_DIM)
    col = (nb, H_MLSTM, HEAD_DIM, 1)
    one = (nb, H_MLSTM, 1, 1)
    mat = (nb, H_MLSTM, HEAD_DIM, HEAD_DIM)
    spec = lambda shape: pl.BlockSpec((1,) + shape[1:], lambda b: (b, 0, 0, 0))
    ins = [
        (qk[:, :W_MLSTM].reshape(row), row), (qk[:, W_MLSTM:].reshape(row), row), (mv.reshape(col), col),
        (mo.reshape(col), col), (g[:, :H_MLSTM].reshape(one), one), (g[:, H_MLSTM:2 * H_MLSTM].reshape(one), one),
        (c0, mat), (n0.reshape(row), row), (m0.reshape(one), one),
    ]
    mh, c, n, m = pl.pallas_call(
        _mlstm_sample_step_kernel,
        grid=(nb,),
        in_specs=[spec(s) for _, s in ins] + [pl.BlockSpec((H_MLSTM, HEAD_DIM, 1), lambda b: (0, 0, 0))],
        out_specs=[spec(col), spec(mat), spec(row), spec(one)],
        out_shape=[jax.ShapeDtypeStruct(s, F32) for s in (col, mat, row, one)],
        compiler_params=_cparams("parallel"),
        name="mlstm_sample_step",
    )(*[a for a, _ in ins], ng.reshape(H_MLSTM, HEAD_DIM, 1))
    return mh.reshape(nb, W_MLSTM), c, n.reshape(nb, H_MLSTM, HEAD_DIM), m.reshape(nb, H_MLSTM)


def _outproj_kernel(x_ref, att_ref, mh_ref, w_ref, g_ref, x1_ref, hn_ref):
    x1 = (x_ref[...] + _dot(att_ref[...].astype(BF16), w_ref[0:W_ATT, :])
          + _dot(mh_ref[...].astype(BF16), w_ref[W_ATT:, :]))
    x1_ref[...] = x1
    ms = jnp.mean(x1 * x1, axis=-1, keepdims=True)
    hn_ref[...] = (x1 * lax.rsqrt(ms + EPS) * g_ref[...]).astype(BF16)


def _outproj_into_kernel(x1_any, hn_any, *refs):
    del x1_any, hn_any
    _outproj_kernel(*refs)


def _out_projection(x2d, att, mh, w_out, g2, tm, n_out, out_rows, into=None):
    n_in = x2d.shape[0]
    in_blocks = n_in // tm
    first = out_rows[0] // tm
    row = lambda i: (i % in_blocks, 0)
    out_row = lambda i: (first + i, 0)
    fixed = lambda i: (0, 0)
    half = pl.BlockSpec((tm, W_ATT), row)
    in_specs = [pl.BlockSpec((tm, D_MODEL), row), half, half,
                pl.BlockSpec((D_MODEL, D_MODEL), fixed), pl.BlockSpec((1, D_MODEL), fixed)]
    args = (x2d, att, mh, w_out, g2)
    body = _outproj_kernel
    aliases = {}
    if into is not None:
        in_specs = [pl.BlockSpec(memory_space=pl.ANY)] * 2 + in_specs
        args = (*into, *args)
        body = _outproj_into_kernel
        aliases = {0: 0, 1: 1}
    return pl.pallas_call(
        body,
        grid=((out_rows[1] - out_rows[0]) // tm,),
        in_specs=in_specs,
        out_specs=[pl.BlockSpec((tm, D_MODEL), out_row), pl.BlockSpec((tm, D_MODEL), out_row)],
        out_shape=[jax.ShapeDtypeStruct((n_out, D_MODEL), F32), jax.ShapeDtypeStruct((n_out, D_MODEL), BF16)],
        input_output_aliases=aliases,
        compiler_params=_cparams("parallel"),
        name="out_projection",
    )(*args)


def _peer_scores_kernel(hn_ref, wq_ref, g_ref, sk_ref, s_ref):
    q = _dot(hn_ref[...], wq_ref[...])
    half = PEER_DKEY // 2
    for h in range(PEER_HEADS):
        qh = q[:, h * PEER_DKEY:(h + 1) * PEER_DKEY]
        qn = qh * lax.rsqrt(jnp.mean(qh * qh, axis=-1, keepdims=True) + EPS) * g_ref[...]
        for part in range(2):
            s = _dot_nt(qn[:, part * half:(part + 1) * half].astype(BF16), sk_ref[2 * h + part])
            r0 = (2 * h + part) * PEER_NKEYS
            s_ref[r0:r0 + PEER_NKEYS, :] = s.T


def _peer_scores(hn, wq, g, sk, tm):
    n = hn.shape[0]
    rows = PEER_HEADS * 2 * PEER_NKEYS
    return pl.pallas_call(
        _peer_scores_kernel,
        grid=(n // tm,),
        in_specs=[pl.BlockSpec((tm, D_MODEL), lambda i: (i, 0)),
                  pl.BlockSpec(wq.shape, lambda i: (0, 0)),
                  pl.BlockSpec((1, PEER_DKEY), lambda i: (0, 0)),
                  pl.BlockSpec(sk.shape, lambda i: (0, 0, 0))],
        out_specs=pl.BlockSpec((rows, tm), lambda i: (0, i)),
        out_shape=jax.ShapeDtypeStruct((rows, n), F32),
        compiler_params=_cparams("parallel"),
        name="peer_scores",
    )(hn, wq, g, sk)


_STAIR = [(a, b) for a in range(PEER_TOPK) for b in range(PEER_TOPK) if (a + 1) * (b + 1) <= PEER_TOPK]


def _peer_select_kernel(s1_ref, s2_ref, rank2_ref, cnt1_ref, e1_ref, e2_ref,
                        work1_ref, work2_ref, rank1_ref, rank2s_ref, vals1_ref, vals2_ref, cnt_ref, cnt1s_ref):
    shape = work1_ref.shape
    keyf = lax.broadcasted_iota(jnp.int32, shape, 0).astype(F32)
    chunks = [slice(c * LANES, (c + 1) * LANES) for c in range(SUBLANES)]
    for c, cs in enumerate(chunks):
        work1_ref[:, c, :] = s1_ref[:, cs]
        work2_ref[:, c, :] = s2_ref[:, cs]
    rank1_ref[...] = jnp.full(shape, float(PEER_TOPK), F32)
    rank2s_ref[...] = jnp.full(shape, float(PEER_TOPK), F32)

    def extract(a, carry):
        for work_ref, rank_ref, vals_ref in ((work1_ref, rank1_ref, vals1_ref), (work2_ref, rank2s_ref, vals2_ref)):
            w = work_ref[...]
            mx = jnp.max(w, axis=0, keepdims=True)
            first = jnp.min(jnp.where(w == mx, keyf, float(PEER_NKEYS)), axis=0, keepdims=True)
            sel = keyf == first
            work_ref[...] = jnp.where(sel, -jnp.inf, w)
            rank_ref[...] = jnp.where(sel, jnp.asarray(a, F32), rank_ref[...])
            vals_ref[pl.ds(a, 1)] = mx
        return carry

    lax.fori_loop(0, PEER_TOPK, extract, 0)

    v1 = [vals1_ref[a] for a in range(PEER_TOPK)]
    v2 = [vals2_ref[b] for b in range(PEER_TOPK)]
    cand = [v1[a] + v2[b] for a, b in _STAIR]
    pos = []
    for i, (a, b) in enumerate(_STAIR):
        static = sum(1 for (a2, b2) in _STAIR if a2 <= a and b2 <= b and (a2, b2) != (a, b))
        pos.append(jnp.full(cand[0].shape, float(static), F32))
    for i, (ai, bi) in enumerate(_STAIR):
        for j in range(i + 1, len(_STAIR)):
            aj, bj = _STAIR[j]
            if ai <= aj and bi <= bj:
                continue
            i_first = cand[i] >= cand[j]
            pos[j] = pos[j] + jnp.where(i_first, 1.0, 0.0)
            pos[i] = pos[i] + jnp.where(i_first, 0.0, 1.0)
    e1s = [jnp.exp(v1[a] - v1[0]) for a in range(PEER_TOPK)]
    e2s = [jnp.exp(v2[b] - v2[0]) for b in range(PEER_TOPK)]
    z = jnp.zeros_like(cand[0])
    for a in range(PEER_TOPK):
        cnt_a = jnp.zeros_like(z)
        za = jnp.zeros_like(z)
        for i, (a2, b) in enumerate(_STAIR):
            if a2 != a:
                continue
            chosen = pos[i] < float(PEER_TOPK)
            cnt_a = cnt_a + jnp.where(chosen, 1.0, 0.0)
            za = za + jnp.where(chosen, e2s[b], 0.0)
        cnt_ref[a] = cnt_a
        z = z + e1s[a] * za
    z_inv = 1.0 / z

    cnt1s_ref[...] = jnp.zeros(shape, F32)

    def spread(a, carry):
        cnt1s_ref[...] = jnp.where(rank1_ref[...] == jnp.asarray(a, F32), cnt_ref[pl.ds(a, 1)], cnt1s_ref[...])
        return carry

    lax.fori_loop(0, PEER_TOPK, spread, 0)
    for c, cs in enumerate(chunks):
        rank2_ref[:, cs] = rank2s_ref[:, c, :].astype(BF16)
        cnt1_ref[:, cs] = cnt1s_ref[:, c, :]
        e1_ref[:, cs] = jnp.exp(s1_ref[:, cs] - v1[0][c:c + 1, :]) * z_inv[c:c + 1, :]
        e2_ref[:, cs] = jnp.exp(s2_ref[:, cs] - v2[0][c:c + 1, :]).astype(BF16)


def _peer_select(scores_t):
    n = scores_t.shape[1]
    tok = LANES * SUBLANES
    blk = (PEER_NKEYS, tok)
    blk3 = (PEER_NKEYS, SUBLANES, LANES)
    small = (PEER_TOPK, SUBLANES, LANES)
    out_rows = PEER_HEADS * PEER_NKEYS
    return pl.pallas_call(
        _peer_select_kernel,
        grid=(PEER_HEADS, n // tok),
        in_specs=[pl.BlockSpec(blk, lambda h, g: (2 * h, g)), pl.BlockSpec(blk, lambda h, g: (2 * h + 1, g))],
        out_specs=[pl.BlockSpec(blk, lambda h, g: (h, g))] * 4,
        out_shape=[jax.ShapeDtypeStruct((out_rows, n), dt) for dt in (BF16, F32, F32, BF16)],
        scratch_shapes=[pltpu.VMEM(blk3, F32)] * 4 + [pltpu.VMEM(small, F32)] * 3 + [pltpu.VMEM(blk3, F32)],
        compiler_params=_cparams("parallel", "parallel"),
        name="peer_select",
    )(scores_t, scores_t)


PEER_TOK_TILE = 512
PEER_I1_PER_STEP = 8
PEER_MXU_CHUNKS = 4
BF16_ROWS = 16


def _peer_dense_kernel(nblk, hn_ref, u_ref, vt_prev_ref, vt_last_ref, rank2_ref, e2_ref, cnt1_ref, e1_ref, x1_ref, y_ref,
                       ht_ref, p_even_ref, p_odd_ref, acc_ref):
    j = pl.program_id(1)
    tokens = hn_ref.shape[0]
    reps = PEER_NKEYS // BF16_ROWS
    assert PEER_I1_PER_STEP == SUBLANES

    def row_bf16(tile, s):
        x8 = jnp.broadcast_to(tile[s:s + 1, :], (SUBLANES, tokens))
        x16 = jnp.concatenate([x8, x8], axis=0).astype(BF16)
        return jnp.concatenate([x16] * reps, axis=0)

    def evaluate(p_write, p_read):
        first = pl.multiple_of(j * PEER_I1_PER_STEP, SUBLANES)
        cnt_tiles = [cnt1_ref[pl.ds(h * PEER_NKEYS + first, SUBLANES), :] for h in range(PEER_HEADS)]
        e1_tiles = [e1_ref[pl.ds(h * PEER_NKEYS + first, SUBLANES), :] for h in range(PEER_HEADS)]
        per_chunk = PEER_I1_PER_STEP // PEER_MXU_CHUNKS
        crow = per_chunk * PEER_NKEYS
        for c in range(PEER_MXU_CHUNKS):
            rc = slice(c * crow, (c + 1) * crow)
            a_c = _dot(u_ref[rc, :], ht_ref[...])
            if p_read is not None:
                acc_ref[rc, :] += _dot(vt_prev_ref[rc, :], p_read[...])
            for sc in range(per_chunk):
                s = c * per_chunk + sc
                g = None
                for h in range(PEER_HEADS):
                    rows = slice(h * PEER_NKEYS, (h + 1) * PEER_NKEYS)
                    w = jnp.where(rank2_ref[rows, :] < row_bf16(cnt_tiles[h], s),
                                  e2_ref[rows, :] * row_bf16(e1_tiles[h], s), jnp.zeros((), BF16))
                    g = w if g is None else g + w
                a = a_c[sc * PEER_NKEYS:(sc + 1) * PEER_NKEYS, :]
                act = a + a * lax.erf(a * (2.0 ** -0.5))
                p_write[s * PEER_NKEYS:(s + 1) * PEER_NKEYS, :] = g * act.astype(BF16)

    @pl.when(j == 0)
    def _():
        acc_ref[...] = jnp.zeros_like(acc_ref)
        ht_ref[...] = hn_ref[...].astype(F32).T.astype(BF16)
        evaluate(p_even_ref, None)

    @pl.when(j % 2 == 1)
    def _():
        evaluate(p_odd_ref, p_even_ref)

    @pl.when(jnp.logical_and(j % 2 == 0, j > 0))
    def _():
        evaluate(p_even_ref, p_odd_ref)

    @pl.when(j == nblk - 1)
    def _():
        p_last = p_odd_ref if nblk % 2 == 0 else p_even_ref
        acc = acc_ref[...] + _dot(vt_last_ref[...], p_last[...])
        y_ref[...] = x1_ref[...] + acc.T


def _peer_dense(hn, u, vt, rank2, e2, cnt1, e1, x1, n):
    T = PEER_TOK_TILE
    eb = PEER_I1_PER_STEP * PEER_NKEYS
    nexp = u.shape[0]
    sel_rows = PEER_HEADS * PEER_NKEYS
    nblk = nexp // eb
    tok = lambda t, j: (0, t)
    return pl.pallas_call(
        functools.partial(_peer_dense_kernel, nblk),
        grid=(n // T, nblk),
        in_specs=[
            pl.BlockSpec((T, D_MODEL), lambda t, j: (t, 0)),
            pl.BlockSpec((eb, D_MODEL), lambda t, j: (j, 0)),
            pl.BlockSpec((D_MODEL, eb), lambda t, j: (0, jnp.maximum(j - 1, 0))),
            pl.BlockSpec((D_MODEL, eb), lambda t, j: (0, nblk - 1)),
            pl.BlockSpec((sel_rows, T), tok),
            pl.BlockSpec((sel_rows, T), tok),
            pl.BlockSpec((sel_rows, T), tok),
            pl.BlockSpec((sel_rows, T), tok),
            pl.BlockSpec((T, D_MODEL), lambda t, j: (t, 0)),
        ],
        out_specs=pl.BlockSpec((T, D_MODEL), lambda t, j: (t, 0)),
        out_shape=jax.ShapeDtypeStruct((n, D_MODEL), F32),
        scratch_shapes=[pltpu.VMEM((D_MODEL, T), BF16), pltpu.VMEM((eb, T), BF16), pltpu.VMEM((eb, T), BF16), pltpu.VMEM((D_MODEL, T), F32)],
        compiler_params=_cparams("parallel", "arbitrary"),
        name="peer_dense",
    )(hn, u, vt, vt, rank2, e2, cnt1, e1, x1)


def _tri_constants():
    L = MLSTM_CHUNK
    tril = np.tril(np.ones((L, L), np.float32))
    return jnp.asarray(tril, BF16), jnp.asarray(tril.T, BF16)


def _head_constants():
    bd = np.kron(np.eye(H_ATT, dtype=np.float32), np.full((HEAD_DIM, HEAD_DIM), 1.0 / HEAD_DIM, np.float32))
    return jnp.asarray(bd, BF16), jnp.ones((HEAD_DIM, HEAD_DIM), BF16)


def kernel(x_prompt, x_sample, cache_attn_k, cache_attn_v, state_mlstm_C, state_mlstm_n, state_mlstm_m,
           state_mlstm_conv, norm1_g, w_in, att_qnorm_g, att_knorm_g, b_gates, mlstm_conv_w, mlstm_conv_b,
           mlstm_norm_g, w_out, norm2_g, peer_w_query, peer_qnorm_g, peer_subkeys, peer_u, peer_v):
    batch, seq = x_prompt.shape[:2]
    nsamp = x_sample.shape[0]
    past_len = 16384
    assert norm1_g.shape[0] == 1 and x_sample.shape[1] == 1 and seq % MLSTM_CHUNK == 0
    wb = cache_attn_k.shape[2]
    li = 0
    bd, ones_head = _head_constants()
    tril, triu = _tri_constants()

    w = w_in[li]
    gate_lo, gate_hi = 3072, 3072 + 2 * H_MLSTM
    w_main = jnp.concatenate([w[:, :gate_lo], w[:, gate_hi:]], axis=1).astype(BF16)
    wg = jnp.pad(w[:, gate_lo:gate_hi], ((0, 0), (0, LANES - 2 * H_MLSTM)))
    wg_hi = wg.astype(BF16)
    wg_lo = (wg - wg_hi.astype(F32)).astype(BF16)
    g1 = norm1_g[li][None, :]
    qg = jnp.tile(att_qnorm_g[li], H_ATT)[None, :]
    kg = jnp.tile(att_knorm_g[li], H_ATT)[None, :]
    bg = jnp.pad(b_gates[li], (0, LANES - 2 * H_MLSTM))[None, :]
    conv_w = mlstm_conv_w[li]
    conv_b = mlstm_conv_b[li][None, :]
    ng = mlstm_norm_g[li][None, :]
    wo = w_out[li].astype(BF16)
    g2 = norm2_g[li][None, :]
    wq = peer_w_query[li].astype(BF16)
    pqg = peer_qnorm_g[li][None, :]
    sk = peer_subkeys[li].reshape(PEER_HEADS * 2, PEER_NKEYS, PEER_DKEY // 2).astype(BF16)
    u_b = peer_u[li].astype(BF16)
    vt_b = (0.5 * peer_v[li]).astype(BF16).T

    n_p = batch * seq
    xp2 = x_prompt.reshape(n_p, D_MODEL)
    pos_p = jnp.arange(seq, dtype=jnp.int32)
    q, k, v, mqk, mv, mo, gates = _in_projection(xp2, pos_p, 256, g1, w_main, wg_hi, wg_lo, qg, kg, bd)
    att_p = _attn_prompt(q, k, v, batch, seq)
    mh_p, c_p, n_p_state, m_p = _mlstm_prompt(mqk, mv, mo, gates, batch, seq, conv_w, conv_b, bg, ng, tril, triu)
    n_all = n_p + nsamp
    group = LANES * SUBLANES
    n_pad = -(-n_all // group) * group
    x1_all, hn_all = _out_projection(xp2, att_p, mh_p.reshape(n_p, W_MLSTM), wo, g2, 256, n_pad, (0, n_pad))
    wbp = min(wb, seq)
    new_k_prompt = k.reshape(batch, seq, H_ATT, HEAD_DIM)[None, :, seq - wbp:]
    new_v_prompt = v.reshape(batch, seq, H_ATT, HEAD_DIM)[None, :, seq - wbp:]
    new_conv_prompt = mqk.reshape(batch, seq, -1)[None, :, seq - (CONV_W - 1):]

    xs2 = x_sample.reshape(nsamp, D_MODEL)
    pos_s = jnp.full((nsamp,), past_len, dtype=jnp.int32)
    qs, ks, vs, mqk_s, mv_s, mo_s, gates_s = _in_projection(xs2, pos_s, nsamp, g1, w_main, wg_hi, wg_lo, qg, kg, bd)
    att_s = _attn_sample(qs, ks, vs, cache_attn_k[li], cache_attn_v[li], ones_head)
    mh_s, c_s, n_s, m_s = _mlstm_sample(mqk_s, mv_s, mo_s, gates_s, state_mlstm_conv[li], state_mlstm_C[li],
                                        state_mlstm_n[li], state_mlstm_m[li], conv_w, conv_b, bg, ng)
    x1_all, hn_all = _out_projection(xs2, att_s, mh_s, wo, g2, nsamp, n_pad, (n_p, n_all),
                                     into=(x1_all, hn_all))
    new_conv_sample = jnp.concatenate([state_mlstm_conv[li][:, 1:], mqk_s[:, None, :]], axis=1)[None]

    scores_t = _peer_scores(hn_all, wq, pqg, sk, 256)
    rank2, cnt1, e1, e2 = _peer_select(scores_t)
    n_dense = -(-n_all // PEER_TOK_TILE) * PEER_TOK_TILE
    y_all = _peer_dense(hn_all, u_b, vt_b, rank2, e2, cnt1, e1, x1_all, n_dense)
    y_prompt = y_all[:n_p].reshape(batch, seq, D_MODEL)
    y_sample = y_all[n_p:n_all].reshape(nsamp, 1, D_MODEL)

    return (y_prompt, y_sample, new_k_prompt, new_v_prompt,
            ks.reshape(1, nsamp, 1, H_ATT, HEAD_DIM), vs.reshape(1, nsamp, 1, H_ATT, HEAD_DIM),
            c_p[None], n_p_state[None], m_p[None, :, :, 0], new_conv_prompt,
            c_s[None], n_s[None], m_s[None], new_conv_sample)
```

```python
import functools

import numpy as np
import jax
import jax.numpy as jnp
from jax import lax
from jax.experimental import pallas as pl
from jax.experimental.pallas import tpu as pltpu

F32 = jnp.float32
BF16 = jnp.bfloat16

D_MODEL = 1024
HEAD_DIM = 64
W_ATT = 512
W_MLSTM = 512
H_ATT = 8
H_MLSTM = 8
DILATIONS = ((128, 1), (512, 4), (2048, 16))
ATT_BLOCK = 128
ROPE_THETA = 10000.0
CONV_W = 4
PEER_HEADS = 8
PEER_NKEYS = 128
PEER_DKEY = 256
PEER_TOPK = 16
EPS = 1e-6
NEG = -1e30

LANES = 128
SUBLANES = 8
VMEM_LIMIT = 56 * 1024 * 1024


def _cparams(*sem):
    return pltpu.CompilerParams(dimension_semantics=sem, vmem_limit_bytes=VMEM_LIMIT)


def _split2(x):
    hi = x.astype(BF16)
    lo = (x - hi.astype(F32)).astype(BF16)
    return hi, lo


def _split3(x):
    hi = x.astype(BF16)
    r = x - hi.astype(F32)
    mid = r.astype(BF16)
    lo = (r - mid.astype(F32)).astype(BF16)
    return hi, mid, lo


def _dot(a, b):
    return jnp.dot(a, b, preferred_element_type=F32)


def _dot_nt(a, b):
    return lax.dot_general(a, b, (((1,), (1,)), ((), ())), preferred_element_type=F32)


def _dot_parts(parts, b):
    acc = _dot(parts[0], b)
    for p in parts[1:]:
        acc = acc + _dot(p, b)
    return acc


def _sigmoid(x):
    return 1.0 / (1.0 + jnp.exp(-x))


def _log_sigmoid(x):
    return jnp.minimum(x, 0.0) - jnp.log1p(jnp.exp(-jnp.abs(x)))


def _inproj_kernel(x_ref, g_ref, w_ref, wgh_ref, wgl_ref, qg_ref, kg_ref, cos_ref, sin_ref, bd_ref,
                   q_ref, k_ref, v_ref, mqk_ref, mv_ref, mo_ref, gates_ref):
    x = x_ref[...]
    ms = jnp.mean(x * x, axis=-1, keepdims=True)
    xn = x * lax.rsqrt(ms + EPS) * g_ref[...]
    xh, xl = _split2(xn)

    def seg(lo, hi):
        return _dot(xh, w_ref[:, lo:hi])

    bd = bd_ref[...]
    cos = cos_ref[...]
    sin = sin_ref[...]
    lane = lax.broadcasted_iota(jnp.int32, cos.shape, 1)
    first_half = (lane % HEAD_DIM) < (HEAD_DIM // 2)

    def head_norm_rope(a, g):
        sq = a * a
        hi, lo = _split2(sq)
        msq = _dot(hi, bd) + _dot(lo, bd)
        y = a * lax.rsqrt(msq + EPS) * g
        rot = jnp.where(first_half, pltpu.roll(y, W_ATT - HEAD_DIM // 2, 1), pltpu.roll(y, HEAD_DIM // 2, 1))
        return y * cos + rot * sin

    q_ref[...] = head_norm_rope(seg(0, 512), qg_ref[...]) * (HEAD_DIM ** -0.5)
    k_ref[...] = head_norm_rope(seg(512, 1024), kg_ref[...])
    v_ref[...] = seg(1024, 1536)
    mqk_ref[...] = seg(1536, 2560)
    mv_ref[...] = seg(2560, 3072)
    mo_ref[...] = seg(3072, 3584)
    gates_ref[...] = _dot(xh, wgh_ref[...]) + _dot(xl, wgh_ref[...]) + _dot(xh, wgl_ref[...])


def _rope_tables(pos):
    half = HEAD_DIM // 2
    inv = ROPE_THETA ** (-jnp.arange(half, dtype=F32) / half)
    ang = pos.astype(F32)[:, None] * inv[None, :]
    cos = jnp.cos(ang)
    sin = jnp.sin(ang)
    cos_h = jnp.concatenate([cos, cos], axis=-1)
    sin_h = jnp.concatenate([-sin, sin], axis=-1)
    return jnp.tile(cos_h, (1, H_ATT)), jnp.tile(sin_h, (1, H_ATT))


def _in_projection(x2d, pos, tm, norm_g, w_main, wg_hi, wg_lo, qg, kg, bd):
    n = x2d.shape[0]
    cos, sin = _rope_tables(pos)
    pblocks = pos.shape[0] // tm
    row = lambda i: (i, 0)
    fixed = lambda i: (0, 0)
    tab = lambda i: (i % pblocks, 0)
    widths = (512, 512, 512, 1024, 512, 512, LANES)
    return pl.pallas_call(
        _inproj_kernel,
        grid=(n // tm,),
        in_specs=[
            pl.BlockSpec((tm, D_MODEL), row),
            pl.BlockSpec((1, D_MODEL), fixed),
            pl.BlockSpec(w_main.shape, fixed),
            pl.BlockSpec(wg_hi.shape, fixed),
            pl.BlockSpec(wg_lo.shape, fixed),
            pl.BlockSpec((1, W_ATT), fixed),
            pl.BlockSpec((1, W_ATT), fixed),
            pl.BlockSpec((tm, W_ATT), tab),
            pl.BlockSpec((tm, W_ATT), tab),
            pl.BlockSpec((W_ATT, W_ATT), fixed),
        ],
        out_specs=[pl.BlockSpec((tm, w), row) for w in widths],
        out_shape=[jax.ShapeDtypeStruct((n, w), F32) for w in widths],
        compiler_params=_cparams("parallel"),
        name="in_projection",
    )(x2d, norm_g, w_main, wg_hi, wg_lo, qg, kg, cos, sin, bd)


def _merge_branches(outs, lses):
    m = functools.reduce(jnp.maximum, lses)
    es = [jnp.exp(l - m) for l in lses]
    return sum(e * o for e, o in zip(es, outs)) / sum(es)


ATT_GROUP = 4


def _attn_prompt_kernel(q_ref, k_ref, v_ref, att_ref, o_scr, l_scr):
    seq = q_ref.shape[1]
    blk = ATT_BLOCK
    lane = lax.broadcasted_iota(jnp.int32, (1, LANES), 1)
    ii = lax.broadcasted_iota(jnp.int32, (2 * blk, 2 * blk), 0) % blk
    jj = lax.broadcasted_iota(jnp.int32, (2 * blk, 2 * blk), 1)
    mask_prev = jnp.minimum(jj - ii, ii + blk - jj) >= 0
    mask_first = (lax.broadcasted_iota(jnp.int32, (2 * blk, blk), 1)
                  <= lax.broadcasted_iota(jnp.int32, (2 * blk, blk), 0) % blk)

    first_head = (lane // HEAD_DIM) == 0

    def attend(q, k2, v2, mask):
        q2 = jnp.concatenate([jnp.where(first_head, q, 0.0), jnp.where(first_head, 0.0, q)], axis=0).astype(BF16)
        s = jnp.where(mask, _dot_nt(q2, k2), NEG)
        m = jnp.max(s, axis=-1, keepdims=True)
        p = jnp.exp(s - m)
        l = jnp.sum(p, axis=-1, keepdims=True)
        o = _dot((p / l).astype(BF16), v2)
        lse = jnp.broadcast_to(m + jnp.log(l), o.shape)
        return jnp.where(first_head, o[:blk], o[blk:]), jnp.where(first_head, lse[:blk], lse[blk:])

    def group(branch, dil, starts, chained, first_has_prev):
        def rows(at):
            return pl.ds(at, blk, stride=dil) if dil > 1 else pl.ds(at, blk)

        qs = [q_ref[0, rows(at), :] for at in starts]
        ks = [k_ref[0, rows(at), :].astype(BF16) for at in starts]
        vs = [v_ref[0, rows(at), :].astype(BF16) for at in starts]
        k_before = v_before = None
        if chained and first_has_prev:
            before = starts[0] - blk * dil
            k_before = k_ref[0, rows(before), :].astype(BF16)
            v_before = v_ref[0, rows(before), :].astype(BF16)
        results = []
        for i in range(len(starts)):
            kp, vp = (k_before, v_before) if i == 0 else (ks[i - 1], vs[i - 1])
            if chained and kp is not None:
                results.append(attend(qs[i], jnp.concatenate([kp, ks[i]], axis=0),
                                      jnp.concatenate([vp, vs[i]], axis=0), mask_prev))
            else:
                results.append(attend(qs[i], ks[i], vs[i], mask_first))
        for at, (o_pair, l_pair) in zip(starts, results):
            o_scr[branch, rows(at), :] = o_pair
            l_scr[branch, rows(at), :] = l_pair

    for branch, (_, dil) in enumerate(DILATIONS):
        nblocks = seq // dil // blk
        step = blk * dil
        if nblocks == 1:
            def classes(g, carry, branch=branch, dil=dil):
                group(branch, dil, [g * ATT_GROUP + i for i in range(ATT_GROUP)], False, False)
                return carry

            lax.fori_loop(0, dil // ATT_GROUP, classes, 0)
        else:
            assert nblocks % ATT_GROUP == 0

            def residue(r, carry, branch=branch, dil=dil, nblocks=nblocks, step=step):
                group(branch, dil, [r + i * step for i in range(ATT_GROUP)], True, False)

                def later(g, c):
                    base = g * (ATT_GROUP * step)
                    base = pl.multiple_of(base, blk) if dil == 1 else r + base
                    group(branch, dil, [base + i * step for i in range(ATT_GROUP)], True, True)
                    return c

                if nblocks > ATT_GROUP:
                    lax.fori_loop(1, nblocks // ATT_GROUP, later, 0)
                return carry

            if dil == 1:
                residue(0, 0)
            else:
                lax.fori_loop(0, dil, residue, 0)

    nbr = len(DILATIONS)
    att_ref[0] = _merge_branches([o_scr[i] for i in range(nbr)], [l_scr[i] for i in range(nbr)])


def _attn_prompt(q, k, v, batch, seq):
    shape3 = (batch, seq, W_ATT)
    blk = pl.BlockSpec((1, seq, LANES), lambda b, p: (b, 0, p))
    att = pl.pallas_call(
        _attn_prompt_kernel,
        grid=(batch, W_ATT // LANES),
        in_specs=[blk, blk, blk],
        out_specs=blk,
        out_shape=jax.ShapeDtypeStruct(shape3, F32),
        scratch_shapes=[pltpu.VMEM((len(DILATIONS), seq, LANES), F32)] * 2,
        compiler_params=_cparams("parallel", "parallel"),
        name="attn_prompt",
    )(q.reshape(shape3), k.reshape(shape3), v.reshape(shape3))
    return att.reshape(batch * seq, W_ATT)


def _attn_sample_kernel(q_ref, kn_ref, vn_ref, k1_ref, v1_ref, k4_ref, v4_ref, k16_ref, v16_ref, ones_ref, att_ref):
    q = q_ref[0]
    kn = kn_ref[0]
    vn = vn_ref[0]
    ones = ones_ref[...]

    def lane_sum(a):
        return _dot_parts(_split3(a), ones)

    s_new = lane_sum(q * kn)
    outs, lses = [], []
    for kc_ref, vc_ref in ((k1_ref, v1_ref), (k4_ref, v4_ref), (k16_ref, v16_ref)):
        kc = kc_ref[...].reshape(ATT_BLOCK, H_ATT, HEAD_DIM)
        vc = vc_ref[...].reshape(ATT_BLOCK, H_ATT, HEAD_DIM)
        s = lane_sum((kc * q[None]).reshape(ATT_BLOCK * H_ATT, HEAD_DIM)).reshape(ATT_BLOCK, H_ATT, HEAD_DIM)
        m = jnp.maximum(jnp.max(s, axis=0), s_new)
        p = jnp.exp(s - m[None])
        p_new = jnp.exp(s_new - m)
        l = jnp.sum(p, axis=0) + p_new
        outs.append((jnp.sum(p * vc, axis=0) + p_new * vn) / l)
        lses.append(m + jnp.log(l))
    att_ref[0] = _merge_branches(outs, lses)


def _attn_sample(q, k_new, v_new, cache_k, cache_v, ones):
    nb, wb = cache_k.shape[0], cache_k.shape[1]
    span = ATT_BLOCK
    vec = (nb, H_ATT, HEAD_DIM)
    in_specs = [pl.BlockSpec((1, H_ATT, HEAD_DIM), lambda b: (b, 0, 0))] * 3
    args = [t.reshape(vec) for t in (q, k_new, v_new)]
    for _, dil in DILATIONS:
        groups = wb // dil
        assert groups % span == 0
        last = groups // span - 1
        for c in (cache_k, cache_v):
            args.append(c.reshape(nb, groups, dil, H_ATT, HEAD_DIM))
            in_specs.append(pl.BlockSpec((1, span, 1, H_ATT, HEAD_DIM),
                                         functools.partial(lambda b, blk: (b, blk, 0, 0, 0), blk=last)))
    in_specs.append(pl.BlockSpec(ones.shape, lambda b: (0, 0)))
    args.append(ones)
    att = pl.pallas_call(
        _attn_sample_kernel,
        grid=(nb,),
        in_specs=in_specs,
        out_specs=pl.BlockSpec((1, H_ATT, HEAD_DIM), lambda b: (b, 0, 0)),
        out_shape=jax.ShapeDtypeStruct(vec, F32),
        compiler_params=_cparams("parallel"),
        name="attn_sample",
    )(*args)
    return att.reshape(nb, W_ATT)


MLSTM_CHUNK = 128


def _mlstm_prompt_kernel(mqk_ref, mv_ref, mo_ref, gates_ref, cw_ref, cb_ref, bg_ref, ng_ref, tril_ref, triu_ref,
                         mh_ref, c_out_ref, n_out_ref, m_out_ref,
                         conv_scr, c_scr, n_scr, m_scr):
    L = MLSTM_CHUNK
    c_idx = pl.program_id(1)

    @pl.when(c_idx == 0)
    def _():
        conv_scr[0:SUBLANES, :] = jnp.zeros((SUBLANES, 2 * W_MLSTM), F32)
        c_scr[...] = jnp.zeros_like(c_scr)
        n_scr[...] = jnp.zeros_like(n_scr)
        m_scr[...] = jnp.zeros_like(m_scr)

    x = mqk_ref[0]
    conv_scr[SUBLANES:SUBLANES + L, :] = x
    conv = cb_ref[...] + cw_ref[CONV_W - 1:CONV_W, :] * x
    for j in range(CONV_W - 1):
        shift = CONV_W - 1 - j
        conv = conv + cw_ref[j:j + 1, :] * conv_scr[SUBLANES - shift:SUBLANES - shift + L, :]
    conv_scr[0:SUBLANES, :] = x[L - SUBLANES:L, :]
    qk = conv * _sigmoid(conv)
    q_all = qk[:, :W_MLSTM]
    k_all = qk[:, W_MLSTM:] * (HEAD_DIM ** -0.5)
    v_all = mv_ref[0]
    vt_all = v_all.T
    mo = mo_ref[0]

    gb = gates_ref[0] + bg_ref[...]
    gbt = gb.T
    lf_col = _log_sigmoid(gb)
    lf_row = _log_sigmoid(gbt[H_MLSTM:2 * H_MLSTM, :])
    hi, mid, lo = _split3(lf_col)
    tril = tril_ref[...]
    b_col_all = _dot(tril, hi) + _dot(tril, mid) + _dot(tril, lo)
    b_row_all = _dot_parts(_split3(lf_row), triu_ref[...])

    ti = lax.broadcasted_iota(jnp.int32, (L, L), 0)
    si = lax.broadcasted_iota(jnp.int32, (L, L), 1)
    causal = si <= ti

    for h in range(H_MLSTM):
        hs = slice(h * HEAD_DIM, (h + 1) * HEAD_DIM)
        qh = q_all[:, hs]
        kh = k_all[:, hs]
        vh = v_all[:, hs]
        bc = b_col_all[:, H_MLSTM + h:H_MLSTM + h + 1]
        br = b_row_all[h:h + 1, :]
        igr = gbt[h:h + 1, :]
        igc = gb[:, h:h + 1]
        m_prev = m_scr[h:h + 1, 0:1]
        c_prev = c_scr[h]
        n_prev = n_scr[h:h + 1, :]

        log_d = jnp.where(causal, bc - br + igr, NEG)
        m_inter = bc + m_prev
        m_t = jnp.maximum(m_inter, jnp.max(log_d, axis=-1, keepdims=True))
        qh_b = qh.astype(BF16)
        kh_b = kh.astype(BF16)
        sd = _dot_nt(qh_b, kh_b) * jnp.exp(log_d - m_t)
        scale_inter = jnp.exp(m_inter - m_t)
        num = scale_inter * _dot_nt(qh_b, c_prev.astype(BF16)) + _dot(sd.astype(BF16), vh.astype(BF16))
        den = scale_inter * jnp.sum(qh * n_prev, axis=-1, keepdims=True) + jnp.sum(sd, axis=-1, keepdims=True)
        hh = num / jnp.maximum(jnp.abs(den), jnp.exp(-m_t))

        m_new = m_t[L - 1:L, :]
        b_last = bc[L - 1:L, :]
        decay = jnp.exp(b_last + m_prev - m_new)
        w_row = jnp.exp(b_last - br + igr - m_new)
        w_col = jnp.exp(b_last - bc + igc - m_new)
        vtw = (vt_all[hs, :] * w_row).astype(BF16)
        c_new = decay * c_prev + _dot(vtw, kh_b)
        n_new = decay * n_prev + jnp.sum(w_col * kh, axis=0, keepdims=True)
        c_scr[h] = c_new
        n_scr[h:h + 1, :] = n_new
        m_scr[h:h + 1, :] = jnp.broadcast_to(m_new, (1, LANES))
        c_out_ref[0, h] = c_new
        n_out_ref[0, h:h + 1, :] = n_new
        m_out_ref[0, h:h + 1, :] = jnp.broadcast_to(m_new, (1, LANES))

        y = hh * lax.rsqrt(jnp.mean(hh * hh, axis=-1, keepdims=True) + EPS) * ng_ref[:, hs]
        mh_ref[0, :, hs] = _sigmoid(mo[:, hs]) * y


def _mlstm_prompt(mqk, mv, mo, gates, batch, seq, conv_w, conv_b, bg, ng, tril, triu):
    L = MLSTM_CHUNK
    nchunk = seq // L
    chunk = lambda b, c: (b, c, 0)
    fixed = lambda b, c: (0, 0)
    return pl.pallas_call(
        _mlstm_prompt_kernel,
        grid=(batch, nchunk),
        in_specs=[
            pl.BlockSpec((1, L, 2 * W_MLSTM), chunk),
            pl.BlockSpec((1, L, W_MLSTM), chunk),
            pl.BlockSpec((1, L, W_MLSTM), chunk),
            pl.BlockSpec((1, L, LANES), chunk),
            pl.BlockSpec((CONV_W, 2 * W_MLSTM), fixed),
            pl.BlockSpec((1, 2 * W_MLSTM), fixed),
            pl.BlockSpec((1, LANES), fixed),
            pl.BlockSpec((1, W_MLSTM), fixed),
            pl.BlockSpec((L, L), fixed),
            pl.BlockSpec((L, L), fixed),
        ],
        out_specs=[
            pl.BlockSpec((1, L, W_MLSTM), chunk),
            pl.BlockSpec((1, H_MLSTM, HEAD_DIM, HEAD_DIM), lambda b, c: (b, 0, 0, 0)),
            pl.BlockSpec((1, H_MLSTM, HEAD_DIM), lambda b, c: (b, 0, 0)),
            pl.BlockSpec((1, H_MLSTM, LANES), lambda b, c: (b, 0, 0)),
        ],
        out_shape=[
            jax.ShapeDtypeStruct((batch, seq, W_MLSTM), F32),
            jax.ShapeDtypeStruct((batch, H_MLSTM, HEAD_DIM, HEAD_DIM), F32),
            jax.ShapeDtypeStruct((batch, H_MLSTM, HEAD_DIM), F32),
            jax.ShapeDtypeStruct((batch, H_MLSTM, LANES), F32),
        ],
        scratch_shapes=[
            pltpu.VMEM((L + SUBLANES, 2 * W_MLSTM), F32),
            pltpu.VMEM((H_MLSTM, HEAD_DIM, HEAD_DIM), F32),
            pltpu.VMEM((H_MLSTM, HEAD_DIM), F32),
            pltpu.VMEM((H_MLSTM, LANES), F32),
        ],
        compiler_params=_cparams("parallel", "arbitrary"),
        name="mlstm_prompt",
    )(mqk.reshape(batch, seq, -1), mv.reshape(batch, seq, -1), mo.reshape(batch, seq, -1),
      gates.reshape(batch, seq, -1), conv_w, conv_b, bg, ng, tril, triu)


def _mlstm_sample_pre_kernel(mqk_ref, buf_ref, gates_ref, cw_ref, cb_ref, bg_ref, qk_ref, g_ref):
    conv = cb_ref[...] + cw_ref[CONV_W - 1:CONV_W, :] * mqk_ref[...]
    for j in range(CONV_W - 1):
        conv = conv + cw_ref[j:j + 1, :] * buf_ref[j]
    qk = conv * _sigmoid(conv)
    lane = lax.broadcasted_iota(jnp.int32, qk.shape, 1)
    qk_ref[...] = jnp.where(lane < W_MLSTM, qk, qk * (HEAD_DIM ** -0.5))
    gb = gates_ref[...] + bg_ref[...]
    glane = lax.broadcasted_iota(jnp.int32, gb.shape, 1)
    g_ref[...] = jnp.where(glane < H_MLSTM, gb, _log_sigmoid(gb))


def _mlstm_sample_step_kernel(q_ref, k_ref, v_ref, mo_ref, ig_ref, lf_ref, c0_ref, n0_ref, m0_ref, ng_ref,
                              mh_ref, c_ref, n_ref, m_ref):
    for h in range(H_MLSTM):
        q = q_ref[0, h]
        k = k_ref[0, h]
        v = v_ref[0, h]
        ig = ig_ref[0, h]
        lf = lf_ref[0, h]
        c0 = c0_ref[0, h]
        n0 = n0_ref[0, h]
        m0 = m0_ref[0, h]
        m_inter = lf + m0
        m_t = jnp.maximum(m_inter, ig)
        w_in = jnp.exp(ig - m_t)
        sd = jnp.sum(q * k, axis=-1, keepdims=True) * w_in
        scale_inter = jnp.exp(m_inter - m_t)
        num = scale_inter * jnp.sum(c0 * q, axis=-1, keepdims=True) + sd * v
        den = scale_inter * jnp.sum(n0 * q, axis=-1, keepdims=True) + sd
        hh = num / jnp.maximum(jnp.abs(den), jnp.exp(-m_t))
        decay = jnp.exp(lf + m0 - m_t)
        c_ref[0, h] = decay * c0 + w_in * (v * k)
        n_ref[0, h] = decay * n0 + w_in * k
        m_ref[0, h] = m_t
        y = hh * lax.rsqrt(jnp.mean(hh * hh, axis=0, keepdims=True) + EPS) * ng_ref[h]
        mh_ref[0, h] = _sigmoid(mo_ref[0, h]) * y


def _mlstm_sample(mqk, mv, mo, gates, conv_buf, c0, n0, m0, conv_w, conv_b, bg, ng):
    nb = mqk.shape[0]
    full = lambda *shape: pl.BlockSpec(shape, lambda: (0,) * len(shape))
    qk, g = pl.pallas_call(
        _mlstm_sample_pre_kernel,
        in_specs=[full(nb, 2 * W_MLSTM), full(CONV_W - 1, nb, 2 * W_MLSTM), full(nb, LANES),
                  full(CONV_W, 2 * W_MLSTM), full(1, 2 * W_MLSTM), full(1, LANES)],
        out_specs=[full(nb, 2 * W_MLSTM), full(nb, LANES)],
        out_shape=[jax.ShapeDtypeStruct((nb, 2 * W_MLSTM), F32), jax.ShapeDtypeStruct((nb, LANES), F32)],
        name="mlstm_sample_pre",
    )(mqk, jnp.swapaxes(conv_buf, 0, 1), gates, conv_w, conv_b, bg)
    row = (nb, H_MLSTM, 1, HEAD_DIM)
    col = (nb, H_MLSTM, HEAD_DIM, 1)
    one = (nb, H_MLSTM, 1, 1)
    mat = (nb, H_MLSTM, HEAD_DIM, HEAD_DIM)
    spec = lambda shape: pl.BlockSpec((1,) + shape[1:], lambda b: (b, 0, 0, 0))
    ins = [
        (qk[:, :W_MLSTM].reshape(row), row), (qk[:, W_MLSTM:].reshape(row), row), (mv.reshape(col), col),
        (mo.reshape(col), col), (g[:, :H_MLSTM].reshape(one), one), (g[:, H_MLSTM:2 * H_MLSTM].reshape(one), one),
        (c0, mat), (n0.reshape(row), row), (m0.reshape(one), one),
    ]
    mh, c, n, m = pl.pallas_call(
        _mlstm_sample_step_kernel,
        grid=(nb,),
        in_specs=[spec(s) for _, s in ins] + [pl.BlockSpec((H_MLSTM, HEAD_DIM, 1), lambda b: (0, 0, 0))],
        out_specs=[spec(col), spec(mat), spec(row), spec(one)],
        out_shape=[jax.ShapeDtypeStruct(s, F32) for s in (col, mat, row, one)],
        compiler_params=_cparams("parallel"),
        name="mlstm_sample_step",
    )(*[a for a, _ in ins], ng.reshape(H_MLSTM, HEAD_DIM, 1))
    return mh.reshape(nb, W_MLSTM), c, n.reshape(nb, H_MLSTM, HEAD_DIM), m.reshape(nb, H_MLSTM)


def _outproj_kernel(x_ref, att_ref, mh_ref, w_ref, g_ref, x1_ref, hn_ref):
    x1 = (x_ref[...] + _dot(att_ref[...].astype(BF16), w_ref[0:W_ATT, :])
          + _dot(mh_ref[...].astype(BF16), w_ref[W_ATT:, :]))
    x1_ref[...] = x1
    ms = jnp.mean(x1 * x1, axis=-1, keepdims=True)
    hn_ref[...] = (x1 * lax.rsqrt(ms + EPS) * g_ref[...]).astype(BF16)


def _outproj_into_kernel(x1_any, hn_any, *refs):
    del x1_any, hn_any
    _outproj_kernel(*refs)


def _out_projection(x2d, att, mh, w_out, g2, tm, n_out, out_rows, into=None):
    n_in = x2d.shape[0]
    in_blocks = n_in // tm
    first = out_rows[0] // tm
    row = lambda i: (i % in_blocks, 0)
    out_row = lambda i: (first + i, 0)
    fixed = lambda i: (0, 0)
    half = pl.BlockSpec((tm, W_ATT), row)
    in_specs = [pl.BlockSpec((tm, D_MODEL), row), half, half,
                pl.BlockSpec((D_MODEL, D_MODEL), fixed), pl.BlockSpec((1, D_MODEL), fixed)]
    args = (x2d, att, mh, w_out, g2)
    body = _outproj_kernel
    aliases = {}
    if into is not None:
        in_specs = [pl.BlockSpec(memory_space=pl.ANY)] * 2 + in_specs
        args = (*into, *args)
        body = _outproj_into_kernel
        aliases = {0: 0, 1: 1}
    return pl.pallas_call(
        body,
        grid=((out_rows[1] - out_rows[0]) // tm,),
        in_specs=in_specs,
        out_specs=[pl.BlockSpec((tm, D_MODEL), out_row), pl.BlockSpec((tm, D_MODEL), out_row)],
        out_shape=[jax.ShapeDtypeStruct((n_out, D_MODEL), F32), jax.ShapeDtypeStruct((n_out, D_MODEL), BF16)],
        input_output_aliases=aliases,
        compiler_params=_cparams("parallel"),
        name="out_projection",
    )(*args)


def _peer_scores_kernel(hn_ref, wq_ref, g_ref, sk_ref, s_ref):
    q = _dot(hn_ref[...], wq_ref[...])
    half = PEER_DKEY // 2
    for h in range(PEER_HEADS):
        qh = q[:, h * PEER_DKEY:(h + 1) * PEER_DKEY]
        qn = qh * lax.rsqrt(jnp.mean(qh * qh, axis=-1, keepdims=True) + EPS) * g_ref[...]
        for part in range(2):
            s = _dot_nt(qn[:, part * half:(part + 1) * half].astype(BF16), sk_ref[2 * h + part])
            r0 = (2 * h + part) * PEER_NKEYS
            s_ref[r0:r0 + PEER_NKEYS, :] = s.T


def _peer_scores(hn, wq, g, sk, tm):
    n = hn.shape[0]
    rows = PEER_HEADS * 2 * PEER_NKEYS
    return pl.pallas_call(
        _peer_scores_kernel,
        grid=(n // tm,),
        in_specs=[pl.BlockSpec((tm, D_MODEL), lambda i: (i, 0)),
                  pl.BlockSpec(wq.shape, lambda i: (0, 0)),
                  pl.BlockSpec((1, PEER_DKEY), lambda i: (0, 0)),
                  pl.BlockSpec(sk.shape, lambda i: (0, 0, 0))],
        out_specs=pl.BlockSpec((rows, tm), lambda i: (0, i)),
        out_shape=jax.ShapeDtypeStruct((rows, n), F32),
        compiler_params=_cparams("parallel"),
        name="peer_scores",
    )(hn, wq, g, sk)


_STAIR = [(a, b) for a in range(PEER_TOPK) for b in range(PEER_TOPK) if (a + 1) * (b + 1) <= PEER_TOPK]


def _peer_select_kernel(s1_ref, s2_ref, rank2_ref, cnt1_ref, e1_ref, e2_ref,
                        work1_ref, work2_ref, rank1_ref, rank2s_ref, vals1_ref, vals2_ref, cnt_ref, cnt1s_ref):
    shape = work1_ref.shape
    keyf = lax.broadcasted_iota(jnp.int32, shape, 0).astype(F32)
    chunks = [slice(c * LANES, (c + 1) * LANES) for c in range(SUBLANES)]
    halves = ((s1_ref, work1_ref, rank1_ref, vals1_ref), (s2_ref, work2_ref, rank2s_ref, vals2_ref))

    def load_scores():
        for src_ref, work_ref, rank_ref, _ in halves:
            for c, cs in enumerate(chunks):
                work_ref[:, c, :] = src_ref[:, cs]
            rank_ref[...] = jnp.full(shape, float(PEER_TOPK), F32)

    def top16_distinct():
        def one_round(a, previous):
            maxima = []
            for (_, work_ref, rank_ref, vals_ref), prev in zip(halves, previous):
                w = work_ref[...]
                rank_ref[...] = jnp.where(w == prev, jnp.asarray(a - 1, F32), rank_ref[...])
                mx = jnp.max(jnp.where(w < prev, w, -jnp.inf), axis=0, keepdims=True)
                vals_ref[pl.ds(a, 1)] = mx
                maxima.append(mx)
            return tuple(maxima)

        top = jnp.full((1,) + shape[1:], jnp.inf, F32)
        last = lax.fori_loop(0, PEER_TOPK, one_round, (top, top))
        for (_, work_ref, rank_ref, _), prev in zip(halves, last):
            rank_ref[...] = jnp.where(work_ref[...] == prev, float(PEER_TOPK - 1), rank_ref[...])

    def top16_ties():
        def one_round(a, carry):
            for _, work_ref, rank_ref, vals_ref in halves:
                w = work_ref[...]
                mx = jnp.max(w, axis=0, keepdims=True)
                sel = keyf == jnp.min(jnp.where(w == mx, keyf, float(PEER_NKEYS)), axis=0, keepdims=True)
                work_ref[...] = jnp.where(sel, -jnp.inf, w)
                rank_ref[...] = jnp.where(sel, jnp.asarray(a, F32), rank_ref[...])
                vals_ref[pl.ds(a, 1)] = mx
            return carry

        lax.fori_loop(0, PEER_TOPK, one_round, 0)

    load_scores()
    top16_distinct()
    taken = [jnp.sum(jnp.where(rank_ref[...] < float(PEER_TOPK), 1.0, 0.0), axis=0) for _, _, rank_ref, _ in halves]
    ties = jnp.max(jnp.maximum(jnp.abs(taken[0] - PEER_TOPK), jnp.abs(taken[1] - PEER_TOPK))) > 0.0

    @pl.when(ties)
    def _():
        load_scores()
        top16_ties()

    v1 = [vals1_ref[a] for a in range(PEER_TOPK)]
    v2 = [vals2_ref[b] for b in range(PEER_TOPK)]
    cand = [v1[a] + v2[b] for a, b in _STAIR]
    pos = []
    for i, (a, b) in enumerate(_STAIR):
        static = sum(1 for (a2, b2) in _STAIR if a2 <= a and b2 <= b and (a2, b2) != (a, b))
        pos.append(jnp.full(cand[0].shape, float(static), F32))
    for i, (ai, bi) in enumerate(_STAIR):
        for j in range(i + 1, len(_STAIR)):
            aj, bj = _STAIR[j]
            if ai <= aj and bi <= bj:
                continue
            i_first = cand[i] >= cand[j]
            pos[j] = pos[j] + jnp.where(i_first, 1.0, 0.0)
            pos[i] = pos[i] + jnp.where(i_first, 0.0, 1.0)
    e1s = [jnp.exp(v1[a] - v1[0]) for a in range(PEER_TOPK)]
    e2s = [jnp.exp(v2[b] - v2[0]) for b in range(PEER_TOPK)]
    z = jnp.zeros_like(cand[0])
    for a in range(PEER_TOPK):
        cnt_a = jnp.zeros_like(z)
        za = jnp.zeros_like(z)
        for i, (a2, b) in enumerate(_STAIR):
            if a2 != a:
                continue
            chosen = pos[i] < float(PEER_TOPK)
            cnt_a = cnt_a + jnp.where(chosen, 1.0, 0.0)
            za = za + jnp.where(chosen, e2s[b], 0.0)
        cnt_ref[a] = cnt_a
        z = z + e1s[a] * za
    z_inv = 1.0 / z

    cnt1s_ref[...] = jnp.zeros(shape, F32)

    def spread(a, carry):
        cnt1s_ref[...] = jnp.where(rank1_ref[...] == jnp.asarray(a, F32), cnt_ref[pl.ds(a, 1)], cnt1s_ref[...])
        return carry

    lax.fori_loop(0, PEER_TOPK, spread, 0)
    for c, cs in enumerate(chunks):
        rank2_ref[:, cs] = rank2s_ref[:, c, :].astype(BF16)
        cnt1_ref[:, cs] = cnt1s_ref[:, c, :]
        e1_ref[:, cs] = jnp.exp(s1_ref[:, cs] - v1[0][c:c + 1, :]) * z_inv[c:c + 1, :]
        e2_ref[:, cs] = jnp.exp(s2_ref[:, cs] - v2[0][c:c + 1, :]).astype(BF16)


def _peer_select(scores_t):
    n = scores_t.shape[1]
    tok = LANES * SUBLANES
    blk = (PEER_NKEYS, tok)
    blk3 = (PEER_NKEYS, SUBLANES, LANES)
    small = (PEER_TOPK, SUBLANES, LANES)
    out_rows = PEER_HEADS * PEER_NKEYS
    return pl.pallas_call(
        _peer_select_kernel,
        grid=(PEER_HEADS, n // tok),
        in_specs=[pl.BlockSpec(blk, lambda h, g: (2 * h, g)), pl.BlockSpec(blk, lambda h, g: (2 * h + 1, g))],
        out_specs=[pl.BlockSpec(blk, lambda h, g: (h, g))] * 4,
        out_shape=[jax.ShapeDtypeStruct((out_rows, n), dt) for dt in (BF16, F32, F32, BF16)],
        scratch_shapes=[pltpu.VMEM(blk3, F32)] * 4 + [pltpu.VMEM(small, F32)] * 3 + [pltpu.VMEM(blk3, F32)],
        compiler_params=_cparams("parallel", "parallel"),
        name="peer_select",
    )(scores_t, scores_t)


PEER_TOK_TILE = 512
PEER_I1_PER_STEP = 8
PEER_MXU_CHUNKS = 4
BF16_ROWS = 16


def _peer_dense_kernel(nblk, hn_ref, u_ref, vt_prev_ref, vt_last_ref, rank2_ref, e2_ref, cnt1_ref, e1_ref, x1_ref, y_ref,
                       ht_ref, p_even_ref, p_odd_ref, acc_ref):
    j = pl.program_id(1)
    tokens = hn_ref.shape[0]
    reps = PEER_NKEYS // BF16_ROWS
    assert PEER_I1_PER_STEP == SUBLANES

    def row_bf16(tile, s):
        x8 = jnp.broadcast_to(tile[s:s + 1, :], (SUBLANES, tokens))
        x16 = jnp.concatenate([x8, x8], axis=0).astype(BF16)
        return jnp.concatenate([x16] * reps, axis=0)

    def evaluate(p_write, p_read):
        first = pl.multiple_of(j * PEER_I1_PER_STEP, SUBLANES)
        cnt_tiles = [cnt1_ref[pl.ds(h * PEER_NKEYS + first, SUBLANES), :] for h in range(PEER_HEADS)]
        e1_tiles = [e1_ref[pl.ds(h * PEER_NKEYS + first, SUBLANES), :] for h in range(PEER_HEADS)]
        per_chunk = PEER_I1_PER_STEP // PEER_MXU_CHUNKS
        crow = per_chunk * PEER_NKEYS
        for c in range(PEER_MXU_CHUNKS):
            rc = slice(c * crow, (c + 1) * crow)
            a_c = _dot(u_ref[rc, :], ht_ref[...])
            if p_read is not None:
                acc_ref[rc, :] += _dot(vt_prev_ref[rc, :], p_read[...])
            for sc in range(per_chunk):
                s = c * per_chunk + sc
                g = None
                for h in range(PEER_HEADS):
                    rows = slice(h * PEER_NKEYS, (h + 1) * PEER_NKEYS)
                    w = jnp.where(rank2_ref[rows, :] < row_bf16(cnt_tiles[h], s),
                                  e2_ref[rows, :] * row_bf16(e1_tiles[h], s), jnp.zeros((), BF16))
                    g = w if g is None else g + w
                a = a_c[sc * PEER_NKEYS:(sc + 1) * PEER_NKEYS, :]
                act = a + a * lax.erf(a * (2.0 ** -0.5))
                p_write[s * PEER_NKEYS:(s + 1) * PEER_NKEYS, :] = g * act.astype(BF16)

    @pl.when(j == 0)
    def _():
        acc_ref[...] = jnp.zeros_like(acc_ref)
        ht_ref[...] = hn_ref[...].astype(F32).T.astype(BF16)
        evaluate(p_even_ref, None)

    @pl.when(j % 2 == 1)
    def _():
        evaluate(p_odd_ref, p_even_ref)

    @pl.when(jnp.logical_and(j % 2 == 0, j > 0))
    def _():
        evaluate(p_even_ref, p_odd_ref)

    @pl.when(j == nblk - 1)
    def _():
        p_last = p_odd_ref if nblk % 2 == 0 else p_even_ref
        acc = acc_ref[...] + _dot(vt_last_ref[...], p_last[...])
        y_ref[...] = x1_ref[...] + acc.T


def _peer_dense(hn, u, vt, rank2, e2, cnt1, e1, x1, n):
    T = PEER_TOK_TILE
    eb = PEER_I1_PER_STEP * PEER_NKEYS
    nexp = u.shape[0]
    sel_rows = PEER_HEADS * PEER_NKEYS
    nblk = nexp // eb
    tok = lambda t, j: (0, t)
    return pl.pallas_call(
        functools.partial(_peer_dense_kernel, nblk),
        grid=(n // T, nblk),
        in_specs=[
            pl.BlockSpec((T, D_MODEL), lambda t, j: (t, 0)),
            pl.BlockSpec((eb, D_MODEL), lambda t, j: (j, 0)),
            pl.BlockSpec((D_MODEL, eb), lambda t, j: (0, jnp.maximum(j - 1, 0))),
            pl.BlockSpec((D_MODEL, eb), lambda t, j: (0, nblk - 1)),
            pl.BlockSpec((sel_rows, T), tok),
            pl.BlockSpec((sel_rows, T), tok),
            pl.BlockSpec((sel_rows, T), tok),
            pl.BlockSpec((sel_rows, T), tok),
            pl.BlockSpec((T, D_MODEL), lambda t, j: (t, 0)),
        ],
        out_specs=pl.BlockSpec((T, D_MODEL), lambda t, j: (t, 0)),
        out_shape=jax.ShapeDtypeStruct((n, D_MODEL), F32),
        scratch_shapes=[pltpu.VMEM((D_MODEL, T), BF16), pltpu.VMEM((eb, T), BF16), pltpu.VMEM((eb, T), BF16), pltpu.VMEM((D_MODEL, T), F32)],
        compiler_params=_cparams("parallel", "arbitrary"),
        name="peer_dense",
    )(hn, u, vt, vt, rank2, e2, cnt1, e1, x1)


def _tri_constants():
    L = MLSTM_CHUNK
    tril = np.tril(np.ones((L, L), np.float32))
    return jnp.asarray(tril, BF16), jnp.asarray(tril.T, BF16)


def _head_constants():
    bd = np.kron(np.eye(H_ATT, dtype=np.float32), np.full((HEAD_DIM, HEAD_DIM), 1.0 / HEAD_DIM, np.float32))
    return jnp.asarray(bd, BF16), jnp.ones((HEAD_DIM, HEAD_DIM), BF16)


def kernel(x_prompt, x_sample, cache_attn_k, cache_attn_v, state_mlstm_C, state_mlstm_n, state_mlstm_m,
           state_mlstm_conv, norm1_g, w_in, att_qnorm_g, att_knorm_g, b_gates, mlstm_conv_w, mlstm_conv_b,
           mlstm_norm_g, w_out, norm2_g, peer_w_query, peer_qnorm_g, peer_subkeys, peer_u, peer_v):
    batch, seq = x_prompt.shape[:2]
    nsamp = x_sample.shape[0]
    past_len = 16384
    assert norm1_g.shape[0] == 1 and x_sample.shape[1] == 1 and seq % MLSTM_CHUNK == 0
    wb = cache_attn_k.shape[2]
    li = 0
    bd, ones_head = _head_constants()
    tril, triu = _tri_constants()

    w = w_in[li]
    gate_lo, gate_hi = 3072, 3072 + 2 * H_MLSTM
    w_main = jnp.concatenate([w[:, :gate_lo], w[:, gate_hi:]], axis=1).astype(BF16)
    wg = jnp.pad(w[:, gate_lo:gate_hi], ((0, 0), (0, LANES - 2 * H_MLSTM)))
    wg_hi = wg.astype(BF16)
    wg_lo = (wg - wg_hi.astype(F32)).astype(BF16)
    g1 = norm1_g[li][None, :]
    qg = jnp.tile(att_qnorm_g[li], H_ATT)[None, :]
    kg = jnp.tile(att_knorm_g[li], H_ATT)[None, :]
    bg = jnp.pad(b_gates[li], (0, LANES - 2 * H_MLSTM))[None, :]
    conv_w = mlstm_conv_w[li]
    conv_b = mlstm_conv_b[li][None, :]
    ng = mlstm_norm_g[li][None, :]
    wo = w_out[li].astype(BF16)
    g2 = norm2_g[li][None, :]
    wq = peer_w_query[li].astype(BF16)
    pqg = peer_qnorm_g[li][None, :]
    sk = peer_subkeys[li].reshape(PEER_HEADS * 2, PEER_NKEYS, PEER_DKEY // 2).astype(BF16)
    u_b = peer_u[li].astype(BF16)
    vt_b = (0.5 * peer_v[li]).astype(BF16).T

    n_p = batch * seq
    xp2 = x_prompt.reshape(n_p, D_MODEL)
    pos_p = jnp.arange(seq, dtype=jnp.int32)
    q, k, v, mqk, mv, mo, gates = _in_projection(xp2, pos_p, 256, g1, w_main, wg_hi, wg_lo, qg, kg, bd)
    att_p = _attn_prompt(q, k, v, batch, seq)
    mh_p, c_p, n_p_state, m_p = _mlstm_prompt(mqk, mv, mo, gates, batch, seq, conv_w, conv_b, bg, ng, tril, triu)
    n_all = n_p + nsamp
    group = LANES * SUBLANES
    n_pad = -(-n_all // group) * group
    x1_all, hn_all = _out_projection(xp2, att_p, mh_p.reshape(n_p, W_MLSTM), wo, g2, 256, n_pad, (0, n_pad))
    wbp = min(wb, seq)
    new_k_prompt = k.reshape(batch, seq, H_ATT, HEAD_DIM)[None, :, seq - wbp:]
    new_v_prompt = v.reshape(batch, seq, H_ATT, HEAD_DIM)[None, :, seq - wbp:]
    new_conv_prompt = mqk.reshape(batch, seq, -1)[None, :, seq - (CONV_W - 1):]

    xs2 = x_sample.reshape(nsamp, D_MODEL)
    pos_s = jnp.full((nsamp,), past_len, dtype=jnp.int32)
    qs, ks, vs, mqk_s, mv_s, mo_s, gates_s = _in_projection(xs2, pos_s, nsamp, g1, w_main, wg_hi, wg_lo, qg, kg, bd)
    att_s = _attn_sample(qs, ks, vs, cache_attn_k[li], cache_attn_v[li], ones_head)
    mh_s, c_s, n_s, m_s = _mlstm_sample(mqk_s, mv_s, mo_s, gates_s, state_mlstm_conv[li], state_mlstm_C[li],
                                        state_mlstm_n[li], state_mlstm_m[li], conv_w, conv_b, bg, ng)
    x1_all, hn_all = _out_projection(xs2, att_s, mh_s, wo, g2, nsamp, n_pad, (n_p, n_all),
                                     into=(x1_all, hn_all))
    new_conv_sample = jnp.concatenate([state_mlstm_conv[li][:, 1:], mqk_s[:, None, :]], axis=1)[None]

    scores_t = _peer_scores(hn_all, wq, pqg, sk, 256)
    rank2, cnt1, e1, e2 = _peer_select(scores_t)
    n_dense = -(-n_all // PEER_TOK_TILE) * PEER_TOK_TILE
    y_all = _peer_dense(hn_all, u_b, vt_b, rank2, e2, cnt1, e1, x1_all, n_dense)
    y_prompt = y_all[:n_p].reshape(batch, seq, D_MODEL)
    y_sample = y_all[n_p:n_all].reshape(nsamp, 1, D_MODEL)

    return (y_prompt, y_sample, new_k_prompt, new_v_prompt,
            ks.reshape(1, nsamp, 1, H_ATT, HEAD_DIM), vs.reshape(1, nsamp, 1, H_ATT, HEAD_DIM),
            c_p[None], n_p_state[None], m_p[None, :, :, 0], new_conv_prompt,
            c_s[None], n_s[None], m_s[None], new_conv_sample)
```

```python
import functools

import numpy as np
import jax
import jax.numpy as jnp
from jax import lax
from jax.experimental import pallas as pl
from jax.experimental.pallas import tpu as pltpu

F32 = jnp.float32
BF16 = jnp.bfloat16

D_MODEL = 1024
HEAD_DIM = 64
W_ATT = 512
W_MLSTM = 512
H_ATT = 8
H_MLSTM = 8
DILATIONS = ((128, 1), (512, 4), (2048, 16))
ATT_BLOCK = 128
ROPE_THETA = 10000.0
CONV_W = 4
PEER_HEADS = 8
PEER_NKEYS = 128
PEER_DKEY = 256
PEER_TOPK = 16
EPS = 1e-6
NEG = -1e30

LANES = 128
SUBLANES = 8
VMEM_LIMIT = 56 * 1024 * 1024


def _cparams(*sem):
    return pltpu.CompilerParams(dimension_semantics=sem, vmem_limit_bytes=VMEM_LIMIT)


def _split2(x):
    hi = x.astype(BF16)
    lo = (x - hi.astype(F32)).astype(BF16)
    return hi, lo


def _split3(x):
    hi = x.astype(BF16)
    r = x - hi.astype(F32)
    mid = r.astype(BF16)
    lo = (r - mid.astype(F32)).astype(BF16)
    return hi, mid, lo


def _dot(a, b):
    return jnp.dot(a, b, preferred_element_type=F32)


def _dot_nt(a, b):
    return lax.dot_general(a, b, (((1,), (1,)), ((), ())), preferred_element_type=F32)


def _dot_parts(parts, b):
    acc = _dot(parts[0], b)
    for p in parts[1:]:
        acc = acc + _dot(p, b)
    return acc


def _sigmoid(x):
    return 1.0 / (1.0 + jnp.exp(-x))


def _log_sigmoid(x):
    return jnp.minimum(x, 0.0) - jnp.log1p(jnp.exp(-jnp.abs(x)))


def _inproj_kernel(x_ref, g_ref, w_ref, wgh_ref, wgl_ref, qg_ref, kg_ref, cos_ref, sin_ref, bd_ref,
                   q_ref, k_ref, v_ref, mqk_ref, mv_ref, mo_ref, gates_ref, kt_ref=None, vt_ref=None):
    x = x_ref[...]
    ms = jnp.mean(x * x, axis=-1, keepdims=True)
    xn = x * lax.rsqrt(ms + EPS) * g_ref[...]
    xh, xl = _split2(xn)

    def seg(lo, hi):
        return _dot(xh, w_ref[:, lo:hi])

    bd = bd_ref[...]
    cos = cos_ref[...]
    sin = sin_ref[...]
    lane = lax.broadcasted_iota(jnp.int32, cos.shape, 1)
    first_half = (lane % HEAD_DIM) < (HEAD_DIM // 2)

    def head_norm_rope(a, g):
        sq = a * a
        hi, lo = _split2(sq)
        msq = _dot(hi, bd) + _dot(lo, bd)
        y = a * lax.rsqrt(msq + EPS) * g
        rot = jnp.where(first_half, pltpu.roll(y, W_ATT - HEAD_DIM // 2, 1), pltpu.roll(y, HEAD_DIM // 2, 1))
        return y * cos + rot * sin

    q_ref[...] = head_norm_rope(seg(0, 512), qg_ref[...]) * (HEAD_DIM ** -0.5)
    k = head_norm_rope(seg(512, 1024), kg_ref[...])
    v = seg(1024, 1536)
    k_ref[...] = k
    v_ref[...] = v
    if kt_ref is not None:
        kt_ref[0] = k.T.reshape(H_ATT, HEAD_DIM, k.shape[0])
        vt_ref[0] = v.T.reshape(H_ATT, HEAD_DIM, v.shape[0])
    mqk_ref[...] = seg(1536, 2560)
    mv_ref[...] = seg(2560, 3072)
    mo_ref[...] = seg(3072, 3584)
    gates_ref[...] = _dot(xh, wgh_ref[...]) + _dot(xl, wgh_ref[...]) + _dot(xh, wgl_ref[...])


def _rope_tables(pos):
    half = HEAD_DIM // 2
    inv = ROPE_THETA ** (-jnp.arange(half, dtype=F32) / half)
    ang = pos.astype(F32)[:, None] * inv[None, :]
    cos = jnp.cos(ang)
    sin = jnp.sin(ang)
    cos_h = jnp.concatenate([cos, cos], axis=-1)
    sin_h = jnp.concatenate([-sin, sin], axis=-1)
    return jnp.tile(cos_h, (1, H_ATT)), jnp.tile(sin_h, (1, H_ATT))


def _in_projection(x2d, pos, tm, norm_g, w_main, wg_hi, wg_lo, qg, kg, bd, transposed_kv=False):
    n = x2d.shape[0]
    cos, sin = _rope_tables(pos)
    pblocks = pos.shape[0] // tm
    row = lambda i: (i, 0)
    fixed = lambda i: (0, 0)
    tab = lambda i: (i % pblocks, 0)
    widths = (512, 512, 512, 1024, 512, 512, LANES)
    out_specs = [pl.BlockSpec((tm, w), row) for w in widths]
    out_shape = [jax.ShapeDtypeStruct((n, w), F32) for w in widths]
    if transposed_kv:
        tshape = (n // pos.shape[0], H_ATT, HEAD_DIM, pos.shape[0])
        out_specs += [pl.BlockSpec((1, H_ATT, HEAD_DIM, tm), lambda i: (i // pblocks, 0, 0, i % pblocks))] * 2
        out_shape += [jax.ShapeDtypeStruct(tshape, F32)] * 2
    return pl.pallas_call(
        _inproj_kernel,
        grid=(n // tm,),
        in_specs=[
            pl.BlockSpec((tm, D_MODEL), row),
            pl.BlockSpec((1, D_MODEL), fixed),
            pl.BlockSpec(w_main.shape, fixed),
            pl.BlockSpec(wg_hi.shape, fixed),
            pl.BlockSpec(wg_lo.shape, fixed),
            pl.BlockSpec((1, W_ATT), fixed),
            pl.BlockSpec((1, W_ATT), fixed),
            pl.BlockSpec((tm, W_ATT), tab),
            pl.BlockSpec((tm, W_ATT), tab),
            pl.BlockSpec((W_ATT, W_ATT), fixed),
        ],
        out_specs=out_specs,
        out_shape=out_shape,
        compiler_params=_cparams("parallel"),
        name="in_projection",
    )(x2d, norm_g, w_main, wg_hi, wg_lo, qg, kg, cos, sin, bd)


def _merge_branches(outs, lses):
    m = functools.reduce(jnp.maximum, lses)
    es = [jnp.exp(l - m) for l in lses]
    return sum(e * o for e, o in zip(es, outs)) / sum(es)


ATT_GROUP = 4


def _attn_prompt_kernel(q_ref, k_ref, v_ref, att_ref, o_scr, l_scr):
    seq = q_ref.shape[1]
    blk = ATT_BLOCK
    lane = lax.broadcasted_iota(jnp.int32, (1, LANES), 1)
    ii = lax.broadcasted_iota(jnp.int32, (2 * blk, 2 * blk), 0) % blk
    jj = lax.broadcasted_iota(jnp.int32, (2 * blk, 2 * blk), 1)
    mask_prev = jnp.minimum(jj - ii, ii + blk - jj) >= 0
    mask_first = (lax.broadcasted_iota(jnp.int32, (2 * blk, blk), 1)
                  <= lax.broadcasted_iota(jnp.int32, (2 * blk, blk), 0) % blk)

    first_head = (lane // HEAD_DIM) == 0

    def attend(q, k2, v2, mask):
        q2 = jnp.concatenate([jnp.where(first_head, q, 0.0), jnp.where(first_head, 0.0, q)], axis=0).astype(BF16)
        s = jnp.where(mask, _dot_nt(q2, k2), NEG)
        m = jnp.max(s, axis=-1, keepdims=True)
        p = jnp.exp(s - m)
        l = jnp.sum(p, axis=-1, keepdims=True)
        o = _dot((p / l).astype(BF16), v2)
        lse = jnp.broadcast_to(m + jnp.log(l), o.shape)
        return jnp.where(first_head, o[:blk], o[blk:]), jnp.where(first_head, lse[:blk], lse[blk:])

    def group(branch, dil, starts, chained, first_has_prev):
        def rows(at):
            return pl.ds(at, blk, stride=dil) if dil > 1 else pl.ds(at, blk)

        qs = [q_ref[0, rows(at), :] for at in starts]
        ks = [k_ref[0, rows(at), :].astype(BF16) for at in starts]
        vs = [v_ref[0, rows(at), :].astype(BF16) for at in starts]
        k_before = v_before = None
        if chained and first_has_prev:
            before = starts[0] - blk * dil
            k_before = k_ref[0, rows(before), :].astype(BF16)
            v_before = v_ref[0, rows(before), :].astype(BF16)
        results = []
        for i in range(len(starts)):
            kp, vp = (k_before, v_before) if i == 0 else (ks[i - 1], vs[i - 1])
            if chained and kp is not None:
                results.append(attend(qs[i], jnp.concatenate([kp, ks[i]], axis=0),
                                      jnp.concatenate([vp, vs[i]], axis=0), mask_prev))
            else:
                results.append(attend(qs[i], ks[i], vs[i], mask_first))
        for at, (o_pair, l_pair) in zip(starts, results):
            o_scr[branch, rows(at), :] = o_pair
            l_scr[branch, rows(at), :] = l_pair

    for branch, (_, dil) in enumerate(DILATIONS):
        nblocks = seq // dil // blk
        step = blk * dil
        if nblocks == 1:
            def classes(g, carry, branch=branch, dil=dil):
                group(branch, dil, [g * ATT_GROUP + i for i in range(ATT_GROUP)], False, False)
                return carry

            lax.fori_loop(0, dil // ATT_GROUP, classes, 0)
        else:
            assert nblocks % ATT_GROUP == 0

            def residue(r, carry, branch=branch, dil=dil, nblocks=nblocks, step=step):
                group(branch, dil, [r + i * step for i in range(ATT_GROUP)], True, False)

                def later(g, c):
                    base = g * (ATT_GROUP * step)
                    base = pl.multiple_of(base, blk) if dil == 1 else r + base
                    group(branch, dil, [base + i * step for i in range(ATT_GROUP)], True, True)
                    return c

                if nblocks > ATT_GROUP:
                    lax.fori_loop(1, nblocks // ATT_GROUP, later, 0)
                return carry

            if dil == 1:
                residue(0, 0)
            else:
                lax.fori_loop(0, dil, residue, 0)

    nbr = len(DILATIONS)
    att_ref[0] = _merge_branches([o_scr[i] for i in range(nbr)], [l_scr[i] for i in range(nbr)])


def _attn_prompt(q, k, v, batch, seq):
    shape3 = (batch, seq, W_ATT)
    blk = pl.BlockSpec((1, seq, LANES), lambda b, p: (b, 0, p))
    att = pl.pallas_call(
        _attn_prompt_kernel,
        grid=(batch, W_ATT // LANES),
        in_specs=[blk, blk, blk],
        out_specs=blk,
        out_shape=jax.ShapeDtypeStruct(shape3, F32),
        scratch_shapes=[pltpu.VMEM((len(DILATIONS), seq, LANES), F32)] * 2,
        compiler_params=_cparams("parallel", "parallel"),
        name="attn_prompt",
    )(q.reshape(shape3), k.reshape(shape3), v.reshape(shape3))
    return att.reshape(batch * seq, W_ATT)


def _attn_sample_kernel(q_ref, kn_ref, vn_ref, kt_ref, vt_ref, att_ref):
    q = q_ref[0]
    kn = kn_ref[0]
    vn = vn_ref[0]
    kt = kt_ref[0]
    vt = vt_ref[0]
    wb = kt.shape[-1]
    scores = jnp.sum(kt * q, axis=1, keepdims=True)
    s_new = jnp.sum(q * kn, axis=1, keepdims=True)
    row = lax.broadcasted_iota(jnp.int32, (1, 1, wb), 2)
    outs, lses = [], []
    for _, dil in DILATIONS:
        on_grid = jnp.where(row % dil == 0, row, -1)
        s = jnp.where(on_grid >= wb - ATT_BLOCK * dil, scores, NEG)
        m = jnp.maximum(jnp.max(s, axis=2, keepdims=True), s_new)
        p = jnp.exp(s - m)
        p_new = jnp.exp(s_new - m)
        l = jnp.sum(p, axis=2, keepdims=True) + p_new
        outs.append((jnp.sum(vt * p, axis=2, keepdims=True) + p_new * vn) / l)
        lses.append(m + jnp.log(l))
    att_ref[0] = _merge_branches(outs, lses)


def _attn_sample(q, k_new, v_new, cache_kt, cache_vt):
    nb, wb = cache_kt.shape[0], cache_kt.shape[-1]
    assert all(wb % dil == 0 and wb >= ATT_BLOCK * dil for _, dil in DILATIONS)
    col = (nb, H_ATT, HEAD_DIM, 1)
    col_spec = pl.BlockSpec((1, H_ATT, HEAD_DIM, 1), lambda b: (b, 0, 0, 0))
    cache_spec = pl.BlockSpec((1, H_ATT, HEAD_DIM, wb), lambda b: (b, 0, 0, 0))
    att = pl.pallas_call(
        _attn_sample_kernel,
        grid=(nb,),
        in_specs=[col_spec] * 3 + [cache_spec] * 2,
        out_specs=col_spec,
        out_shape=jax.ShapeDtypeStruct(col, F32),
        compiler_params=_cparams("parallel"),
        name="attn_sample",
    )(q.reshape(col), k_new.reshape(col), v_new.reshape(col), cache_kt, cache_vt)
    return att.reshape(nb, W_ATT)


MLSTM_CHUNK = 128


def _mlstm_prompt_kernel(mqk_ref, mv_ref, mo_ref, gates_ref, cw_ref, cb_ref, bg_ref, ng_ref, tril_ref, triu_ref,
                         rep_ref, mh_ref, c_out_ref, n_out_ref, m_out_ref,
                         conv_scr, c_scr, n_scr, m_scr):
    L = MLSTM_CHUNK
    c_idx = pl.program_id(1)

    @pl.when(c_idx == 0)
    def _():
        conv_scr[0:SUBLANES, :] = jnp.zeros((SUBLANES, 2 * W_MLSTM), F32)
        c_scr[...] = jnp.zeros_like(c_scr)
        n_scr[...] = jnp.zeros_like(n_scr)
        m_scr[...] = jnp.zeros_like(m_scr)

    x = mqk_ref[0]
    conv_scr[SUBLANES:SUBLANES + L, :] = x
    conv = cb_ref[...] + cw_ref[CONV_W - 1:CONV_W, :] * x
    for j in range(CONV_W - 1):
        shift = CONV_W - 1 - j
        conv = conv + cw_ref[j:j + 1, :] * conv_scr[SUBLANES - shift:SUBLANES - shift + L, :]
    conv_scr[0:SUBLANES, :] = x[L - SUBLANES:L, :]
    qk = conv * _sigmoid(conv)
    q_all = qk[:, :W_MLSTM]
    k_all = qk[:, W_MLSTM:] * (HEAD_DIM ** -0.5)
    v_all = mv_ref[0]
    vt_all = v_all.T
    mo = mo_ref[0]

    assert L == LANES
    gb = gates_ref[0] + bg_ref[...]
    gbt = gb.T
    g_rep = _dot_parts(_split3(gb), rep_ref[...])
    ig_rep = g_rep[:, :H_MLSTM * LANES]
    lf_rep = _log_sigmoid(g_rep[:, H_MLSTM * LANES:])
    lf_row = _log_sigmoid(gbt[H_MLSTM:2 * H_MLSTM, :])
    tril = tril_ref[...]
    b_rep = _dot(tril, lf_rep.astype(BF16))
    rest = lf_rep - lf_rep.astype(BF16).astype(F32)
    b_rep = b_rep + _dot(tril, rest.astype(BF16))
    b_rep = b_rep + _dot(tril, (rest - rest.astype(BF16).astype(F32)).astype(BF16))
    b_row_all = _dot_parts(_split3(lf_row), triu_ref[...])

    ti = lax.broadcasted_iota(jnp.int32, (L, L), 0)
    si = lax.broadcasted_iota(jnp.int32, (L, L), 1)
    causal = si <= ti

    for h in range(H_MLSTM):
        hs = slice(h * HEAD_DIM, (h + 1) * HEAD_DIM)
        rep = slice(h * LANES, (h + 1) * LANES)
        qh = q_all[:, hs]
        kh = k_all[:, hs]
        vh = v_all[:, hs]
        bc = b_rep[:, rep]
        igc = ig_rep[:, rep]
        br = b_row_all[h:h + 1, :]
        igr = gbt[h:h + 1, :]
        m_prev = m_scr[h:h + 1, :]
        c_prev = c_scr[h]
        n_prev = n_scr[h:h + 1, :]

        log_d = jnp.where(causal, bc - br + igr, NEG)
        m_inter = bc + m_prev
        m_t = jnp.maximum(m_inter, jnp.max(log_d, axis=-1, keepdims=True))
        qh_b = qh.astype(BF16)
        kh_b = kh.astype(BF16)
        sd = _dot_nt(qh_b, kh_b) * jnp.exp(log_d - m_t)
        scale_inter = jnp.exp(m_inter - m_t)
        num = (scale_inter[:, :HEAD_DIM] * _dot_nt(qh_b, c_prev.astype(BF16))
               + _dot(sd.astype(BF16), vh.astype(BF16)))
        den = scale_inter * jnp.sum(qh * n_prev, axis=-1, keepdims=True) + jnp.sum(sd, axis=-1, keepdims=True)
        hh = num / jnp.maximum(jnp.abs(den), jnp.exp(-m_t))[:, :HEAD_DIM]

        m_new = m_t[L - 1:L, :]
        b_last = bc[L - 1:L, :]
        decay = jnp.exp(b_last + m_prev - m_new)
        w_row = jnp.exp(b_last - br + igr - m_new)
        w_col = jnp.exp(b_last - bc + igc - m_new)
        vtw = (vt_all[hs, :] * w_row).astype(BF16)
        c_new = decay[:, :HEAD_DIM] * c_prev + _dot(vtw, kh_b)
        n_new = decay[:, :HEAD_DIM] * n_prev + jnp.sum(w_col[:, :HEAD_DIM] * kh, axis=0, keepdims=True)
        c_scr[h] = c_new
        n_scr[h:h + 1, :] = n_new
        m_scr[h:h + 1, :] = m_new
        c_out_ref[0, h] = c_new
        n_out_ref[0, h:h + 1, :] = n_new
        m_out_ref[0, h:h + 1, :] = m_new

        y = hh * lax.rsqrt(jnp.mean(hh * hh, axis=-1, keepdims=True) + EPS) * ng_ref[:, hs]
        mh_ref[0, :, hs] = _sigmoid(mo[:, hs]) * y


def _mlstm_prompt(mqk, mv, mo, gates, batch, seq, conv_w, conv_b, bg, ng, tril, triu, rep):
    L = MLSTM_CHUNK
    nchunk = seq // L
    chunk = lambda b, c: (b, c, 0)
    fixed = lambda b, c: (0, 0)
    return pl.pallas_call(
        _mlstm_prompt_kernel,
        grid=(batch, nchunk),
        in_specs=[
            pl.BlockSpec((1, L, 2 * W_MLSTM), chunk),
            pl.BlockSpec((1, L, W_MLSTM), chunk),
            pl.BlockSpec((1, L, W_MLSTM), chunk),
            pl.BlockSpec((1, L, LANES), chunk),
            pl.BlockSpec((CONV_W, 2 * W_MLSTM), fixed),
            pl.BlockSpec((1, 2 * W_MLSTM), fixed),
            pl.BlockSpec((1, LANES), fixed),
            pl.BlockSpec((1, W_MLSTM), fixed),
            pl.BlockSpec((L, L), fixed),
            pl.BlockSpec((L, L), fixed),
            pl.BlockSpec(rep.shape, fixed),
        ],
        out_specs=[
            pl.BlockSpec((1, L, W_MLSTM), chunk),
            pl.BlockSpec((1, H_MLSTM, HEAD_DIM, HEAD_DIM), lambda b, c: (b, 0, 0, 0)),
            pl.BlockSpec((1, H_MLSTM, HEAD_DIM), lambda b, c: (b, 0, 0)),
            pl.BlockSpec((1, H_MLSTM, LANES), lambda b, c: (b, 0, 0)),
        ],
        out_shape=[
            jax.ShapeDtypeStruct((batch, seq, W_MLSTM), F32),
            jax.ShapeDtypeStruct((batch, H_MLSTM, HEAD_DIM, HEAD_DIM), F32),
            jax.ShapeDtypeStruct((batch, H_MLSTM, HEAD_DIM), F32),
            jax.ShapeDtypeStruct((batch, H_MLSTM, LANES), F32),
        ],
        scratch_shapes=[
            pltpu.VMEM((L + SUBLANES, 2 * W_MLSTM), F32),
            pltpu.VMEM((H_MLSTM, HEAD_DIM, HEAD_DIM), F32),
            pltpu.VMEM((H_MLSTM, HEAD_DIM), F32),
            pltpu.VMEM((H_MLSTM, LANES), F32),
        ],
        compiler_params=_cparams("parallel", "arbitrary"),
        name="mlstm_prompt",
    )(mqk.reshape(batch, seq, -1), mv.reshape(batch, seq, -1), mo.reshape(batch, seq, -1),
      gates.reshape(batch, seq, -1), conv_w, conv_b, bg, ng, tril, triu, rep)


def _mlstm_sample_pre_kernel(mqk_ref, buf_ref, gates_ref, cw_ref, cb_ref, bg_ref, qk_ref, g_ref):
    conv = cb_ref[...] + cw_ref[CONV_W - 1:CONV_W, :] * mqk_ref[...]
    for j in range(CONV_W - 1):
        conv = conv + cw_ref[j:j + 1, :] * buf_ref[j]
    qk = conv * _sigmoid(conv)
    lane = lax.broadcasted_iota(jnp.int32, qk.shape, 1)
    qk_ref[...] = jnp.where(lane < W_MLSTM, qk, qk * (HEAD_DIM ** -0.5))
    gb = gates_ref[...] + bg_ref[...]
    glane = lax.broadcasted_iota(jnp.int32, gb.shape, 1)
    g_ref[...] = jnp.where(glane < H_MLSTM, gb, _log_sigmoid(gb))


def _mlstm_sample_step_kernel(q_ref, k_ref, v_ref, mo_ref, ig_ref, lf_ref, c0_ref, n0_ref, m0_ref, ng_ref,
                              mh_ref, c_ref, n_ref, m_ref):
    for h in range(H_MLSTM):
        q = q_ref[0, h]
        k = k_ref[0, h]
        v = v_ref[0, h]
        ig = ig_ref[0, h]
        lf = lf_ref[0, h]
        c0 = c0_ref[0, h]
        n0 = n0_ref[0, h]
        m0 = m0_ref[0, h]
        m_inter = lf + m0
        m_t = jnp.maximum(m_inter, ig)
        w_in = jnp.exp(ig - m_t)
        sd = jnp.sum(q * k, axis=-1, keepdims=True) * w_in
        scale_inter = jnp.exp(m_inter - m_t)
        num = scale_inter * jnp.sum(c0 * q, axis=-1, keepdims=True) + sd * v
        den = scale_inter * jnp.sum(n0 * q, axis=-1, keepdims=True) + sd
        hh = num / jnp.maximum(jnp.abs(den), jnp.exp(-m_t))
        decay = jnp.exp(lf + m0 - m_t)
        c_ref[0, h] = decay * c0 + w_in * (v * k)
        n_ref[0, h] = decay * n0 + w_in * k
        m_ref[0, h] = m_t
        y = hh * lax.rsqrt(jnp.mean(hh * hh, axis=0, keepdims=True) + EPS) * ng_ref[h]
        mh_ref[0, h] = _sigmoid(mo_ref[0, h]) * y


def _mlstm_sample(mqk, mv, mo, gates, conv_buf, c0, n0, m0, conv_w, conv_b, bg, ng):
    nb = mqk.shape[0]
    full = lambda *shape: pl.BlockSpec(shape, lambda: (0,) * len(shape))
    qk, g = pl.pallas_call(
        _mlstm_sample_pre_kernel,
        in_specs=[full(nb, 2 * W_MLSTM), full(CONV_W - 1, nb, 2 * W_MLSTM), full(nb, LANES),
                  full(CONV_W, 2 * W_MLSTM), full(1, 2 * W_MLSTM), full(1, LANES)],
        out_specs=[full(nb, 2 * W_MLSTM), full(nb, LANES)],
        out_shape=[jax.ShapeDtypeStruct((nb, 2 * W_MLSTM), F32), jax.ShapeDtypeStruct((nb, LANES), F32)],
        name="mlstm_sample_pre",
    )(mqk, jnp.swapaxes(conv_buf, 0, 1), gates, conv_w, conv_b, bg)
    row = (nb, H_MLSTM, 1, HEAD_DIM)
    col = (nb, H_MLSTM, HEAD_DIM, 1)
    one = (nb, H_MLSTM, 1, 1)
    mat = (nb, H_MLSTM, HEAD_DIM, HEAD_DIM)
    spec = lambda shape: pl.BlockSpec((1,) + shape[1:], lambda b: (b, 0, 0, 0))
    ins = [
        (qk[:, :W_MLSTM].reshape(row), row), (qk[:, W_MLSTM:].reshape(row), row), (mv.reshape(col), col),
        (mo.reshape(col), col), (g[:, :H_MLSTM].reshape(one), one), (g[:, H_MLSTM:2 * H_MLSTM].reshape(one), one),
        (c0, mat), (n0.reshape(row), row), (m0.reshape(one), one),
    ]
    mh, c, n, m = pl.pallas_call(
        _mlstm_sample_step_kernel,
        grid=(nb,),
        in_specs=[spec(s) for _, s in ins] + [pl.BlockSpec((H_MLSTM, HEAD_DIM, 1), lambda b: (0, 0, 0))],
        out_specs=[spec(col), spec(mat), spec(row), spec(one)],
        out_shape=[jax.ShapeDtypeStruct(s, F32) for s in (col, mat, row, one)],
        compiler_params=_cparams("parallel"),
        name="mlstm_sample_step",
    )(*[a for a, _ in ins], ng.reshape(H_MLSTM, HEAD_DIM, 1))
    return mh.reshape(nb, W_MLSTM), c, n.reshape(nb, H_MLSTM, HEAD_DIM), m.reshape(nb, H_MLSTM)


def _outproj_kernel(x_ref, att_ref, mh_ref, w_ref, g_ref, x1_ref, hn_ref):
    x1 = (x_ref[...] + _dot(att_ref[...].astype(BF16), w_ref[0:W_ATT, :])
          + _dot(mh_ref[...].astype(BF16), w_ref[W_ATT:, :]))
    x1_ref[...] = x1
    ms = jnp.mean(x1 * x1, axis=-1, keepdims=True)
    hn_ref[...] = (x1 * lax.rsqrt(ms + EPS) * g_ref[...]).astype(BF16)


def _outproj_into_kernel(x1_any, hn_any, *refs):
    del x1_any, hn_any
    _outproj_kernel(*refs)


def _out_projection(x2d, att, mh, w_out, g2, tm, n_out, out_rows, into=None):
    n_in = x2d.shape[0]
    in_blocks = n_in // tm
    first = out_rows[0] // tm
    row = lambda i: (i % in_blocks, 0)
    out_row = lambda i: (first + i, 0)
    fixed = lambda i: (0, 0)
    half = pl.BlockSpec((tm, W_ATT), row)
    in_specs = [pl.BlockSpec((tm, D_MODEL), row), half, half,
                pl.BlockSpec((D_MODEL, D_MODEL), fixed), pl.BlockSpec((1, D_MODEL), fixed)]
    args = (x2d, att, mh, w_out, g2)
    body = _outproj_kernel
    aliases = {}
    if into is not None:
        in_specs = [pl.BlockSpec(memory_space=pl.ANY)] * 2 + in_specs
        args = (*into, *args)
        body = _outproj_into_kernel
        aliases = {0: 0, 1: 1}
    return pl.pallas_call(
        body,
        grid=((out_rows[1] - out_rows[0]) // tm,),
        in_specs=in_specs,
        out_specs=[pl.BlockSpec((tm, D_MODEL), out_row), pl.BlockSpec((tm, D_MODEL), out_row)],
        out_shape=[jax.ShapeDtypeStruct((n_out, D_MODEL), F32), jax.ShapeDtypeStruct((n_out, D_MODEL), BF16)],
        input_output_aliases=aliases,
        compiler_params=_cparams("parallel"),
        name="out_projection",
    )(*args)


def _peer_scores_kernel(hn_ref, wq_ref, g_ref, sk_ref, s_ref):
    q = _dot(hn_ref[...], wq_ref[...])
    half = PEER_DKEY // 2
    for h in range(PEER_HEADS):
        qh = q[:, h * PEER_DKEY:(h + 1) * PEER_DKEY]
        qn = qh * lax.rsqrt(jnp.mean(qh * qh, axis=-1, keepdims=True) + EPS) * g_ref[...]
        for part in range(2):
            s = _dot_nt(qn[:, part * half:(part + 1) * half].astype(BF16), sk_ref[2 * h + part])
            r0 = (2 * h + part) * PEER_NKEYS
            s_ref[r0:r0 + PEER_NKEYS, :] = s.T


def _peer_scores(hn, wq, g, sk, tm):
    n = hn.shape[0]
    rows = PEER_HEADS * 2 * PEER_NKEYS
    return pl.pallas_call(
        _peer_scores_kernel,
        grid=(n // tm,),
        in_specs=[pl.BlockSpec((tm, D_MODEL), lambda i: (i, 0)),
                  pl.BlockSpec(wq.shape, lambda i: (0, 0)),
                  pl.BlockSpec((1, PEER_DKEY), lambda i: (0, 0)),
                  pl.BlockSpec(sk.shape, lambda i: (0, 0, 0))],
        out_specs=pl.BlockSpec((rows, tm), lambda i: (0, i)),
        out_shape=jax.ShapeDtypeStruct((rows, n), F32),
        compiler_params=_cparams("parallel"),
        name="peer_scores",
    )(hn, wq, g, sk)


_STAIR = [(a, b) for a in range(PEER_TOPK) for b in range(PEER_TOPK) if (a + 1) * (b + 1) <= PEER_TOPK]


def _peer_select_kernel(s1_ref, s2_ref, rank2_ref, cnt1_ref, e1_ref, e2_ref,
                        work1_ref, work2_ref, rank1_ref, rank2s_ref, vals1_ref, vals2_ref, cnt_ref, cnt1s_ref):
    shape = work1_ref.shape
    keyf = lax.broadcasted_iota(jnp.int32, shape, 0).astype(F32)
    chunks = [slice(c * LANES, (c + 1) * LANES) for c in range(SUBLANES)]
    halves = ((s1_ref, work1_ref, rank1_ref, vals1_ref), (s2_ref, work2_ref, rank2s_ref, vals2_ref))

    def load_scores():
        for src_ref, work_ref, rank_ref, _ in halves:
            for c, cs in enumerate(chunks):
                work_ref[:, c, :] = src_ref[:, cs]
            rank_ref[...] = jnp.full(shape, float(PEER_TOPK), F32)

    def top16_distinct():
        def one_round(a, previous):
            maxima = []
            for (_, work_ref, rank_ref, vals_ref), prev in zip(halves, previous):
                w = work_ref[...]
                rank_ref[...] = jnp.where(w == prev, jnp.asarray(a - 1, F32), rank_ref[...])
                mx = jnp.max(jnp.where(w < prev, w, -jnp.inf), axis=0, keepdims=True)
                vals_ref[pl.ds(a, 1)] = mx
                maxima.append(mx)
            return tuple(maxima)

        top = jnp.full((1,) + shape[1:], jnp.inf, F32)
        last = lax.fori_loop(0, PEER_TOPK, one_round, (top, top))
        for (_, work_ref, rank_ref, _), prev in zip(halves, last):
            rank_ref[...] = jnp.where(work_ref[...] == prev, float(PEER_TOPK - 1), rank_ref[...])

    def top16_ties():
        def one_round(a, carry):
            for _, work_ref, rank_ref, vals_ref in halves:
                w = work_ref[...]
                mx = jnp.max(w, axis=0, keepdims=True)
                sel = keyf == jnp.min(jnp.where(w == mx, keyf, float(PEER_NKEYS)), axis=0, keepdims=True)
                work_ref[...] = jnp.where(sel, -jnp.inf, w)
                rank_ref[...] = jnp.where(sel, jnp.asarray(a, F32), rank_ref[...])
                vals_ref[pl.ds(a, 1)] = mx
            return carry

        lax.fori_loop(0, PEER_TOPK, one_round, 0)

    load_scores()
    top16_distinct()
    taken = [jnp.sum(jnp.where(rank_ref[...] < float(PEER_TOPK), 1.0, 0.0), axis=0) for _, _, rank_ref, _ in halves]
    ties = jnp.max(jnp.maximum(jnp.abs(taken[0] - PEER_TOPK), jnp.abs(taken[1] - PEER_TOPK))) > 0.0

    @pl.when(ties)
    def _():
        load_scores()
        top16_ties()

    v1 = [vals1_ref[a] for a in range(PEER_TOPK)]
    v2 = [vals2_ref[b] for b in range(PEER_TOPK)]
    cand = [v1[a] + v2[b] for a, b in _STAIR]
    pos = []
    for i, (a, b) in enumerate(_STAIR):
        static = sum(1 for (a2, b2) in _STAIR if a2 <= a and b2 <= b and (a2, b2) != (a, b))
        pos.append(jnp.full(cand[0].shape, float(static), F32))
    for i, (ai, bi) in enumerate(_STAIR):
        for j in range(i + 1, len(_STAIR)):
            aj, bj = _STAIR[j]
            if ai <= aj and bi <= bj:
                continue
            i_first = cand[i] >= cand[j]
            pos[j] = pos[j] + jnp.where(i_first, 1.0, 0.0)
            pos[i] = pos[i] + jnp.where(i_first, 0.0, 1.0)
    e1s = [jnp.exp(v1[a] - v1[0]) for a in range(PEER_TOPK)]
    e2s = [jnp.exp(v2[b] - v2[0]) for b in range(PEER_TOPK)]
    z = jnp.zeros_like(cand[0])
    for a in range(PEER_TOPK):
        cnt_a = jnp.zeros_like(z)
        za = jnp.zeros_like(z)
        for i, (a2, b) in enumerate(_STAIR):
            if a2 != a:
                continue
            chosen = pos[i] < float(PEER_TOPK)
            cnt_a = cnt_a + jnp.where(chosen, 1.0, 0.0)
            za = za + jnp.where(chosen, e2s[b], 0.0)
        cnt_ref[a] = cnt_a
        z = z + e1s[a] * za
    z_inv = 1.0 / z

    cnt1s_ref[...] = jnp.zeros(shape, F32)

    def spread(a, carry):
        cnt1s_ref[...] = jnp.where(rank1_ref[...] == jnp.asarray(a, F32), cnt_ref[pl.ds(a, 1)], cnt1s_ref[...])
        return carry

    lax.fori_loop(0, PEER_TOPK, spread, 0)
    for c, cs in enumerate(chunks):
        rank2_ref[:, cs] = rank2s_ref[:, c, :].astype(BF16)
        cnt1_ref[:, cs] = cnt1s_ref[:, c, :]
        e1_ref[:, cs] = jnp.exp(s1_ref[:, cs] - v1[0][c:c + 1, :]) * z_inv[c:c + 1, :]
        e2_ref[:, cs] = jnp.exp(s2_ref[:, cs] - v2[0][c:c + 1, :]).astype(BF16)


def _peer_select(scores_t):
    n = scores_t.shape[1]
    tok = LANES * SUBLANES
    blk = (PEER_NKEYS, tok)
    blk3 = (PEER_NKEYS, SUBLANES, LANES)
    small = (PEER_TOPK, SUBLANES, LANES)
    out_rows = PEER_HEADS * PEER_NKEYS
    return pl.pallas_call(
        _peer_select_kernel,
        grid=(PEER_HEADS, n // tok),
        in_specs=[pl.BlockSpec(blk, lambda h, g: (2 * h, g)), pl.BlockSpec(blk, lambda h, g: (2 * h + 1, g))],
        out_specs=[pl.BlockSpec(blk, lambda h, g: (h, g))] * 4,
        out_shape=[jax.ShapeDtypeStruct((out_rows, n), dt) for dt in (BF16, F32, F32, BF16)],
        scratch_shapes=[pltpu.VMEM(blk3, F32)] * 4 + [pltpu.VMEM(small, F32)] * 3 + [pltpu.VMEM(blk3, F32)],
        compiler_params=_cparams("parallel", "parallel"),
        name="peer_select",
    )(scores_t, scores_t)


PEER_TOK_TILE = 512
PEER_I1_PER_STEP = 8
PEER_MXU_CHUNKS = 4
BF16_ROWS = 16


def _peer_dense_kernel(nblk, hn_ref, u_ref, vt_prev_ref, vt_last_ref, rank2_ref, e2_ref, cnt1_ref, e1_ref, x1_ref, y_ref,
                       ht_ref, p_even_ref, p_odd_ref, acc_ref):
    j = pl.program_id(1)
    tokens = hn_ref.shape[0]
    reps = PEER_NKEYS // BF16_ROWS
    assert PEER_I1_PER_STEP == SUBLANES

    def row_bf16(tile, s):
        x8 = jnp.broadcast_to(tile[s:s + 1, :], (SUBLANES, tokens))
        x16 = jnp.concatenate([x8, x8], axis=0).astype(BF16)
        return jnp.concatenate([x16] * reps, axis=0)

    def evaluate(p_write, p_read):
        first = pl.multiple_of(j * PEER_I1_PER_STEP, SUBLANES)
        cnt_tiles = [cnt1_ref[pl.ds(h * PEER_NKEYS + first, SUBLANES), :] for h in range(PEER_HEADS)]
        e1_tiles = [e1_ref[pl.ds(h * PEER_NKEYS + first, SUBLANES), :] for h in range(PEER_HEADS)]
        per_chunk = PEER_I1_PER_STEP // PEER_MXU_CHUNKS
        crow = per_chunk * PEER_NKEYS
        for c in range(PEER_MXU_CHUNKS):
            rc = slice(c * crow, (c + 1) * crow)
            a_c = _dot(u_ref[rc, :], ht_ref[...])
            if p_read is not None:
                acc_ref[rc, :] += _dot(vt_prev_ref[rc, :], p_read[...])
            for sc in range(per_chunk):
                s = c * per_chunk + sc
                g = None
                for h in range(PEER_HEADS):
                    rows = slice(h * PEER_NKEYS, (h + 1) * PEER_NKEYS)
                    w = jnp.where(rank2_ref[rows, :] < row_bf16(cnt_tiles[h], s),
                                  e2_ref[rows, :] * row_bf16(e1_tiles[h], s), jnp.zeros((), BF16))
                    g = w if g is None else g + w
                a = a_c[sc * PEER_NKEYS:(sc + 1) * PEER_NKEYS, :]
                act = a + a * lax.erf(a * (2.0 ** -0.5))
                p_write[s * PEER_NKEYS:(s + 1) * PEER_NKEYS, :] = g * act.astype(BF16)

    @pl.when(j == 0)
    def _():
        acc_ref[...] = jnp.zeros_like(acc_ref)
        ht_ref[...] = hn_ref[...].astype(F32).T.astype(BF16)
        evaluate(p_even_ref, None)

    @pl.when(j % 2 == 1)
    def _():
        evaluate(p_odd_ref, p_even_ref)

    @pl.when(jnp.logical_and(j % 2 == 0, j > 0))
    def _():
        evaluate(p_even_ref, p_odd_ref)

    @pl.when(j == nblk - 1)
    def _():
        p_last = p_odd_ref if nblk % 2 == 0 else p_even_ref
        acc = acc_ref[...] + _dot(vt_last_ref[...], p_last[...])
        y_ref[...] = x1_ref[...] + acc.T


def _peer_dense(hn, u, vt, rank2, e2, cnt1, e1, x1, n):
    T = PEER_TOK_TILE
    eb = PEER_I1_PER_STEP * PEER_NKEYS
    nexp = u.shape[0]
    sel_rows = PEER_HEADS * PEER_NKEYS
    nblk = nexp // eb
    tok = lambda t, j: (0, t)
    return pl.pallas_call(
        functools.partial(_peer_dense_kernel, nblk),
        grid=(n // T, nblk),
        in_specs=[
            pl.BlockSpec((T, D_MODEL), lambda t, j: (t, 0)),
            pl.BlockSpec((eb, D_MODEL), lambda t, j: (j, 0)),
            pl.BlockSpec((D_MODEL, eb), lambda t, j: (0, jnp.maximum(j - 1, 0))),
            pl.BlockSpec((D_MODEL, eb), lambda t, j: (0, nblk - 1)),
            pl.BlockSpec((sel_rows, T), tok),
            pl.BlockSpec((sel_rows, T), tok),
            pl.BlockSpec((sel_rows, T), tok),
            pl.BlockSpec((sel_rows, T), tok),
            pl.BlockSpec((T, D_MODEL), lambda t, j: (t, 0)),
        ],
        out_specs=pl.BlockSpec((T, D_MODEL), lambda t, j: (t, 0)),
        out_shape=jax.ShapeDtypeStruct((n, D_MODEL), F32),
        scratch_shapes=[pltpu.VMEM((D_MODEL, T), BF16), pltpu.VMEM((eb, T), BF16), pltpu.VMEM((eb, T), BF16), pltpu.VMEM((D_MODEL, T), F32)],
        compiler_params=_cparams("parallel", "arbitrary"),
        name="peer_dense",
    )(hn, u, vt, vt, rank2, e2, cnt1, e1, x1)


def _tri_constants():
    L = MLSTM_CHUNK
    tril = np.tril(np.ones((L, L), np.float32))
    rep = (np.arange(LANES)[:, None] == np.arange(2 * H_MLSTM * LANES)[None, :] // LANES).astype(np.float32)
    return jnp.asarray(tril, BF16), jnp.asarray(tril.T, BF16), jnp.asarray(rep, BF16)


def _head_constants():
    bd = np.kron(np.eye(H_ATT, dtype=np.float32), np.full((HEAD_DIM, HEAD_DIM), 1.0 / HEAD_DIM, np.float32))
    return jnp.asarray(bd, BF16)


def kernel(x_prompt, x_sample, cache_attn_k, cache_attn_v, state_mlstm_C, state_mlstm_n, state_mlstm_m,
           state_mlstm_conv, norm1_g, w_in, att_qnorm_g, att_knorm_g, b_gates, mlstm_conv_w, mlstm_conv_b,
           mlstm_norm_g, w_out, norm2_g, peer_w_query, peer_qnorm_g, peer_subkeys, peer_u, peer_v):
    batch, seq = x_prompt.shape[:2]
    nsamp = x_sample.shape[0]
    past_len = 16384
    assert norm1_g.shape[0] == 1 and x_sample.shape[1] == 1 and seq % MLSTM_CHUNK == 0
    wb = cache_attn_k.shape[2]
    li = 0
    bd = _head_constants()
    tril, triu, gate_rep = _tri_constants()

    w = w_in[li]
    gate_lo, gate_hi = 3072, 3072 + 2 * H_MLSTM
    w_main = jnp.concatenate([w[:, :gate_lo], w[:, gate_hi:]], axis=1).astype(BF16)
    wg = jnp.pad(w[:, gate_lo:gate_hi], ((0, 0), (0, LANES - 2 * H_MLSTM)))
    wg_hi = wg.astype(BF16)
    wg_lo = (wg - wg_hi.astype(F32)).astype(BF16)
    g1 = norm1_g[li][None, :]
    qg = jnp.tile(att_qnorm_g[li], H_ATT)[None, :]
    kg = jnp.tile(att_knorm_g[li], H_ATT)[None, :]
    bg = jnp.pad(b_gates[li], (0, LANES - 2 * H_MLSTM))[None, :]
    conv_w = mlstm_conv_w[li]
    conv_b = mlstm_conv_b[li][None, :]
    ng = mlstm_norm_g[li][None, :]
    wo = w_out[li].astype(BF16)
    g2 = norm2_g[li][None, :]
    wq = peer_w_query[li].astype(BF16)
    pqg = peer_qnorm_g[li][None, :]
    sk = peer_subkeys[li].reshape(PEER_HEADS * 2, PEER_NKEYS, PEER_DKEY // 2).astype(BF16)
    u_b = peer_u[li].astype(BF16)
    vt_b = (0.5 * peer_v[li]).astype(BF16).T

    n_p = batch * seq
    xp2 = x_prompt.reshape(n_p, D_MODEL)
    pos_p = jnp.arange(seq, dtype=jnp.int32)
    q, k, v, mqk, mv, mo, gates, kt, vt = _in_projection(xp2, pos_p, 512, g1, w_main, wg_hi, wg_lo, qg, kg, bd,
                                                         transposed_kv=True)
    att_p = _attn_prompt(q, k, v, batch, seq)
    mh_p, c_p, n_p_state, m_p = _mlstm_prompt(mqk, mv, mo, gates, batch, seq, conv_w, conv_b, bg, ng, tril, triu,
                                                  gate_rep)
    n_all = n_p + nsamp
    group = LANES * SUBLANES
    n_pad = -(-n_all // group) * group
    x1_all, hn_all = _out_projection(xp2, att_p, mh_p.reshape(n_p, W_MLSTM), wo, g2, 256, n_pad, (0, n_pad))
    wbp = min(wb, seq)
    new_k_prompt = jnp.transpose(kt, (0, 3, 1, 2))[None, :, seq - wbp:]
    new_v_prompt = jnp.transpose(vt, (0, 3, 1, 2))[None, :, seq - wbp:]
    new_conv_prompt = mqk.reshape(batch, seq, -1)[None, :, seq - (CONV_W - 1):]

    xs2 = x_sample.reshape(nsamp, D_MODEL)
    pos_s = jnp.full((nsamp,), past_len, dtype=jnp.int32)
    qs, ks, vs, mqk_s, mv_s, mo_s, gates_s = _in_projection(xs2, pos_s, nsamp, g1, w_main, wg_hi, wg_lo, qg, kg, bd)
    cache_kt = jnp.transpose(cache_attn_k[li], (0, 2, 3, 1))
    cache_vt = jnp.transpose(cache_attn_v[li], (0, 2, 3, 1))
    att_s = _attn_sample(qs, ks, vs, cache_kt, cache_vt)
    mh_s, c_s, n_s, m_s = _mlstm_sample(mqk_s, mv_s, mo_s, gates_s, state_mlstm_conv[li], state_mlstm_C[li],
                                        state_mlstm_n[li], state_mlstm_m[li], conv_w, conv_b, bg, ng)
    x1_all, hn_all = _out_projection(xs2, att_s, mh_s, wo, g2, nsamp, n_pad, (n_p, n_all),
                                     into=(x1_all, hn_all))
    new_conv_sample = jnp.concatenate([state_mlstm_conv[li][:, 1:], mqk_s[:, None, :]], axis=1)[None]

    scores_t = _peer_scores(hn_all, wq, pqg, sk, 256)
    rank2, cnt1, e1, e2 = _peer_select(scores_t)
    n_dense = -(-n_all // PEER_TOK_TILE) * PEER_TOK_TILE
    y_all = _peer_dense(hn_all, u_b, vt_b, rank2, e2, cnt1, e1, x1_all, n_dense)
    y_prompt = y_all[:n_p].reshape(batch, seq, D_MODEL)
    y_sample = y_all[n_p:n_all].reshape(nsamp, 1, D_MODEL)

    return (y_prompt, y_sample, new_k_prompt, new_v_prompt,
            ks.reshape(1, nsamp, 1, H_ATT, HEAD_DIM), vs.reshape(1, nsamp, 1, H_ATT, HEAD_DIM),
            c_p[None], n_p_state[None], m_p[None, :, :, 0], new_conv_prompt,
            c_s[None], n_s[None], m_s[None], new_conv_sample)
```

```python
import functools

import numpy as np
import jax
import jax.numpy as jnp
from jax import lax
from jax.experimental import pallas as pl
from jax.experimental.pallas import tpu as pltpu

F32 = jnp.float32
BF16 = jnp.bfloat16

D_MODEL = 1024
HEAD_DIM = 64
W_ATT = 512
W_MLSTM = 512
H_ATT = 8
H_MLSTM = 8
DILATIONS = ((128, 1), (512, 4), (2048, 16))
ATT_BLOCK = 128
ROPE_THETA = 10000.0
CONV_W = 4
PEER_HEADS = 8
PEER_NKEYS = 128
PEER_DKEY = 256
PEER_TOPK = 16
EPS = 1e-6
NEG = -1e30

LANES = 128
SUBLANES = 8
VMEM_LIMIT = 56 * 1024 * 1024


def _cparams(*sem):
    return pltpu.CompilerParams(dimension_semantics=sem, vmem_limit_bytes=VMEM_LIMIT)


def _split2(x):
    hi = x.astype(BF16)
    lo = (x - hi.astype(F32)).astype(BF16)
    return hi, lo


def _split3(x):
    hi = x.astype(BF16)
    r = x - hi.astype(F32)
    mid = r.astype(BF16)
    lo = (r - mid.astype(F32)).astype(BF16)
    return hi, mid, lo


def _dot(a, b):
    return jnp.dot(a, b, preferred_element_type=F32)


def _dot_nt(a, b):
    return lax.dot_general(a, b, (((1,), (1,)), ((), ())), preferred_element_type=F32)


def _dot_parts(parts, b):
    acc = _dot(parts[0], b)
    for p in parts[1:]:
        acc = acc + _dot(p, b)
    return acc


def _sigmoid(x):
    return 1.0 / (1.0 + jnp.exp(-x))


def _log_sigmoid(x):
    return jnp.minimum(x, 0.0) - jnp.log1p(jnp.exp(-jnp.abs(x)))


def _inproj_kernel(x_ref, g_ref, w_ref, wgh_ref, wgl_ref, qg_ref, kg_ref, cos_ref, sin_ref, bd_ref,
                   q_ref, k_ref, v_ref, mqk_ref, mv_ref, mo_ref, gates_ref, kt_ref=None, vt_ref=None):
    x = x_ref[...]
    ms = jnp.mean(x * x, axis=-1, keepdims=True)
    xn = x * lax.rsqrt(ms + EPS) * g_ref[...]
    xh, xl = _split2(xn)

    def seg(lo, hi):
        return _dot(xh, w_ref[:, lo:hi])

    bd = bd_ref[...]
    cos = cos_ref[...]
    sin = sin_ref[...]
    lane = lax.broadcasted_iota(jnp.int32, cos.shape, 1)
    first_half = (lane % HEAD_DIM) < (HEAD_DIM // 2)

    def head_norm_rope(a, g):
        sq = a * a
        hi, lo = _split2(sq)
        msq = _dot(hi, bd) + _dot(lo, bd)
        y = a * lax.rsqrt(msq + EPS) * g
        rot = jnp.where(first_half, pltpu.roll(y, W_ATT - HEAD_DIM // 2, 1), pltpu.roll(y, HEAD_DIM // 2, 1))
        return y * cos + rot * sin

    q_ref[...] = head_norm_rope(seg(0, 512), qg_ref[...]) * (HEAD_DIM ** -0.5)
    k = head_norm_rope(seg(512, 1024), kg_ref[...])
    v = seg(1024, 1536)
    k_ref[...] = k
    v_ref[...] = v
    if kt_ref is not None:
        kt_ref[0] = k.T.reshape(H_ATT, HEAD_DIM, k.shape[0])
        vt_ref[0] = v.T.reshape(H_ATT, HEAD_DIM, v.shape[0])
    mqk_ref[...] = seg(1536, 2560)
    mv_ref[...] = seg(2560, 3072)
    mo_ref[...] = seg(3072, 3584)
    gates_ref[...] = _dot(xh, wgh_ref[...]) + _dot(xl, wgh_ref[...]) + _dot(xh, wgl_ref[...])


def _rope_tables(pos):
    half = HEAD_DIM // 2
    inv = ROPE_THETA ** (-jnp.arange(half, dtype=F32) / half)
    ang = pos.astype(F32)[:, None] * inv[None, :]
    cos = jnp.cos(ang)
    sin = jnp.sin(ang)
    cos_h = jnp.concatenate([cos, cos], axis=-1)
    sin_h = jnp.concatenate([-sin, sin], axis=-1)
    return jnp.tile(cos_h, (1, H_ATT)), jnp.tile(sin_h, (1, H_ATT))


def _in_projection(x2d, pos, tm, norm_g, w_main, wg_hi, wg_lo, qg, kg, bd, transposed_kv=False):
    n = x2d.shape[0]
    cos, sin = _rope_tables(pos)
    pblocks = pos.shape[0] // tm
    row = lambda i: (i, 0)
    fixed = lambda i: (0, 0)
    tab = lambda i: (i % pblocks, 0)
    widths = (512, 512, 512, 1024, 512, 512, LANES)
    out_specs = [pl.BlockSpec((tm, w), row) for w in widths]
    out_shape = [jax.ShapeDtypeStruct((n, w), F32) for w in widths]
    if transposed_kv:
        tshape = (n // pos.shape[0], H_ATT, HEAD_DIM, pos.shape[0])
        out_specs += [pl.BlockSpec((1, H_ATT, HEAD_DIM, tm), lambda i: (i // pblocks, 0, 0, i % pblocks))] * 2
        out_shape += [jax.ShapeDtypeStruct(tshape, F32)] * 2
    return pl.pallas_call(
        _inproj_kernel,
        grid=(n // tm,),
        in_specs=[
            pl.BlockSpec((tm, D_MODEL), row),
            pl.BlockSpec((1, D_MODEL), fixed),
            pl.BlockSpec(w_main.shape, fixed),
            pl.BlockSpec(wg_hi.shape, fixed),
            pl.BlockSpec(wg_lo.shape, fixed),
            pl.BlockSpec((1, W_ATT), fixed),
            pl.BlockSpec((1, W_ATT), fixed),
            pl.BlockSpec((tm, W_ATT), tab),
            pl.BlockSpec((tm, W_ATT), tab),
            pl.BlockSpec((W_ATT, W_ATT), fixed),
        ],
        out_specs=out_specs,
        out_shape=out_shape,
        compiler_params=_cparams("parallel"),
        name="in_projection",
    )(x2d, norm_g, w_main, wg_hi, wg_lo, qg, kg, cos, sin, bd)


def _merge_branches(outs, lses):
    m = functools.reduce(jnp.maximum, lses)
    es = [jnp.exp(l - m) for l in lses]
    return sum(e * o for e, o in zip(es, outs)) / sum(es)


ATT_GROUP = 4


def _attn_prompt_kernel(q_ref, k_ref, v_ref, att_ref, o_scr, l_scr):
    seq = q_ref.shape[1]
    blk = ATT_BLOCK
    lane = lax.broadcasted_iota(jnp.int32, (1, LANES), 1)
    ii = lax.broadcasted_iota(jnp.int32, (2 * blk, 2 * blk), 0) % blk
    jj = lax.broadcasted_iota(jnp.int32, (2 * blk, 2 * blk), 1)
    mask_prev = jnp.minimum(jj - ii, ii + blk - jj) >= 0
    mask_first = (lax.broadcasted_iota(jnp.int32, (2 * blk, blk), 1)
                  <= lax.broadcasted_iota(jnp.int32, (2 * blk, blk), 0) % blk)

    first_head = (lane // HEAD_DIM) == 0

    def attend(q, k2, v2, mask):
        q2 = jnp.concatenate([jnp.where(first_head, q, 0.0), jnp.where(first_head, 0.0, q)], axis=0).astype(BF16)
        s = jnp.where(mask, _dot_nt(q2, k2), NEG)
        m = jnp.max(s, axis=-1, keepdims=True)
        p = jnp.exp(s - m)
        l = jnp.sum(p, axis=-1, keepdims=True)
        o = _dot((p / l).astype(BF16), v2)
        lse = jnp.broadcast_to(m + jnp.log(l), o.shape)
        return jnp.where(first_head, o[:blk], o[blk:]), jnp.where(first_head, lse[:blk], lse[blk:])

    def group(branch, dil, starts, chained, first_has_prev):
        def rows(at):
            return pl.ds(at, blk, stride=dil) if dil > 1 else pl.ds(at, blk)

        qs = [q_ref[0, rows(at), :] for at in starts]
        ks = [k_ref[0, rows(at), :].astype(BF16) for at in starts]
        vs = [v_ref[0, rows(at), :].astype(BF16) for at in starts]
        k_before = v_before = None
        if chained and first_has_prev:
            before = starts[0] - blk * dil
            k_before = k_ref[0, rows(before), :].astype(BF16)
            v_before = v_ref[0, rows(before), :].astype(BF16)
        results = []
        for i in range(len(starts)):
            kp, vp = (k_before, v_before) if i == 0 else (ks[i - 1], vs[i - 1])
            if chained and kp is not None:
                results.append(attend(qs[i], jnp.concatenate([kp, ks[i]], axis=0),
                                      jnp.concatenate([vp, vs[i]], axis=0), mask_prev))
            else:
                results.append(attend(qs[i], ks[i], vs[i], mask_first))
        for at, (o_pair, l_pair) in zip(starts, results):
            o_scr[branch, rows(at), :] = o_pair
            l_scr[branch, rows(at), :] = l_pair

    for branch, (_, dil) in enumerate(DILATIONS):
        nblocks = seq // dil // blk
        step = blk * dil
        if nblocks == 1:
            def classes(g, carry, branch=branch, dil=dil):
                group(branch, dil, [g * ATT_GROUP + i for i in range(ATT_GROUP)], False, False)
                return carry

            lax.fori_loop(0, dil // ATT_GROUP, classes, 0)
        else:
            assert nblocks % ATT_GROUP == 0

            def residue(r, carry, branch=branch, dil=dil, nblocks=nblocks, step=step):
                group(branch, dil, [r + i * step for i in range(ATT_GROUP)], True, False)

                def later(g, c):
                    base = g * (ATT_GROUP * step)
                    base = pl.multiple_of(base, blk) if dil == 1 else r + base
                    group(branch, dil, [base + i * step for i in range(ATT_GROUP)], True, True)
                    return c

                if nblocks > ATT_GROUP:
                    lax.fori_loop(1, nblocks // ATT_GROUP, later, 0)
                return carry

            if dil == 1:
                residue(0, 0)
            else:
                lax.fori_loop(0, dil, residue, 0)

    nbr = len(DILATIONS)
    att_ref[0] = _merge_branches([o_scr[i] for i in range(nbr)], [l_scr[i] for i in range(nbr)])


def _attn_prompt(q, k, v, batch, seq):
    shape3 = (batch, seq, W_ATT)
    blk = pl.BlockSpec((1, seq, LANES), lambda b, p: (b, 0, p))
    att = pl.pallas_call(
        _attn_prompt_kernel,
        grid=(batch, W_ATT // LANES),
        in_specs=[blk, blk, blk],
        out_specs=blk,
        out_shape=jax.ShapeDtypeStruct(shape3, F32),
        scratch_shapes=[pltpu.VMEM((len(DILATIONS), seq, LANES), F32)] * 2,
        compiler_params=_cparams("parallel", "parallel"),
        name="attn_prompt",
    )(q.reshape(shape3), k.reshape(shape3), v.reshape(shape3))
    return att.reshape(batch * seq, W_ATT)


def _attn_sample_kernel(q_ref, kn_ref, vn_ref, kt_ref, vt_ref, att_ref):
    q = q_ref[0]
    kn = kn_ref[0]
    vn = vn_ref[0]
    kt = kt_ref[0]
    vt = vt_ref[0]
    wb = kt.shape[-1]
    scores = jnp.sum(kt * q, axis=1, keepdims=True)
    s_new = jnp.sum(q * kn, axis=1, keepdims=True)
    row = lax.broadcasted_iota(jnp.int32, (1, 1, wb), 2)
    outs, lses = [], []
    for _, dil in DILATIONS:
        on_grid = jnp.where(row % dil == 0, row, -1)
        s = jnp.where(on_grid >= wb - ATT_BLOCK * dil, scores, NEG)
        m = jnp.maximum(jnp.max(s, axis=2, keepdims=True), s_new)
        p = jnp.exp(s - m)
        p_new = jnp.exp(s_new - m)
        l = jnp.sum(p, axis=2, keepdims=True) + p_new
        outs.append((jnp.sum(vt * p, axis=2, keepdims=True) + p_new * vn) / l)
        lses.append(m + jnp.log(l))
    att_ref[0] = _merge_branches(outs, lses)


def _attn_sample(q, k_new, v_new, cache_kt, cache_vt):
    nb, wb = cache_kt.shape[0], cache_kt.shape[-1]
    assert all(wb % dil == 0 and wb >= ATT_BLOCK * dil for _, dil in DILATIONS)
    col = (nb, H_ATT, HEAD_DIM, 1)
    col_spec = pl.BlockSpec((1, H_ATT, HEAD_DIM, 1), lambda b: (b, 0, 0, 0))
    cache_spec = pl.BlockSpec((1, H_ATT, HEAD_DIM, wb), lambda b: (b, 0, 0, 0))
    att = pl.pallas_call(
        _attn_sample_kernel,
        grid=(nb,),
        in_specs=[col_spec] * 3 + [cache_spec] * 2,
        out_specs=col_spec,
        out_shape=jax.ShapeDtypeStruct(col, F32),
        compiler_params=_cparams("parallel"),
        name="attn_sample",
    )(q.reshape(col), k_new.reshape(col), v_new.reshape(col), cache_kt, cache_vt)
    return att.reshape(nb, W_ATT)


MLSTM_CHUNK = 128


def _mlstm_prompt_kernel(mqk_ref, mv_ref, mo_ref, gates_ref, cw_ref, cb_ref, bg_ref, ng_ref, tril_ref, triu_ref,
                         rep_ref, mh_ref, c_out_ref, n_out_ref, m_out_ref,
                         conv_scr, c_scr, n_scr, m_scr):
    L = MLSTM_CHUNK
    c_idx = pl.program_id(1)

    @pl.when(c_idx == 0)
    def _():
        conv_scr[0:SUBLANES, :] = jnp.zeros((SUBLANES, 2 * W_MLSTM), F32)
        c_scr[...] = jnp.zeros_like(c_scr)
        n_scr[...] = jnp.zeros_like(n_scr)
        m_scr[...] = jnp.zeros_like(m_scr)

    x = mqk_ref[0]
    conv_scr[SUBLANES:SUBLANES + L, :] = x
    conv = cb_ref[...] + cw_ref[CONV_W - 1:CONV_W, :] * x
    for j in range(CONV_W - 1):
        shift = CONV_W - 1 - j
        conv = conv + cw_ref[j:j + 1, :] * conv_scr[SUBLANES - shift:SUBLANES - shift + L, :]
    conv_scr[0:SUBLANES, :] = x[L - SUBLANES:L, :]
    qk = conv * _sigmoid(conv)
    q_all = qk[:, :W_MLSTM]
    k_all = qk[:, W_MLSTM:] * (HEAD_DIM ** -0.5)
    v_all = mv_ref[0]
    vt_all = v_all.T
    mo = mo_ref[0]

    assert L == LANES
    gb = gates_ref[0] + bg_ref[...]
    gbt = gb.T
    g_rep = _dot_parts(_split3(gb), rep_ref[...])
    ig_rep = g_rep[:, :H_MLSTM * LANES]
    lf_rep = _log_sigmoid(g_rep[:, H_MLSTM * LANES:])
    lf_row = _log_sigmoid(gbt[H_MLSTM:2 * H_MLSTM, :])
    tril = tril_ref[...]
    b_rep = _dot(tril, lf_rep.astype(BF16))
    rest = lf_rep - lf_rep.astype(BF16).astype(F32)
    b_rep = b_rep + _dot(tril, rest.astype(BF16))
    b_rep = b_rep + _dot(tril, (rest - rest.astype(BF16).astype(F32)).astype(BF16))
    b_row_all = _dot_parts(_split3(lf_row), triu_ref[...])

    ti = lax.broadcasted_iota(jnp.int32, (L, L), 0)
    si = lax.broadcasted_iota(jnp.int32, (L, L), 1)
    causal = si <= ti

    for h in range(H_MLSTM):
        hs = slice(h * HEAD_DIM, (h + 1) * HEAD_DIM)
        rep = slice(h * LANES, (h + 1) * LANES)
        qh = q_all[:, hs]
        kh = k_all[:, hs]
        vh = v_all[:, hs]
        bc = b_rep[:, rep]
        igc = ig_rep[:, rep]
        br = b_row_all[h:h + 1, :]
        igr = gbt[h:h + 1, :]
        m_prev = m_scr[h:h + 1, :]
        c_prev = c_scr[h]
        n_prev = n_scr[h:h + 1, :]

        log_d = jnp.where(causal, bc - br + igr, NEG)
        m_inter = bc + m_prev
        m_t = jnp.maximum(m_inter, jnp.max(log_d, axis=-1, keepdims=True))
        qh_b = qh.astype(BF16)
        kh_b = kh.astype(BF16)
        sd = _dot_nt(qh_b, kh_b) * jnp.exp(log_d - m_t)
        scale_inter = jnp.exp(m_inter - m_t)
        num = (scale_inter[:, :HEAD_DIM] * _dot_nt(qh_b, c_prev.astype(BF16))
               + _dot(sd.astype(BF16), vh.astype(BF16)))
        den = scale_inter * jnp.sum(qh * n_prev, axis=-1, keepdims=True) + jnp.sum(sd, axis=-1, keepdims=True)
        hh = num / jnp.maximum(jnp.abs(den), jnp.exp(-m_t))[:, :HEAD_DIM]

        m_new = m_t[L - 1:L, :]
        b_last = bc[L - 1:L, :]
        decay = jnp.exp(b_last + m_prev - m_new)
        w_row = jnp.exp(b_last - br + igr - m_new)
        w_col = jnp.exp(b_last - bc + igc - m_new)
        vtw = (vt_all[hs, :] * w_row).astype(BF16)
        c_new = decay[:, :HEAD_DIM] * c_prev + _dot(vtw, kh_b)
        n_new = decay[:, :HEAD_DIM] * n_prev + jnp.sum(w_col[:, :HEAD_DIM] * kh, axis=0, keepdims=True)
        c_scr[h] = c_new
        n_scr[h:h + 1, :] = n_new
        m_scr[h:h + 1, :] = m_new
        c_out_ref[0, h] = c_new
        n_out_ref[0, h:h + 1, :] = n_new
        m_out_ref[0, h:h + 1, :] = m_new

        y = hh * lax.rsqrt(jnp.mean(hh * hh, axis=-1, keepdims=True) + EPS) * ng_ref[:, hs]
        mh_ref[0, :, hs] = _sigmoid(mo[:, hs]) * y


def _mlstm_prompt(mqk, mv, mo, gates, batch, seq, conv_w, conv_b, bg, ng, tril, triu, rep):
    L = MLSTM_CHUNK
    nchunk = seq // L
    chunk = lambda b, c: (b, c, 0)
    fixed = lambda b, c: (0, 0)
    return pl.pallas_call(
        _mlstm_prompt_kernel,
        grid=(batch, nchunk),
        in_specs=[
            pl.BlockSpec((1, L, 2 * W_MLSTM), chunk),
            pl.BlockSpec((1, L, W_MLSTM), chunk),
            pl.BlockSpec((1, L, W_MLSTM), chunk),
            pl.BlockSpec((1, L, LANES), chunk),
            pl.BlockSpec((CONV_W, 2 * W_MLSTM), fixed),
            pl.BlockSpec((1, 2 * W_MLSTM), fixed),
            pl.BlockSpec((1, LANES), fixed),
            pl.BlockSpec((1, W_MLSTM), fixed),
            pl.BlockSpec((L, L), fixed),
            pl.BlockSpec((L, L), fixed),
            pl.BlockSpec(rep.shape, fixed),
        ],
        out_specs=[
            pl.BlockSpec((1, L, W_MLSTM), chunk),
            pl.BlockSpec((1, H_MLSTM, HEAD_DIM, HEAD_DIM), lambda b, c: (b, 0, 0, 0)),
            pl.BlockSpec((1, H_MLSTM, HEAD_DIM), lambda b, c: (b, 0, 0)),
            pl.BlockSpec((1, H_MLSTM, LANES), lambda b, c: (b, 0, 0)),
        ],
        out_shape=[
            jax.ShapeDtypeStruct((batch, seq, W_MLSTM), F32),
            jax.ShapeDtypeStruct((batch, H_MLSTM, HEAD_DIM, HEAD_DIM), F32),
            jax.ShapeDtypeStruct((batch, H_MLSTM, HEAD_DIM), F32),
            jax.ShapeDtypeStruct((batch, H_MLSTM, LANES), F32),
        ],
        scratch_shapes=[
            pltpu.VMEM((L + SUBLANES, 2 * W_MLSTM), F32),
            pltpu.VMEM((H_MLSTM, HEAD_DIM, HEAD_DIM), F32),
            pltpu.VMEM((H_MLSTM, HEAD_DIM), F32),
            pltpu.VMEM((H_MLSTM, LANES), F32),
        ],
        compiler_params=_cparams("parallel", "arbitrary"),
        name="mlstm_prompt",
    )(mqk.reshape(batch, seq, -1), mv.reshape(batch, seq, -1), mo.reshape(batch, seq, -1),
      gates.reshape(batch, seq, -1), conv_w, conv_b, bg, ng, tril, triu, rep)


def _mlstm_sample_pre_kernel(mqk_ref, buf_ref, gates_ref, cw_ref, cb_ref, bg_ref, qk_ref, g_ref):
    conv = cb_ref[...] + cw_ref[CONV_W - 1:CONV_W, :] * mqk_ref[...]
    for j in range(CONV_W - 1):
        conv = conv + cw_ref[j:j + 1, :] * buf_ref[j]
    qk = conv * _sigmoid(conv)
    lane = lax.broadcasted_iota(jnp.int32, qk.shape, 1)
    qk_ref[...] = jnp.where(lane < W_MLSTM, qk, qk * (HEAD_DIM ** -0.5))
    gb = gates_ref[...] + bg_ref[...]
    glane = lax.broadcasted_iota(jnp.int32, gb.shape, 1)
    g_ref[...] = jnp.where(glane < H_MLSTM, gb, _log_sigmoid(gb))


def _mlstm_sample_step_kernel(q_ref, k_ref, v_ref, mo_ref, ig_ref, lf_ref, c0_ref, n0_ref, m0_ref, ng_ref,
                              mh_ref, c_ref, n_ref, m_ref):
    for h in range(H_MLSTM):
        q = q_ref[0, h]
        k = k_ref[0, h]
        v = v_ref[0, h]
        ig = ig_ref[0, h]
        lf = lf_ref[0, h]
        c0 = c0_ref[0, h]
        n0 = n0_ref[0, h]
        m0 = m0_ref[0, h]
        m_inter = lf + m0
        m_t = jnp.maximum(m_inter, ig)
        w_in = jnp.exp(ig - m_t)
        sd = jnp.sum(q * k, axis=-1, keepdims=True) * w_in
        scale_inter = jnp.exp(m_inter - m_t)
        num = scale_inter * jnp.sum(c0 * q, axis=-1, keepdims=True) + sd * v
        den = scale_inter * jnp.sum(n0 * q, axis=-1, keepdims=True) + sd
        hh = num / jnp.maximum(jnp.abs(den), jnp.exp(-m_t))
        decay = jnp.exp(lf + m0 - m_t)
        c_ref[0, h] = decay * c0 + w_in * (v * k)
        n_ref[0, h] = decay * n0 + w_in * k
        m_ref[0, h] = m_t
        y = hh * lax.rsqrt(jnp.mean(hh * hh, axis=0, keepdims=True) + EPS) * ng_ref[h]
        mh_ref[0, h] = _sigmoid(mo_ref[0, h]) * y


def _mlstm_sample(mqk, mv, mo, gates, conv_buf, c0, n0, m0, conv_w, conv_b, bg, ng):
    nb = mqk.shape[0]
    full = lambda *shape: pl.BlockSpec(shape, lambda: (0,) * len(shape))
    qk, g = pl.pallas_call(
        _mlstm_sample_pre_kernel,
        in_specs=[full(nb, 2 * W_MLSTM), full(CONV_W - 1, nb, 2 * W_MLSTM), full(nb, LANES),
                  full(CONV_W, 2 * W_MLSTM), full(1, 2 * W_MLSTM), full(1, LANES)],
        out_specs=[full(nb, 2 * W_MLSTM), full(nb, LANES)],
        out_shape=[jax.ShapeDtypeStruct((nb, 2 * W_MLSTM), F32), jax.ShapeDtypeStruct((nb, LANES), F32)],
        name="mlstm_sample_pre",
    )(mqk, jnp.swapaxes(conv_buf, 0, 1), gates, conv_w, conv_b, bg)
    row = (nb, H_MLSTM, 1, HEAD_DIM)
    col = (nb, H_MLSTM, HEAD_DIM, 1)
    one = (nb, H_MLSTM, 1, 1)
    mat = (nb, H_MLSTM, HEAD_DIM, HEAD_DIM)
    spec = lambda shape: pl.BlockSpec((1,) + shape[1:], lambda b: (b, 0, 0, 0))
    ins = [
        (qk[:, :W_MLSTM].reshape(row), row), (qk[:, W_MLSTM:].reshape(row), row), (mv.reshape(col), col),
        (mo.reshape(col), col), (g[:, :H_MLSTM].reshape(one), one), (g[:, H_MLSTM:2 * H_MLSTM].reshape(one), one),
        (c0, mat), (n0.reshape(row), row), (m0.reshape(one), one),
    ]
    mh, c, n, m = pl.pallas_call(
        _mlstm_sample_step_kernel,
        grid=(nb,),
        in_specs=[spec(s) for _, s in ins] + [pl.BlockSpec((H_MLSTM, HEAD_DIM, 1), lambda b: (0, 0, 0))],
        out_specs=[spec(col), spec(mat), spec(row), spec(one)],
        out_shape=[jax.ShapeDtypeStruct(s, F32) for s in (col, mat, row, one)],
        compiler_params=_cparams("parallel"),
        name="mlstm_sample_step",
    )(*[a for a, _ in ins], ng.reshape(H_MLSTM, HEAD_DIM, 1))
    return mh.reshape(nb, W_MLSTM), c, n.reshape(nb, H_MLSTM, HEAD_DIM), m.reshape(nb, H_MLSTM)


def _outproj_kernel(x_ref, att_ref, mh_ref, w_ref, g_ref, x1_ref, hn_ref):
    x1 = (x_ref[...] + _dot(att_ref[...].astype(BF16), w_ref[0:W_ATT, :])
          + _dot(mh_ref[...].astype(BF16), w_ref[W_ATT:, :]))
    x1_ref[...] = x1
    ms = jnp.mean(x1 * x1, axis=-1, keepdims=True)
    hn_ref[...] = (x1 * lax.rsqrt(ms + EPS) * g_ref[...]).astype(BF16)


def _outproj_into_kernel(x1_any, hn_any, *refs):
    del x1_any, hn_any
    _outproj_kernel(*refs)


def _out_projection(x2d, att, mh, w_out, g2, tm, n_out, out_rows, into=None):
    n_in = x2d.shape[0]
    in_blocks = n_in // tm
    first = out_rows[0] // tm
    row = lambda i: (i % in_blocks, 0)
    out_row = lambda i: (first + i, 0)
    fixed = lambda i: (0, 0)
    half = pl.BlockSpec((tm, W_ATT), row)
    in_specs = [pl.BlockSpec((tm, D_MODEL), row), half, half,
                pl.BlockSpec((D_MODEL, D_MODEL), fixed), pl.BlockSpec((1, D_MODEL), fixed)]
    args = (x2d, att, mh, w_out, g2)
    body = _outproj_kernel
    aliases = {}
    if into is not None:
        in_specs = [pl.BlockSpec(memory_space=pl.ANY)] * 2 + in_specs
        args = (*into, *args)
        body = _outproj_into_kernel
        aliases = {0: 0, 1: 1}
    return pl.pallas_call(
        body,
        grid=((out_rows[1] - out_rows[0]) // tm,),
        in_specs=in_specs,
        out_specs=[pl.BlockSpec((tm, D_MODEL), out_row), pl.BlockSpec((tm, D_MODEL), out_row)],
        out_shape=[jax.ShapeDtypeStruct((n_out, D_MODEL), F32), jax.ShapeDtypeStruct((n_out, D_MODEL), BF16)],
        input_output_aliases=aliases,
        compiler_params=_cparams("parallel"),
        name="out_projection",
    )(*args)


def _peer_scores_kernel(hn_ref, wq_ref, g_ref, sk_ref, s_ref):
    q = _dot(hn_ref[...], wq_ref[...])
    half = PEER_DKEY // 2
    for h in range(PEER_HEADS):
        qh = q[:, h * PEER_DKEY:(h + 1) * PEER_DKEY]
        qn = qh * lax.rsqrt(jnp.mean(qh * qh, axis=-1, keepdims=True) + EPS) * g_ref[...]
        for part in range(2):
            s = _dot_nt(qn[:, part * half:(part + 1) * half].astype(BF16), sk_ref[2 * h + part])
            r0 = (2 * h + part) * PEER_NKEYS
            s_ref[r0:r0 + PEER_NKEYS, :] = s.T


def _peer_scores(hn, wq, g, sk, tm):
    n = hn.shape[0]
    rows = PEER_HEADS * 2 * PEER_NKEYS
    return pl.pallas_call(
        _peer_scores_kernel,
        grid=(n // tm,),
        in_specs=[pl.BlockSpec((tm, D_MODEL), lambda i: (i, 0)),
                  pl.BlockSpec(wq.shape, lambda i: (0, 0)),
                  pl.BlockSpec((1, PEER_DKEY), lambda i: (0, 0)),
                  pl.BlockSpec(sk.shape, lambda i: (0, 0, 0))],
        out_specs=pl.BlockSpec((rows, tm), lambda i: (0, i)),
        out_shape=jax.ShapeDtypeStruct((rows, n), F32),
        compiler_params=_cparams("parallel"),
        name="peer_scores",
    )(hn, wq, g, sk)


_STAIR = [(a, b) for a in range(PEER_TOPK) for b in range(PEER_TOPK) if (a + 1) * (b + 1) <= PEER_TOPK]


def _peer_select_kernel(s1_ref, s2_ref, rank2_ref, cnt1_ref, e1_ref, e2_ref,
                        work1_ref, work2_ref, rank1_ref, rank2s_ref, vals1_ref, vals2_ref,
                        vals1t_ref, vals2t_ref, cntt_ref, cnt_ref, zt_ref, zrow_ref):
    shape = rank1_ref.shape
    keyf = lax.broadcasted_iota(jnp.int32, shape, 0).astype(F32)
    chunks = [slice(c * LANES, (c + 1) * LANES) for c in range(SUBLANES)]
    halves = ((s1_ref, work1_ref, rank1_ref, vals1_ref), (s2_ref, work2_ref, rank2s_ref, vals2_ref))

    def top16_distinct():
        for _, _, rank_ref, _ in halves:
            rank_ref[...] = jnp.full(shape, float(PEER_TOPK), F32)

        def one_round(a, previous):
            maxima = []
            for (src_ref, _, rank_ref, vals_ref), prev in zip(halves, previous):
                w = src_ref[...]
                rank_ref[...] = jnp.where(w == prev, jnp.asarray(a - 1, F32), rank_ref[...])
                mx = jnp.max(jnp.where(w < prev, w, -jnp.inf), axis=0, keepdims=True)
                vals_ref[pl.ds(a, 1), :] = mx
                maxima.append(mx)
            return tuple(maxima)

        top = jnp.full((1, shape[1]), jnp.inf, F32)
        last = lax.fori_loop(0, PEER_TOPK, one_round, (top, top))
        for (src_ref, _, rank_ref, _), prev in zip(halves, last):
            rank_ref[...] = jnp.where(src_ref[...] == prev, float(PEER_TOPK - 1), rank_ref[...])

    def top16_ties():
        for src_ref, work_ref, rank_ref, _ in halves:
            work_ref[...] = src_ref[...]
            rank_ref[...] = jnp.full(shape, float(PEER_TOPK), F32)

        def one_round(a, carry):
            for _, work_ref, rank_ref, vals_ref in halves:
                w = work_ref[...]
                mx = jnp.max(w, axis=0, keepdims=True)
                sel = keyf == jnp.min(jnp.where(w == mx, keyf, float(PEER_NKEYS)), axis=0, keepdims=True)
                work_ref[...] = jnp.where(sel, -jnp.inf, w)
                rank_ref[...] = jnp.where(sel, jnp.asarray(a, F32), rank_ref[...])
                vals_ref[pl.ds(a, 1), :] = mx
            return carry

        lax.fori_loop(0, PEER_TOPK, one_round, 0)

    top16_distinct()
    taken = [jnp.sum(jnp.where(rank_ref[...] < float(PEER_TOPK), 1.0, 0.0), axis=0, keepdims=True)
             for _, _, rank_ref, _ in halves]
    ties = jnp.max(jnp.maximum(jnp.abs(taken[0] - PEER_TOPK), jnp.abs(taken[1] - PEER_TOPK))) > 0.0

    @pl.when(ties)
    def _():
        top16_ties()

    for c, cs in enumerate(chunks):
        vals1t_ref[:, c, :] = vals1_ref[:, cs]
        vals2t_ref[:, c, :] = vals2_ref[:, cs]
    v1 = [vals1t_ref[a] for a in range(PEER_TOPK)]
    v2 = [vals2t_ref[b] for b in range(PEER_TOPK)]
    cand = [v1[a] + v2[b] for a, b in _STAIR]
    pos = []
    for i, (a, b) in enumerate(_STAIR):
        static = sum(1 for (a2, b2) in _STAIR if a2 <= a and b2 <= b and (a2, b2) != (a, b))
        pos.append(jnp.full(cand[0].shape, float(static), F32))
    for i, (ai, bi) in enumerate(_STAIR):
        for j in range(i + 1, len(_STAIR)):
            aj, bj = _STAIR[j]
            if ai <= aj and bi <= bj:
                continue
            i_first = cand[i] >= cand[j]
            pos[j] = pos[j] + jnp.where(i_first, 1.0, 0.0)
            pos[i] = pos[i] + jnp.where(i_first, 0.0, 1.0)
    e1s = [jnp.exp(v1[a] - v1[0]) for a in range(PEER_TOPK)]
    e2s = [jnp.exp(v2[b] - v2[0]) for b in range(PEER_TOPK)]
    z = jnp.zeros_like(cand[0])
    for a in range(PEER_TOPK):
        cnt_a = jnp.zeros_like(z)
        za = jnp.zeros_like(z)
        for i, (a2, b) in enumerate(_STAIR):
            if a2 != a:
                continue
            chosen = pos[i] < float(PEER_TOPK)
            cnt_a = cnt_a + jnp.where(chosen, 1.0, 0.0)
            za = za + jnp.where(chosen, e2s[b], 0.0)
        cntt_ref[a] = cnt_a
        z = z + e1s[a] * za
    zt_ref[0] = 1.0 / z
    for c, cs in enumerate(chunks):
        cnt_ref[:, cs] = cntt_ref[:, c, :]
        zrow_ref[:, cs] = zt_ref[:, c, :]

    cnt1_ref[...] = jnp.zeros(shape, F32)

    def spread(a, carry):
        cnt1_ref[...] = jnp.where(rank1_ref[...] == jnp.asarray(a, F32), cnt_ref[pl.ds(a, 1), :], cnt1_ref[...])
        return carry

    lax.fori_loop(0, PEER_TOPK, spread, 0)
    rank2_ref[...] = rank2s_ref[...].astype(BF16)
    e1_ref[...] = jnp.exp(s1_ref[...] - vals1_ref[0:1, :]) * zrow_ref[...]
    e2_ref[...] = jnp.exp(s2_ref[...] - vals2_ref[0:1, :]).astype(BF16)


def _peer_select(scores_t):
    n = scores_t.shape[1]
    tok = LANES * SUBLANES
    blk = (PEER_NKEYS, tok)
    out_rows = PEER_HEADS * PEER_NKEYS
    vm = lambda *shape: pltpu.VMEM(shape, F32)
    return pl.pallas_call(
        _peer_select_kernel,
        grid=(PEER_HEADS, n // tok),
        in_specs=[pl.BlockSpec(blk, lambda h, g: (2 * h, g)), pl.BlockSpec(blk, lambda h, g: (2 * h + 1, g))],
        out_specs=[pl.BlockSpec(blk, lambda h, g: (h, g))] * 4,
        out_shape=[jax.ShapeDtypeStruct((out_rows, n), dt) for dt in (BF16, F32, F32, BF16)],
        scratch_shapes=[vm(*blk)] * 4 + [vm(PEER_TOPK, tok)] * 2 + [vm(PEER_TOPK, SUBLANES, LANES)] * 3
                       + [vm(PEER_TOPK, tok), vm(1, SUBLANES, LANES), vm(1, tok)],
        compiler_params=_cparams("parallel", "parallel"),
        name="peer_select",
    )(scores_t, scores_t)


PEER_TOK_TILE = 512
PEER_I1_PER_STEP = 8
PEER_MXU_CHUNKS = 8
BF16_ROWS = 16


def _peer_dense_kernel(nblk, main_tiles, hn_ref, u_ref, vt_prev_ref, vt_last_ref, rank2_ref, e2_ref, cnt1_ref, e1_ref,
                       x1_ref, y_ref, y_tail_ref, ht_ref, p_even_ref, p_odd_ref, acc_ref):
    j = pl.program_id(1)
    tokens = hn_ref.shape[0]
    reps = PEER_NKEYS // BF16_ROWS
    assert PEER_I1_PER_STEP == SUBLANES

    def row_bf16(tile, s):
        x8 = jnp.broadcast_to(tile[s:s + 1, :], (SUBLANES, tokens))
        x16 = jnp.concatenate([x8, x8], axis=0).astype(BF16)
        return jnp.concatenate([x16] * reps, axis=0)

    def evaluate(p_write, p_read):
        first = pl.multiple_of(j * PEER_I1_PER_STEP, SUBLANES)
        cnt_tiles = [cnt1_ref[pl.ds(h * PEER_NKEYS + first, SUBLANES), :] for h in range(PEER_HEADS)]
        e1_tiles = [e1_ref[pl.ds(h * PEER_NKEYS + first, SUBLANES), :] for h in range(PEER_HEADS)]
        per_chunk = PEER_I1_PER_STEP // PEER_MXU_CHUNKS
        crow = per_chunk * PEER_NKEYS
        for c in range(PEER_MXU_CHUNKS):
            rc = slice(c * crow, (c + 1) * crow)
            a_c = _dot(u_ref[rc, :], ht_ref[...])
            if p_read is not None:
                acc_ref[rc, :] += _dot(vt_prev_ref[rc, :], p_read[...])
            for sc in range(per_chunk):
                s = c * per_chunk + sc
                g = None
                for h in range(PEER_HEADS):
                    rows = slice(h * PEER_NKEYS, (h + 1) * PEER_NKEYS)
                    w = jnp.where(rank2_ref[rows, :] < row_bf16(cnt_tiles[h], s),
                                  e2_ref[rows, :] * row_bf16(e1_tiles[h], s), jnp.zeros((), BF16))
                    g = w if g is None else g + w
                a = a_c[sc * PEER_NKEYS:(sc + 1) * PEER_NKEYS, :]
                act = a + a * lax.erf(a * (2.0 ** -0.5))
                p_write[s * PEER_NKEYS:(s + 1) * PEER_NKEYS, :] = g * act.astype(BF16)

    @pl.when(j == 0)
    def _():
        acc_ref[...] = jnp.zeros_like(acc_ref)
        ht_ref[...] = hn_ref[...].astype(F32).T.astype(BF16)
        evaluate(p_even_ref, None)

    @pl.when(j % 2 == 1)
    def _():
        evaluate(p_odd_ref, p_even_ref)

    @pl.when(jnp.logical_and(j % 2 == 0, j > 0))
    def _():
        evaluate(p_even_ref, p_odd_ref)

    @pl.when(j == nblk - 1)
    def _():
        p_last = p_odd_ref if nblk % 2 == 0 else p_even_ref
        acc = acc_ref[...] + _dot(vt_last_ref[...], p_last[...])
        y = x1_ref[...] + acc.T
        in_main = pl.program_id(0) < main_tiles

        @pl.when(in_main)
        def _():
            y_ref[...] = y

        @pl.when(jnp.logical_not(in_main))
        def _():
            y_tail_ref[...] = y


def _peer_dense(hn, u, vt, rank2, e2, cnt1, e1, x1, n_main):
    T = PEER_TOK_TILE
    eb = PEER_I1_PER_STEP * PEER_NKEYS
    nexp = u.shape[0]
    sel_rows = PEER_HEADS * PEER_NKEYS
    nblk = nexp // eb
    main_tiles = n_main // T
    tok = lambda t, j: (0, t)
    return pl.pallas_call(
        functools.partial(_peer_dense_kernel, nblk, main_tiles),
        grid=(main_tiles + 1, nblk),
        in_specs=[
            pl.BlockSpec((T, D_MODEL), lambda t, j: (t, 0)),
            pl.BlockSpec((eb, D_MODEL), lambda t, j: (j, 0)),
            pl.BlockSpec((D_MODEL, eb), lambda t, j: (0, jnp.maximum(j - 1, 0))),
            pl.BlockSpec((D_MODEL, eb), lambda t, j: (0, nblk - 1)),
            pl.BlockSpec((sel_rows, T), tok),
            pl.BlockSpec((sel_rows, T), tok),
            pl.BlockSpec((sel_rows, T), tok),
            pl.BlockSpec((sel_rows, T), tok),
            pl.BlockSpec((T, D_MODEL), lambda t, j: (t, 0)),
        ],
        out_specs=[pl.BlockSpec((T, D_MODEL), lambda t, j: (jnp.minimum(t, main_tiles - 1), 0)),
                   pl.BlockSpec((T, D_MODEL), lambda t, j: (0, 0))],
        out_shape=[jax.ShapeDtypeStruct((n_main, D_MODEL), F32), jax.ShapeDtypeStruct((T, D_MODEL), F32)],
        scratch_shapes=[pltpu.VMEM((D_MODEL, T), BF16), pltpu.VMEM((eb, T), BF16), pltpu.VMEM((eb, T), BF16),
                        pltpu.VMEM((D_MODEL, T), F32)],
        compiler_params=_cparams("arbitrary", "arbitrary"),
        name="peer_dense",
    )(hn, u, vt, vt, rank2, e2, cnt1, e1, x1)


def _tri_constants():
    L = MLSTM_CHUNK
    tril = np.tril(np.ones((L, L), np.float32))
    rep = (np.arange(LANES)[:, None] == np.arange(2 * H_MLSTM * LANES)[None, :] // LANES).astype(np.float32)
    return jnp.asarray(tril, BF16), jnp.asarray(tril.T, BF16), jnp.asarray(rep, BF16)


def _head_constants():
    bd = np.kron(np.eye(H_ATT, dtype=np.float32), np.full((HEAD_DIM, HEAD_DIM), 1.0 / HEAD_DIM, np.float32))
    return jnp.asarray(bd, BF16)


def kernel(x_prompt, x_sample, cache_attn_k, cache_attn_v, state_mlstm_C, state_mlstm_n, state_mlstm_m,
           state_mlstm_conv, norm1_g, w_in, att_qnorm_g, att_knorm_g, b_gates, mlstm_conv_w, mlstm_conv_b,
           mlstm_norm_g, w_out, norm2_g, peer_w_query, peer_qnorm_g, peer_subkeys, peer_u, peer_v):
    batch, seq = x_prompt.shape[:2]
    nsamp = x_sample.shape[0]
    past_len = 16384
    assert norm1_g.shape[0] == 1 and x_sample.shape[1] == 1 and seq % MLSTM_CHUNK == 0
    wb = cache_attn_k.shape[2]
    li = 0
    bd = _head_constants()
    tril, triu, gate_rep = _tri_constants()

    w = w_in[li]
    gate_lo, gate_hi = 3072, 3072 + 2 * H_MLSTM
    w_main = jnp.concatenate([w[:, :gate_lo], w[:, gate_hi:]], axis=1).astype(BF16)
    wg = jnp.pad(w[:, gate_lo:gate_hi], ((0, 0), (0, LANES - 2 * H_MLSTM)))
    wg_hi = wg.astype(BF16)
    wg_lo = (wg - wg_hi.astype(F32)).astype(BF16)
    g1 = norm1_g[li][None, :]
    qg = jnp.tile(att_qnorm_g[li], H_ATT)[None, :]
    kg = jnp.tile(att_knorm_g[li], H_ATT)[None, :]
    bg = jnp.pad(b_gates[li], (0, LANES - 2 * H_MLSTM))[None, :]
    conv_w = mlstm_conv_w[li]
    conv_b = mlstm_conv_b[li][None, :]
    ng = mlstm_norm_g[li][None, :]
    wo = w_out[li].astype(BF16)
    g2 = norm2_g[li][None, :]
    wq = peer_w_query[li].astype(BF16)
    pqg = peer_qnorm_g[li][None, :]
    sk = peer_subkeys[li].reshape(PEER_HEADS * 2, PEER_NKEYS, PEER_DKEY // 2).astype(BF16)
    u_b = peer_u[li].astype(BF16)
    vt_b = (0.5 * peer_v[li]).astype(BF16).T

    n_p = batch * seq
    xp2 = x_prompt.reshape(n_p, D_MODEL)
    pos_p = jnp.arange(seq, dtype=jnp.int32)
    q, k, v, mqk, mv, mo, gates, kt, vt = _in_projection(xp2, pos_p, 512, g1, w_main, wg_hi, wg_lo, qg, kg, bd,
                                                         transposed_kv=True)
    att_p = _attn_prompt(q, k, v, batch, seq)
    mh_p, c_p, n_p_state, m_p = _mlstm_prompt(mqk, mv, mo, gates, batch, seq, conv_w, conv_b, bg, ng, tril, triu,
                                                  gate_rep)
    n_all = n_p + nsamp
    group = LANES * SUBLANES
    n_pad = -(-n_all // group) * group
    x1_all, hn_all = _out_projection(xp2, att_p, mh_p.reshape(n_p, W_MLSTM), wo, g2, 256, n_pad, (0, n_pad))
    wbp = min(wb, seq)
    new_k_prompt = jnp.transpose(kt, (0, 3, 1, 2))[None, :, seq - wbp:]
    new_v_prompt = jnp.transpose(vt, (0, 3, 1, 2))[None, :, seq - wbp:]
    new_conv_prompt = mqk.reshape(batch, seq, -1)[None, :, seq - (CONV_W - 1):]

    xs2 = x_sample.reshape(nsamp, D_MODEL)
    pos_s = jnp.full((nsamp,), past_len, dtype=jnp.int32)
    qs, ks, vs, mqk_s, mv_s, mo_s, gates_s = _in_projection(xs2, pos_s, nsamp, g1, w_main, wg_hi, wg_lo, qg, kg, bd)
    cache_kt = jnp.transpose(cache_attn_k[li], (0, 2, 3, 1))
    cache_vt = jnp.transpose(cache_attn_v[li], (0, 2, 3, 1))
    att_s = _attn_sample(qs, ks, vs, cache_kt, cache_vt)
    mh_s, c_s, n_s, m_s = _mlstm_sample(mqk_s, mv_s, mo_s, gates_s, state_mlstm_conv[li], state_mlstm_C[li],
                                        state_mlstm_n[li], state_mlstm_m[li], conv_w, conv_b, bg, ng)
    x1_all, hn_all = _out_projection(xs2, att_s, mh_s, wo, g2, nsamp, n_pad, (n_p, n_all),
                                     into=(x1_all, hn_all))
    new_conv_sample = jnp.concatenate([state_mlstm_conv[li][:, 1:], mqk_s[:, None, :]], axis=1)[None]

    scores_t = _peer_scores(hn_all, wq, pqg, sk, 256)
    rank2, cnt1, e1, e2 = _peer_select(scores_t)
    assert n_p % PEER_TOK_TILE == 0 and nsamp <= PEER_TOK_TILE
    y_main, y_tail = _peer_dense(hn_all, u_b, vt_b, rank2, e2, cnt1, e1, x1_all, n_p)
    y_prompt = y_main.reshape(batch, seq, D_MODEL)
    y_sample = y_tail[:nsamp].reshape(nsamp, 1, D_MODEL)

    return (y_prompt, y_sample, new_k_prompt, new_v_prompt,
            ks.reshape(1, nsamp, 1, H_ATT, HEAD_DIM), vs.reshape(1, nsamp, 1, H_ATT, HEAD_DIM),
            c_p[None], n_p_state[None], m_p[None, :, :, 0], new_conv_prompt,
            c_s[None], n_s[None], m_s[None], new_conv_sample)
```

```python
import functools

import numpy as np
import jax
import jax.numpy as jnp
from jax import lax
from jax.experimental import pallas as pl
from jax.experimental.pallas import tpu as pltpu

F32 = jnp.float32
BF16 = jnp.bfloat16

D_MODEL = 1024
HEAD_DIM = 64
W_ATT = 512
W_MLSTM = 512
H_ATT = 8
H_MLSTM = 8
DILATIONS = ((128, 1), (512, 4), (2048, 16))
ATT_BLOCK = 128
ROPE_THETA = 10000.0
CONV_W = 4
PEER_HEADS = 8
PEER_NKEYS = 128
PEER_DKEY = 256
PEER_TOPK = 16
EPS = 1e-6
NEG = -1e30

LANES = 128
SUBLANES = 8
VMEM_LIMIT = 56 * 1024 * 1024


def _cparams(*sem):
    return pltpu.CompilerParams(dimension_semantics=sem, vmem_limit_bytes=VMEM_LIMIT)


def _split2(x):
    hi = x.astype(BF16)
    lo = (x - hi.astype(F32)).astype(BF16)
    return hi, lo


def _split3(x):
    hi = x.astype(BF16)
    r = x - hi.astype(F32)
    mid = r.astype(BF16)
    lo = (r - mid.astype(F32)).astype(BF16)
    return hi, mid, lo


def _dot(a, b):
    return jnp.dot(a, b, preferred_element_type=F32)


def _dot_nt(a, b):
    return lax.dot_general(a, b, (((1,), (1,)), ((), ())), preferred_element_type=F32)


def _dot_parts(parts, b):
    acc = _dot(parts[0], b)
    for p in parts[1:]:
        acc = acc + _dot(p, b)
    return acc


def _sigmoid(x):
    return 1.0 / (1.0 + jnp.exp(-x))


def _log_sigmoid(x):
    return jnp.minimum(x, 0.0) - jnp.log1p(jnp.exp(-jnp.abs(x)))


def _inproj_kernel(x_ref, g_ref, w_ref, wgh_ref, wgl_ref, qg_ref, kg_ref, cos_ref, sin_ref, bd_ref,
                   q_ref, k_ref, v_ref, mqk_ref, mv_ref, mo_ref, gates_ref, kt_ref=None, vt_ref=None):
    x = x_ref[...]
    ms = jnp.mean(x * x, axis=-1, keepdims=True)
    xn = x * lax.rsqrt(ms + EPS) * g_ref[...]
    xh, xl = _split2(xn)

    def seg(lo, hi):
        return _dot(xh, w_ref[:, lo:hi])

    bd = bd_ref[...]
    cos = cos_ref[...]
    sin = sin_ref[...]
    lane = lax.broadcasted_iota(jnp.int32, cos.shape, 1)
    first_half = (lane % HEAD_DIM) < (HEAD_DIM // 2)

    def head_norm_rope(a, g):
        sq = a * a
        hi, lo = _split2(sq)
        msq = _dot(hi, bd) + _dot(lo, bd)
        y = a * lax.rsqrt(msq + EPS) * g
        rot = jnp.where(first_half, pltpu.roll(y, W_ATT - HEAD_DIM // 2, 1), pltpu.roll(y, HEAD_DIM // 2, 1))
        return y * cos + rot * sin

    q_ref[...] = head_norm_rope(seg(0, 512), qg_ref[...]) * (HEAD_DIM ** -0.5)
    k = head_norm_rope(seg(512, 1024), kg_ref[...])
    v = seg(1024, 1536)
    k_ref[...] = k
    v_ref[...] = v
    if kt_ref is not None:
        kt_ref[0] = k.T.reshape(H_ATT, HEAD_DIM, k.shape[0])
        vt_ref[0] = v.T.reshape(H_ATT, HEAD_DIM, v.shape[0])
    mqk_ref[...] = seg(1536, 2560)
    mv_ref[...] = seg(2560, 3072)
    mo_ref[...] = seg(3072, 3584)
    gates_ref[...] = _dot(xh, wgh_ref[...]) + _dot(xl, wgh_ref[...]) + _dot(xh, wgl_ref[...])


def _rope_tables(pos):
    half = HEAD_DIM // 2
    inv = ROPE_THETA ** (-jnp.arange(half, dtype=F32) / half)
    ang = pos.astype(F32)[:, None] * inv[None, :]
    cos = jnp.cos(ang)
    sin = jnp.sin(ang)
    cos_h = jnp.concatenate([cos, cos], axis=-1)
    sin_h = jnp.concatenate([-sin, sin], axis=-1)
    return jnp.tile(cos_h, (1, H_ATT)), jnp.tile(sin_h, (1, H_ATT))


def _in_projection(x2d, pos, tm, norm_g, w_main, wg_hi, wg_lo, qg, kg, bd, transposed_kv=False):
    n = x2d.shape[0]
    cos, sin = _rope_tables(pos)
    pblocks = pos.shape[0] // tm
    row = lambda i: (i, 0)
    fixed = lambda i: (0, 0)
    tab = lambda i: (i % pblocks, 0)
    widths = (512, 512, 512, 1024, 512, 512, LANES)
    out_specs = [pl.BlockSpec((tm, w), row) for w in widths]
    out_shape = [jax.ShapeDtypeStruct((n, w), F32) for w in widths]
    if transposed_kv:
        tshape = (n // pos.shape[0], H_ATT, HEAD_DIM, pos.shape[0])
        out_specs += [pl.BlockSpec((1, H_ATT, HEAD_DIM, tm), lambda i: (i // pblocks, 0, 0, i % pblocks))] * 2
        out_shape += [jax.ShapeDtypeStruct(tshape, F32)] * 2
    return pl.pallas_call(
        _inproj_kernel,
        grid=(n // tm,),
        in_specs=[
            pl.BlockSpec((tm, D_MODEL), row),
            pl.BlockSpec((1, D_MODEL), fixed),
            pl.BlockSpec(w_main.shape, fixed),
            pl.BlockSpec(wg_hi.shape, fixed),
            pl.BlockSpec(wg_lo.shape, fixed),
            pl.BlockSpec((1, W_ATT), fixed),
            pl.BlockSpec((1, W_ATT), fixed),
            pl.BlockSpec((tm, W_ATT), tab),
            pl.BlockSpec((tm, W_ATT), tab),
            pl.BlockSpec((W_ATT, W_ATT), fixed),
        ],
        out_specs=out_specs,
        out_shape=out_shape,
        compiler_params=_cparams("parallel"),
        name="in_projection",
    )(x2d, norm_g, w_main, wg_hi, wg_lo, qg, kg, cos, sin, bd)


def _merge_branches(outs, lses):
    m = functools.reduce(jnp.maximum, lses)
    es = [jnp.exp(l - m) for l in lses]
    return sum(e * o for e, o in zip(es, outs)) / sum(es)


ATT_GROUP = 8


def _attn_prompt_kernel(q_ref, k_ref, v_ref, att_ref, o_scr, l_scr):
    seq = q_ref.shape[1]
    blk = ATT_BLOCK
    lane = lax.broadcasted_iota(jnp.int32, (1, LANES), 1)
    ii = lax.broadcasted_iota(jnp.int32, (2 * blk, 2 * blk), 0) % blk
    jj = lax.broadcasted_iota(jnp.int32, (2 * blk, 2 * blk), 1)
    mask_prev = jnp.minimum(jj - ii, ii + blk - jj) >= 0
    mask_first = (lax.broadcasted_iota(jnp.int32, (2 * blk, blk), 1)
                  <= lax.broadcasted_iota(jnp.int32, (2 * blk, blk), 0) % blk)

    first_head = (lane // HEAD_DIM) == 0

    def attend(q, k2, v2, mask):
        q2 = jnp.concatenate([jnp.where(first_head, q, 0.0), jnp.where(first_head, 0.0, q)], axis=0).astype(BF16)
        s = jnp.where(mask, _dot_nt(q2, k2), NEG)
        m = jnp.max(s, axis=-1, keepdims=True)
        p = jnp.exp(s - m)
        l = jnp.sum(p, axis=-1, keepdims=True)
        o = _dot((p / l).astype(BF16), v2)
        lse = jnp.broadcast_to(m + jnp.log(l), o.shape)
        return jnp.where(first_head, o[:blk], o[blk:]), jnp.where(first_head, lse[:blk], lse[blk:])

    def group(branch, dil, starts, chained, first_has_prev):
        def rows(at):
            return pl.ds(at, blk, stride=dil) if dil > 1 else pl.ds(at, blk)

        qs = [q_ref[0, rows(at), :] for at in starts]
        ks = [k_ref[0, rows(at), :].astype(BF16) for at in starts]
        vs = [v_ref[0, rows(at), :].astype(BF16) for at in starts]
        k_before = v_before = None
        if chained and first_has_prev:
            before = starts[0] - blk * dil
            k_before = k_ref[0, rows(before), :].astype(BF16)
            v_before = v_ref[0, rows(before), :].astype(BF16)
        results = []
        for i in range(len(starts)):
            kp, vp = (k_before, v_before) if i == 0 else (ks[i - 1], vs[i - 1])
            if chained and kp is not None:
                results.append(attend(qs[i], jnp.concatenate([kp, ks[i]], axis=0),
                                      jnp.concatenate([vp, vs[i]], axis=0), mask_prev))
            else:
                results.append(attend(qs[i], ks[i], vs[i], mask_first))
        for at, (o_pair, l_pair) in zip(starts, results):
            o_scr[branch, rows(at), :] = o_pair
            l_scr[branch, rows(at), :] = l_pair

    for branch, (_, dil) in enumerate(DILATIONS):
        nblocks = seq // dil // blk
        step = blk * dil
        if nblocks == 1:
            def classes(g, carry, branch=branch, dil=dil):
                group(branch, dil, [g * ATT_GROUP + i for i in range(ATT_GROUP)], False, False)
                return carry

            lax.fori_loop(0, dil // ATT_GROUP, classes, 0)
        else:
            per_group = min(ATT_GROUP, nblocks)
            assert nblocks % per_group == 0

            def residue(r, carry, branch=branch, dil=dil, nblocks=nblocks, step=step, per_group=per_group):
                group(branch, dil, [r + i * step for i in range(per_group)], True, False)

                def later(g, c):
                    base = g * (per_group * step)
                    base = pl.multiple_of(base, blk) if dil == 1 else r + base
                    group(branch, dil, [base + i * step for i in range(per_group)], True, True)
                    return c

                if nblocks > per_group:
                    lax.fori_loop(1, nblocks // per_group, later, 0)
                return carry

            if dil == 1:
                residue(0, 0)
            else:
                lax.fori_loop(0, dil, residue, 0)

    nbr = len(DILATIONS)
    att_ref[0] = _merge_branches([o_scr[i] for i in range(nbr)], [l_scr[i] for i in range(nbr)])


def _attn_prompt(q, k, v, batch, seq):
    shape3 = (batch, seq, W_ATT)
    blk = pl.BlockSpec((1, seq, LANES), lambda b, p: (b, 0, p))
    att = pl.pallas_call(
        _attn_prompt_kernel,
        grid=(batch, W_ATT // LANES),
        in_specs=[blk, blk, blk],
        out_specs=blk,
        out_shape=jax.ShapeDtypeStruct(shape3, F32),
        scratch_shapes=[pltpu.VMEM((len(DILATIONS), seq, LANES), F32)] * 2,
        compiler_params=_cparams("parallel", "parallel"),
        name="attn_prompt",
    )(q.reshape(shape3), k.reshape(shape3), v.reshape(shape3))
    return att.reshape(batch * seq, W_ATT)


def _attn_sample_kernel(q_ref, kn_ref, vn_ref, kt_ref, vt_ref, att_ref):
    q = q_ref[0]
    kn = kn_ref[0]
    vn = vn_ref[0]
    kt = kt_ref[0]
    vt = vt_ref[0]
    wb = kt.shape[-1]
    scores = jnp.sum(kt * q, axis=1, keepdims=True)
    s_new = jnp.sum(q * kn, axis=1, keepdims=True)
    row = lax.broadcasted_iota(jnp.int32, (1, 1, wb), 2)
    outs, lses = [], []
    for _, dil in DILATIONS:
        on_grid = jnp.where(row % dil == 0, row, -1)
        s = jnp.where(on_grid >= wb - ATT_BLOCK * dil, scores, NEG)
        m = jnp.maximum(jnp.max(s, axis=2, keepdims=True), s_new)
        p = jnp.exp(s - m)
        p_new = jnp.exp(s_new - m)
        l = jnp.sum(p, axis=2, keepdims=True) + p_new
        outs.append((jnp.sum(vt * p, axis=2, keepdims=True) + p_new * vn) / l)
        lses.append(m + jnp.log(l))
    att_ref[0] = _merge_branches(outs, lses)


def _attn_sample(q, k_new, v_new, cache_kt, cache_vt):
    nb, wb = cache_kt.shape[0], cache_kt.shape[-1]
    assert all(wb % dil == 0 and wb >= ATT_BLOCK * dil for _, dil in DILATIONS)
    col = (nb, H_ATT, HEAD_DIM, 1)
    col_spec = pl.BlockSpec((1, H_ATT, HEAD_DIM, 1), lambda b: (b, 0, 0, 0))
    cache_spec = pl.BlockSpec((1, H_ATT, HEAD_DIM, wb), lambda b: (b, 0, 0, 0))
    att = pl.pallas_call(
        _attn_sample_kernel,
        grid=(nb,),
        in_specs=[col_spec] * 3 + [cache_spec] * 2,
        out_specs=col_spec,
        out_shape=jax.ShapeDtypeStruct(col, F32),
        compiler_params=_cparams("parallel"),
        name="attn_sample",
    )(q.reshape(col), k_new.reshape(col), v_new.reshape(col), cache_kt, cache_vt)
    return att.reshape(nb, W_ATT)


MLSTM_CHUNK = 128


MLSTM_SEQS = 2


def _mlstm_prompt_kernel(mqk_ref, mv_ref, mo_ref, gates_ref, cw_ref, cb_ref, bg_ref, ng_ref, tril_ref, triu_ref,
                         rep_ref, mh_ref, c_out_ref, n_out_ref, m_out_ref,
                         conv_scr, c_scr, n_scr, m_scr):
    @pl.when(pl.program_id(1) == 0)
    def _():
        conv_scr[:, 0:SUBLANES, :] = jnp.zeros((MLSTM_SEQS, SUBLANES, 2 * W_MLSTM), F32)
        c_scr[...] = jnp.zeros(c_scr.shape, F32)
        n_scr[...] = jnp.zeros(n_scr.shape, F32)
        m_scr[...] = jnp.zeros(m_scr.shape, F32)

    for i in range(MLSTM_SEQS):
        _mlstm_chunk(mqk_ref.at[i], mv_ref.at[i], mo_ref.at[i], gates_ref.at[i], cw_ref, cb_ref, bg_ref, ng_ref,
                     tril_ref, triu_ref, rep_ref, mh_ref.at[i], c_out_ref.at[i], n_out_ref.at[i], m_out_ref.at[i],
                     conv_scr.at[i], c_scr.at[i], n_scr.at[i], m_scr.at[i])


def _mlstm_chunk(mqk_ref, mv_ref, mo_ref, gates_ref, cw_ref, cb_ref, bg_ref, ng_ref, tril_ref, triu_ref,
                 rep_ref, mh_ref, c_out_ref, n_out_ref, m_out_ref, conv_scr, c_scr, n_scr, m_scr):
    L = MLSTM_CHUNK
    x = mqk_ref[...]
    conv_scr[SUBLANES:SUBLANES + L, :] = x
    conv = cb_ref[...] + cw_ref[CONV_W - 1:CONV_W, :] * x
    for j in range(CONV_W - 1):
        shift = CONV_W - 1 - j
        conv = conv + cw_ref[j:j + 1, :] * conv_scr[SUBLANES - shift:SUBLANES - shift + L, :]
    conv_scr[0:SUBLANES, :] = x[L - SUBLANES:L, :]
    qk = conv * _sigmoid(conv)
    q_all = qk[:, :W_MLSTM]
    k_all = qk[:, W_MLSTM:] * (HEAD_DIM ** -0.5)
    v_all = mv_ref[...]
    vt_all = v_all.T
    mo = mo_ref[...]

    assert L == LANES
    gb = gates_ref[...] + bg_ref[...]
    gbt = gb.T
    g_rep = _dot_parts(_split3(gb), rep_ref[...])
    ig_rep = g_rep[:, :H_MLSTM * LANES]
    lf_rep = _log_sigmoid(g_rep[:, H_MLSTM * LANES:])
    lf_row = _log_sigmoid(gbt[H_MLSTM:2 * H_MLSTM, :])
    tril = tril_ref[...]
    b_rep = _dot(tril, lf_rep.astype(BF16))
    rest = lf_rep - lf_rep.astype(BF16).astype(F32)
    b_rep = b_rep + _dot(tril, rest.astype(BF16))
    b_rep = b_rep + _dot(tril, (rest - rest.astype(BF16).astype(F32)).astype(BF16))
    b_row_all = _dot_parts(_split3(lf_row), triu_ref[...])

    ti = lax.broadcasted_iota(jnp.int32, (L, L), 0)
    si = lax.broadcasted_iota(jnp.int32, (L, L), 1)
    causal = si <= ti

    for h in range(H_MLSTM):
        hs = slice(h * HEAD_DIM, (h + 1) * HEAD_DIM)
        rep = slice(h * LANES, (h + 1) * LANES)
        qh = q_all[:, hs]
        kh = k_all[:, hs]
        vh = v_all[:, hs]
        bc = b_rep[:, rep]
        igc = ig_rep[:, rep]
        br = b_row_all[h:h + 1, :]
        igr = gbt[h:h + 1, :]
        m_prev = m_scr[h:h + 1, :]
        c_prev = c_scr[h]
        n_prev = n_scr[h:h + 1, :]

        log_d = jnp.where(causal, bc - br + igr, NEG)
        m_inter = bc + m_prev
        m_t = jnp.maximum(m_inter, jnp.max(log_d, axis=-1, keepdims=True))
        qh_b = qh.astype(BF16)
        kh_b = kh.astype(BF16)
        sd = _dot_nt(qh_b, kh_b) * jnp.exp(log_d - m_t)
        scale_inter = jnp.exp(m_inter - m_t)
        num = (scale_inter[:, :HEAD_DIM] * _dot_nt(qh_b, c_prev.astype(BF16))
               + _dot(sd.astype(BF16), vh.astype(BF16)))
        den = scale_inter * jnp.sum(qh * n_prev, axis=-1, keepdims=True) + jnp.sum(sd, axis=-1, keepdims=True)
        hh = num / jnp.maximum(jnp.abs(den), jnp.exp(-m_t))[:, :HEAD_DIM]

        m_new = m_t[L - 1:L, :]
        b_last = bc[L - 1:L, :]
        decay = jnp.exp(b_last + m_prev - m_new)
        w_row = jnp.exp(b_last - br + igr - m_new)
        w_col = jnp.exp(b_last - bc + igc - m_new)
        vtw = (vt_all[hs, :] * w_row).astype(BF16)
        c_new = decay[:, :HEAD_DIM] * c_prev + _dot(vtw, kh_b)
        n_new = decay[:, :HEAD_DIM] * n_prev + jnp.sum(w_col[:, :HEAD_DIM] * kh, axis=0, keepdims=True)
        c_scr[h] = c_new
        n_scr[h:h + 1, :] = n_new
        m_scr[h:h + 1, :] = m_new
        c_out_ref[h] = c_new
        n_out_ref[h:h + 1, :] = n_new
        m_out_ref[h:h + 1, :] = m_new

        y = hh * lax.rsqrt(jnp.mean(hh * hh, axis=-1, keepdims=True) + EPS) * ng_ref[:, hs]
        mh_ref[:, hs] = _sigmoid(mo[:, hs]) * y


def _mlstm_prompt(mqk, mv, mo, gates, batch, seq, conv_w, conv_b, bg, ng, tril, triu, rep):
    L = MLSTM_CHUNK
    nseq = MLSTM_SEQS
    assert batch % nseq == 0
    nchunk = seq // L
    chunk = lambda b, c: (b, c, 0)
    fixed = lambda b, c: (0, 0)
    return pl.pallas_call(
        _mlstm_prompt_kernel,
        grid=(batch // nseq, nchunk),
        in_specs=[
            pl.BlockSpec((nseq, L, 2 * W_MLSTM), chunk),
            pl.BlockSpec((nseq, L, W_MLSTM), chunk),
            pl.BlockSpec((nseq, L, W_MLSTM), chunk),
            pl.BlockSpec((nseq, L, LANES), chunk),
            pl.BlockSpec((CONV_W, 2 * W_MLSTM), fixed),
            pl.BlockSpec((1, 2 * W_MLSTM), fixed),
            pl.BlockSpec((1, LANES), fixed),
            pl.BlockSpec((1, W_MLSTM), fixed),
            pl.BlockSpec((L, L), fixed),
            pl.BlockSpec((L, L), fixed),
            pl.BlockSpec(rep.shape, fixed),
        ],
        out_specs=[
            pl.BlockSpec((nseq, L, W_MLSTM), chunk),
            pl.BlockSpec((nseq, H_MLSTM, HEAD_DIM, HEAD_DIM), lambda b, c: (b, 0, 0, 0)),
            pl.BlockSpec((nseq, H_MLSTM, HEAD_DIM), lambda b, c: (b, 0, 0)),
            pl.BlockSpec((nseq, H_MLSTM, LANES), lambda b, c: (b, 0, 0)),
        ],
        out_shape=[
            jax.ShapeDtypeStruct((batch, seq, W_MLSTM), F32),
            jax.ShapeDtypeStruct((batch, H_MLSTM, HEAD_DIM, HEAD_DIM), F32),
            jax.ShapeDtypeStruct((batch, H_MLSTM, HEAD_DIM), F32),
            jax.ShapeDtypeStruct((batch, H_MLSTM, LANES), F32),
        ],
        scratch_shapes=[
            pltpu.VMEM((nseq, L + SUBLANES, 2 * W_MLSTM), F32),
            pltpu.VMEM((nseq, H_MLSTM, HEAD_DIM, HEAD_DIM), F32),
            pltpu.VMEM((nseq, H_MLSTM, HEAD_DIM), F32),
            pltpu.VMEM((nseq, H_MLSTM, LANES), F32),
        ],
        compiler_params=_cparams("parallel", "arbitrary"),
        name="mlstm_prompt",
    )(mqk.reshape(batch, seq, -1), mv.reshape(batch, seq, -1), mo.reshape(batch, seq, -1),
      gates.reshape(batch, seq, -1), conv_w, conv_b, bg, ng, tril, triu, rep)


def _mlstm_sample_pre_kernel(mqk_ref, buf_ref, gates_ref, cw_ref, cb_ref, bg_ref, qk_ref, g_ref):
    conv = cb_ref[...] + cw_ref[CONV_W - 1:CONV_W, :] * mqk_ref[...]
    for j in range(CONV_W - 1):
        conv = conv + cw_ref[j:j + 1, :] * buf_ref[j]
    qk = conv * _sigmoid(conv)
    lane = lax.broadcasted_iota(jnp.int32, qk.shape, 1)
    qk_ref[...] = jnp.where(lane < W_MLSTM, qk, qk * (HEAD_DIM ** -0.5))
    gb = gates_ref[...] + bg_ref[...]
    glane = lax.broadcasted_iota(jnp.int32, gb.shape, 1)
    g_ref[...] = jnp.where(glane < H_MLSTM, gb, _log_sigmoid(gb))


def _mlstm_sample_step_kernel(q_ref, k_ref, v_ref, mo_ref, ig_ref, lf_ref, c0_ref, n0_ref, m0_ref, ng_ref,
                              mh_ref, c_ref, n_ref, m_ref):
    for h in range(H_MLSTM):
        q = q_ref[0, h]
        k = k_ref[0, h]
        v = v_ref[0, h]
        ig = ig_ref[0, h]
        lf = lf_ref[0, h]
        c0 = c0_ref[0, h]
        n0 = n0_ref[0, h]
        m0 = m0_ref[0, h]
        m_inter = lf + m0
        m_t = jnp.maximum(m_inter, ig)
        w_in = jnp.exp(ig - m_t)
        sd = jnp.sum(q * k, axis=-1, keepdims=True) * w_in
        scale_inter = jnp.exp(m_inter - m_t)
        num = scale_inter * jnp.sum(c0 * q, axis=-1, keepdims=True) + sd * v
        den = scale_inter * jnp.sum(n0 * q, axis=-1, keepdims=True) + sd
        hh = num / jnp.maximum(jnp.abs(den), jnp.exp(-m_t))
        decay = jnp.exp(lf + m0 - m_t)
        c_ref[0, h] = decay * c0 + w_in * (v * k)
        n_ref[0, h] = decay * n0 + w_in * k
        m_ref[0, h] = m_t
        y = hh * lax.rsqrt(jnp.mean(hh * hh, axis=0, keepdims=True) + EPS) * ng_ref[h]
        mh_ref[0, h] = _sigmoid(mo_ref[0, h]) * y


def _mlstm_sample(mqk, mv, mo, gates, conv_buf, c0, n0, m0, conv_w, conv_b, bg, ng):
    nb = mqk.shape[0]
    full = lambda *shape: pl.BlockSpec(shape, lambda: (0,) * len(shape))
    qk, g = pl.pallas_call(
        _mlstm_sample_pre_kernel,
        in_specs=[full(nb, 2 * W_MLSTM), full(CONV_W - 1, nb, 2 * W_MLSTM), full(nb, LANES),
                  full(CONV_W, 2 * W_MLSTM), full(1, 2 * W_MLSTM), full(1, LANES)],
        out_specs=[full(nb, 2 * W_MLSTM), full(nb, LANES)],
        out_shape=[jax.ShapeDtypeStruct((nb, 2 * W_MLSTM), F32), jax.ShapeDtypeStruct((nb, LANES), F32)],
        name="mlstm_sample_pre",
    )(mqk, jnp.swapaxes(conv_buf, 0, 1), gates, conv_w, conv_b, bg)
    row = (nb, H_MLSTM, 1, HEAD_DIM)
    col = (nb, H_MLSTM, HEAD_DIM, 1)
    one = (nb, H_MLSTM, 1, 1)
    mat = (nb, H_MLSTM, HEAD_DIM, HEAD_DIM)
    spec = lambda shape: pl.BlockSpec((1,) + shape[1:], lambda b: (b, 0, 0, 0))
    ins = [
        (qk[:, :W_MLSTM].reshape(row), row), (qk[:, W_MLSTM:].reshape(row), row), (mv.reshape(col), col),
        (mo.reshape(col), col), (g[:, :H_MLSTM].reshape(one), one), (g[:, H_MLSTM:2 * H_MLSTM].reshape(one), one),
        (c0, mat), (n0.reshape(row), row), (m0.reshape(one), one),
    ]
    mh, c, n, m = pl.pallas_call(
        _mlstm_sample_step_kernel,
        grid=(nb,),
        in_specs=[spec(s) for _, s in ins] + [pl.BlockSpec((H_MLSTM, HEAD_DIM, 1), lambda b: (0, 0, 0))],
        out_specs=[spec(col), spec(mat), spec(row), spec(one)],
        out_shape=[jax.ShapeDtypeStruct(s, F32) for s in (col, mat, row, one)],
        compiler_params=_cparams("parallel"),
        name="mlstm_sample_step",
    )(*[a for a, _ in ins], ng.reshape(H_MLSTM, HEAD_DIM, 1))
    return mh.reshape(nb, W_MLSTM), c, n.reshape(nb, H_MLSTM, HEAD_DIM), m.reshape(nb, H_MLSTM)


def _outproj_kernel(x_ref, att_ref, mh_ref, w_ref, g_ref, x1_ref, hn_ref):
    x1 = (x_ref[...] + _dot(att_ref[...].astype(BF16), w_ref[0:W_ATT, :])
          + _dot(mh_ref[...].astype(BF16), w_ref[W_ATT:, :]))
    x1_ref[...] = x1
    ms = jnp.mean(x1 * x1, axis=-1, keepdims=True)
    hn_ref[...] = (x1 * lax.rsqrt(ms + EPS) * g_ref[...]).astype(BF16)


def _outproj_into_kernel(x1_any, hn_any, *refs):
    del x1_any, hn_any
    _outproj_kernel(*refs)


def _out_projection(x2d, att, mh, w_out, g2, tm, n_out, out_rows, into=None):
    n_in = x2d.shape[0]
    in_blocks = n_in // tm
    first = out_rows[0] // tm
    row = lambda i: (i % in_blocks, 0)
    out_row = lambda i: (first + i, 0)
    fixed = lambda i: (0, 0)
    half = pl.BlockSpec((tm, W_ATT), row)
    in_specs = [pl.BlockSpec((tm, D_MODEL), row), half, half,
                pl.BlockSpec((D_MODEL, D_MODEL), fixed), pl.BlockSpec((1, D_MODEL), fixed)]
    args = (x2d, att, mh, w_out, g2)
    body = _outproj_kernel
    aliases = {}
    if into is not None:
        in_specs = [pl.BlockSpec(memory_space=pl.ANY)] * 2 + in_specs
        args = (*into, *args)
        body = _outproj_into_kernel
        aliases = {0: 0, 1: 1}
    return pl.pallas_call(
        body,
        grid=((out_rows[1] - out_rows[0]) // tm,),
        in_specs=in_specs,
        out_specs=[pl.BlockSpec((tm, D_MODEL), out_row), pl.BlockSpec((tm, D_MODEL), out_row)],
        out_shape=[jax.ShapeDtypeStruct((n_out, D_MODEL), F32), jax.ShapeDtypeStruct((n_out, D_MODEL), BF16)],
        input_output_aliases=aliases,
        compiler_params=_cparams("parallel"),
        name="out_projection",
    )(*args)


def _peer_scores_kernel(hn_ref, wq_ref, g_ref, sk_ref, s_ref):
    q = _dot(hn_ref[...], wq_ref[...])
    half = PEER_DKEY // 2
    for h in range(PEER_HEADS):
        qh = q[:, h * PEER_DKEY:(h + 1) * PEER_DKEY]
        qn = qh * lax.rsqrt(jnp.mean(qh * qh, axis=-1, keepdims=True) + EPS) * g_ref[...]
        for part in range(2):
            s = _dot_nt(qn[:, part * half:(part + 1) * half].astype(BF16), sk_ref[2 * h + part])
            r0 = (2 * h + part) * PEER_NKEYS
            s_ref[r0:r0 + PEER_NKEYS, :] = s.T


def _peer_scores(hn, wq, g, sk, tm):
    n = hn.shape[0]
    rows = PEER_HEADS * 2 * PEER_NKEYS
    return pl.pallas_call(
        _peer_scores_kernel,
        grid=(n // tm,),
        in_specs=[pl.BlockSpec((tm, D_MODEL), lambda i: (i, 0)),
                  pl.BlockSpec(wq.shape, lambda i: (0, 0)),
                  pl.BlockSpec((1, PEER_DKEY), lambda i: (0, 0)),
                  pl.BlockSpec(sk.shape, lambda i: (0, 0, 0))],
        out_specs=pl.BlockSpec((rows, tm), lambda i: (0, i)),
        out_shape=jax.ShapeDtypeStruct((rows, n), F32),
        compiler_params=_cparams("parallel"),
        name="peer_scores",
    )(hn, wq, g, sk)


_STAIR = [(a, b) for a in range(PEER_TOPK) for b in range(PEER_TOPK) if (a + 1) * (b + 1) <= PEER_TOPK]


def _peer_select_kernel(s1_ref, s2_ref, rank2_ref, cnt1_ref, e1_ref, e2_ref,
                        work1_ref, work2_ref, rank1_ref, rank2s_ref, vals1_ref, vals2_ref,
                        vals1t_ref, vals2t_ref, cntt_ref, cnt_ref, zt_ref, zrow_ref):
    shape = rank1_ref.shape
    keyf = lax.broadcasted_iota(jnp.int32, shape, 0).astype(F32)
    chunks = [slice(c * LANES, (c + 1) * LANES) for c in range(SUBLANES)]
    halves = ((s1_ref, work1_ref, rank1_ref, vals1_ref), (s2_ref, work2_ref, rank2s_ref, vals2_ref))

    def top16_distinct():
        for _, _, rank_ref, _ in halves:
            rank_ref[...] = jnp.full(shape, float(PEER_TOPK), F32)

        def one_round(a, previous):
            maxima = []
            for (src_ref, _, rank_ref, vals_ref), prev in zip(halves, previous):
                w = src_ref[...]
                rank_ref[...] = jnp.where(w == prev, jnp.asarray(a - 1, F32), rank_ref[...])
                mx = jnp.max(jnp.where(w < prev, w, -jnp.inf), axis=0, keepdims=True)
                vals_ref[pl.ds(a, 1), :] = mx
                maxima.append(mx)
            return tuple(maxima)

        top = jnp.full((1, shape[1]), jnp.inf, F32)
        last = lax.fori_loop(0, PEER_TOPK, one_round, (top, top))
        for (src_ref, _, rank_ref, _), prev in zip(halves, last):
            rank_ref[...] = jnp.where(src_ref[...] == prev, float(PEER_TOPK - 1), rank_ref[...])

    def top16_ties():
        for src_ref, work_ref, rank_ref, _ in halves:
            work_ref[...] = src_ref[...]
            rank_ref[...] = jnp.full(shape, float(PEER_TOPK), F32)

        def one_round(a, carry):
            for _, work_ref, rank_ref, vals_ref in halves:
                w = work_ref[...]
                mx = jnp.max(w, axis=0, keepdims=True)
                sel = keyf == jnp.min(jnp.where(w == mx, keyf, float(PEER_NKEYS)), axis=0, keepdims=True)
                work_ref[...] = jnp.where(sel, -jnp.inf, w)
                rank_ref[...] = jnp.where(sel, jnp.asarray(a, F32), rank_ref[...])
                vals_ref[pl.ds(a, 1), :] = mx
            return carry

        lax.fori_loop(0, PEER_TOPK, one_round, 0)

    top16_distinct()
    taken = [jnp.sum(jnp.where(rank_ref[...] < float(PEER_TOPK), 1.0, 0.0), axis=0, keepdims=True)
             for _, _, rank_ref, _ in halves]
    ties = jnp.max(jnp.maximum(jnp.abs(taken[0] - PEER_TOPK), jnp.abs(taken[1] - PEER_TOPK))) > 0.0

    @pl.when(ties)
    def _():
        top16_ties()

    for c, cs in enumerate(chunks):
        vals1t_ref[:, c, :] = vals1_ref[:, cs]
        vals2t_ref[:, c, :] = vals2_ref[:, cs]
    v1 = [vals1t_ref[a] for a in range(PEER_TOPK)]
    v2 = [vals2t_ref[b] for b in range(PEER_TOPK)]
    cand = [v1[a] + v2[b] for a, b in _STAIR]
    pos = []
    for i, (a, b) in enumerate(_STAIR):
        static = sum(1 for (a2, b2) in _STAIR if a2 <= a and b2 <= b and (a2, b2) != (a, b))
        pos.append(jnp.full(cand[0].shape, float(static), F32))
    for i, (ai, bi) in enumerate(_STAIR):
        for j in range(i + 1, len(_STAIR)):
            aj, bj = _STAIR[j]
            if ai <= aj and bi <= bj:
                continue
            i_first = cand[i] >= cand[j]
            pos[j] = pos[j] + jnp.where(i_first, 1.0, 0.0)
            pos[i] = pos[i] + jnp.where(i_first, 0.0, 1.0)
    e1s = [jnp.exp(v1[a] - v1[0]) for a in range(PEER_TOPK)]
    e2s = [jnp.exp(v2[b] - v2[0]) for b in range(PEER_TOPK)]
    z = jnp.zeros_like(cand[0])
    for a in range(PEER_TOPK):
        cnt_a = jnp.zeros_like(z)
        za = jnp.zeros_like(z)
        for i, (a2, b) in enumerate(_STAIR):
            if a2 != a:
                continue
            chosen = pos[i] < float(PEER_TOPK)
            cnt_a = cnt_a + jnp.where(chosen, 1.0, 0.0)
            za = za + jnp.where(chosen, e2s[b], 0.0)
        cntt_ref[a] = cnt_a
        z = z + e1s[a] * za
    zt_ref[0] = 1.0 / z
    for c, cs in enumerate(chunks):
        cnt_ref[:, cs] = cntt_ref[:, c, :]
        zrow_ref[:, cs] = zt_ref[:, c, :]

    cnt1_ref[...] = jnp.zeros(shape, F32)

    def spread(a, carry):
        cnt1_ref[...] = jnp.where(rank1_ref[...] == jnp.asarray(a, F32), cnt_ref[pl.ds(a, 1), :], cnt1_ref[...])
        return carry

    lax.fori_loop(0, PEER_TOPK, spread, 0)
    rank2_ref[...] = rank2s_ref[...].astype(BF16)
    e1_ref[...] = jnp.exp(s1_ref[...] - vals1_ref[0:1, :]) * zrow_ref[...]
    e2_ref[...] = jnp.exp(s2_ref[...] - vals2_ref[0:1, :]).astype(BF16)


def _peer_select(scores_t):
    n = scores_t.shape[1]
    tok = LANES * SUBLANES
    blk = (PEER_NKEYS, tok)
    out_rows = PEER_HEADS * PEER_NKEYS
    vm = lambda *shape: pltpu.VMEM(shape, F32)
    return pl.pallas_call(
        _peer_select_kernel,
        grid=(PEER_HEADS, n // tok),
        in_specs=[pl.BlockSpec(blk, lambda h, g: (2 * h, g)), pl.BlockSpec(blk, lambda h, g: (2 * h + 1, g))],
        out_specs=[pl.BlockSpec(blk, lambda h, g: (h, g))] * 4,
        out_shape=[jax.ShapeDtypeStruct((out_rows, n), dt) for dt in (BF16, F32, F32, BF16)],
        scratch_shapes=[vm(*blk)] * 4 + [vm(PEER_TOPK, tok)] * 2 + [vm(PEER_TOPK, SUBLANES, LANES)] * 3
                       + [vm(PEER_TOPK, tok), vm(1, SUBLANES, LANES), vm(1, tok)],
        compiler_params=_cparams("parallel", "parallel"),
        name="peer_select",
    )(scores_t, scores_t)


PEER_TOK_TILE = 512
PEER_TAIL_TILE = 128
PEER_I1_PER_STEP = 16
PEER_MXU_CHUNKS = 4
BF16_ROWS = 16


def _peer_dense_kernel(nblk, hn_ref, u_ref, vt_prev_ref, vt_last_ref, rank2_ref, e2_ref, cnt1_ref, e1_ref,
                       x1_ref, y_ref, ht_ref, p_even_ref, p_odd_ref, acc_ref):
    j = pl.program_id(1)
    tokens = hn_ref.shape[0]
    reps = PEER_NKEYS // BF16_ROWS
    assert PEER_I1_PER_STEP % SUBLANES == 0 and PEER_I1_PER_STEP % PEER_MXU_CHUNKS == 0

    def row_bf16(tile, s):
        x8 = jnp.broadcast_to(tile[s:s + 1, :], (SUBLANES, tokens))
        x16 = jnp.concatenate([x8, x8], axis=0).astype(BF16)
        return jnp.concatenate([x16] * reps, axis=0)

    def evaluate(p_write, p_read):
        first = pl.multiple_of(j * PEER_I1_PER_STEP, SUBLANES)
        cnt_tiles = [cnt1_ref[pl.ds(h * PEER_NKEYS + first, PEER_I1_PER_STEP), :] for h in range(PEER_HEADS)]
        e1_tiles = [e1_ref[pl.ds(h * PEER_NKEYS + first, PEER_I1_PER_STEP), :] for h in range(PEER_HEADS)]
        per_chunk = PEER_I1_PER_STEP // PEER_MXU_CHUNKS
        crow = per_chunk * PEER_NKEYS
        drow = D_MODEL // PEER_MXU_CHUNKS
        for c in range(PEER_MXU_CHUNKS):
            rc = slice(c * crow, (c + 1) * crow)
            rd = slice(c * drow, (c + 1) * drow)
            a_c = _dot(u_ref[rc, :], ht_ref[...])
            if p_read is not None:
                acc_ref[rd, :] += _dot(vt_prev_ref[rd, :], p_read[...])
            for sc in range(per_chunk):
                s = c * per_chunk + sc
                g = None
                for h in range(PEER_HEADS):
                    rows = slice(h * PEER_NKEYS, (h + 1) * PEER_NKEYS)
                    w = jnp.where(rank2_ref[rows, :] < row_bf16(cnt_tiles[h], s),
                                  e2_ref[rows, :] * row_bf16(e1_tiles[h], s), jnp.zeros((), BF16))
                    g = w if g is None else g + w
                a = a_c[sc * PEER_NKEYS:(sc + 1) * PEER_NKEYS, :]
                act = a + a * lax.erf(a * (2.0 ** -0.5))
                p_write[s * PEER_NKEYS:(s + 1) * PEER_NKEYS, :] = g * act.astype(BF16)

    @pl.when(j == 0)
    def _():
        acc_ref[...] = jnp.zeros_like(acc_ref)
        ht_ref[...] = hn_ref[...].astype(F32).T.astype(BF16)
        evaluate(p_even_ref, None)

    @pl.when(j % 2 == 1)
    def _():
        evaluate(p_odd_ref, p_even_ref)

    @pl.when(jnp.logical_and(j % 2 == 0, j > 0))
    def _():
        evaluate(p_even_ref, p_odd_ref)

    @pl.when(j == nblk - 1)
    def _():
        p_last = p_odd_ref if nblk % 2 == 0 else p_even_ref
        acc = acc_ref[...] + _dot(vt_last_ref[...], p_last[...])
        y_ref[...] = x1_ref[...] + acc.T


def _peer_dense(hn, u, vt, rank2, e2, cnt1, e1, x1, first_row, nrows, T):
    assert first_row % T == 0 and nrows % T == 0
    eb = PEER_I1_PER_STEP * PEER_NKEYS
    nexp = u.shape[0]
    sel_rows = PEER_HEADS * PEER_NKEYS
    nblk = nexp // eb
    t0 = first_row // T
    rows = lambda t, j: (t0 + t, 0)
    tok = lambda t, j: (0, t0 + t)
    return pl.pallas_call(
        functools.partial(_peer_dense_kernel, nblk),
        grid=(nrows // T, nblk),
        in_specs=[
            pl.BlockSpec((T, D_MODEL), rows),
            pl.BlockSpec((eb, D_MODEL), lambda t, j: (j, 0)),
            pl.BlockSpec((D_MODEL, eb), lambda t, j: (0, jnp.maximum(j - 1, 0))),
            pl.BlockSpec((D_MODEL, eb), lambda t, j: (0, nblk - 1)),
            pl.BlockSpec((sel_rows, T), tok),
            pl.BlockSpec((sel_rows, T), tok),
            pl.BlockSpec((sel_rows, T), tok),
            pl.BlockSpec((sel_rows, T), tok),
            pl.BlockSpec((T, D_MODEL), rows),
        ],
        out_specs=pl.BlockSpec((T, D_MODEL), lambda t, j: (t, 0)),
        out_shape=jax.ShapeDtypeStruct((nrows, D_MODEL), F32),
        scratch_shapes=[pltpu.VMEM((D_MODEL, T), BF16), pltpu.VMEM((eb, T), BF16), pltpu.VMEM((eb, T), BF16),
                        pltpu.VMEM((D_MODEL, T), F32)],
        compiler_params=_cparams("parallel", "arbitrary"),
        name="peer_dense",
    )(hn, u, vt, vt, rank2, e2, cnt1, e1, x1)


def _tri_constants():
    L = MLSTM_CHUNK
    tril = np.tril(np.ones((L, L), np.float32))
    rep = (np.arange(LANES)[:, None] == np.arange(2 * H_MLSTM * LANES)[None, :] // LANES).astype(np.float32)
    return jnp.asarray(tril, BF16), jnp.asarray(tril.T, BF16), jnp.asarray(rep, BF16)


def _head_constants():
    bd = np.kron(np.eye(H_ATT, dtype=np.float32), np.full((HEAD_DIM, HEAD_DIM), 1.0 / HEAD_DIM, np.float32))
    return jnp.asarray(bd, BF16)


def kernel(x_prompt, x_sample, cache_attn_k, cache_attn_v, state_mlstm_C, state_mlstm_n, state_mlstm_m,
           state_mlstm_conv, norm1_g, w_in, att_qnorm_g, att_knorm_g, b_gates, mlstm_conv_w, mlstm_conv_b,
           mlstm_norm_g, w_out, norm2_g, peer_w_query, peer_qnorm_g, peer_subkeys, peer_u, peer_v):
    batch, seq = x_prompt.shape[:2]
    nsamp = x_sample.shape[0]
    past_len = 16384
    assert norm1_g.shape[0] == 1 and x_sample.shape[1] == 1 and seq % MLSTM_CHUNK == 0
    wb = cache_attn_k.shape[2]
    li = 0
    bd = _head_constants()
    tril, triu, gate_rep = _tri_constants()

    w = w_in[li]
    gate_lo, gate_hi = 3072, 3072 + 2 * H_MLSTM
    w_main = jnp.concatenate([w[:, :gate_lo], w[:, gate_hi:]], axis=1).astype(BF16)
    wg = jnp.pad(w[:, gate_lo:gate_hi], ((0, 0), (0, LANES - 2 * H_MLSTM)))
    wg_hi = wg.astype(BF16)
    wg_lo = (wg - wg_hi.astype(F32)).astype(BF16)
    g1 = norm1_g[li][None, :]
    qg = jnp.tile(att_qnorm_g[li], H_ATT)[None, :]
    kg = jnp.tile(att_knorm_g[li], H_ATT)[None, :]
    bg = jnp.pad(b_gates[li], (0, LANES - 2 * H_MLSTM))[None, :]
    conv_w = mlstm_conv_w[li]
    conv_b = mlstm_conv_b[li][None, :]
    ng = mlstm_norm_g[li][None, :]
    wo = w_out[li].astype(BF16)
    g2 = norm2_g[li][None, :]
    wq = peer_w_query[li].astype(BF16)
    pqg = peer_qnorm_g[li][None, :]
    sk = peer_subkeys[li].reshape(PEER_HEADS * 2, PEER_NKEYS, PEER_DKEY // 2).astype(BF16)
    u_b = peer_u[li].astype(BF16)
    vt_b = (0.5 * peer_v[li]).astype(BF16).T

    n_p = batch * seq
    xp2 = x_prompt.reshape(n_p, D_MODEL)
    pos_p = jnp.arange(seq, dtype=jnp.int32)
    q, k, v, mqk, mv, mo, gates, kt, vt = _in_projection(xp2, pos_p, 512, g1, w_main, wg_hi, wg_lo, qg, kg, bd,
                                                         transposed_kv=True)
    att_p = _attn_prompt(q, k, v, batch, seq)
    mh_p, c_p, n_p_state, m_p = _mlstm_prompt(mqk, mv, mo, gates, batch, seq, conv_w, conv_b, bg, ng, tril, triu,
                                                  gate_rep)
    n_all = n_p + nsamp
    group = LANES * SUBLANES
    n_pad = -(-n_all // group) * group
    x1_all, hn_all = _out_projection(xp2, att_p, mh_p.reshape(n_p, W_MLSTM), wo, g2, 256, n_pad, (0, n_pad))
    wbp = min(wb, seq)
    new_k_prompt = jnp.transpose(kt, (0, 3, 1, 2))[None, :, seq - wbp:]
    new_v_prompt = jnp.transpose(vt, (0, 3, 1, 2))[None, :, seq - wbp:]
    new_conv_prompt = mqk.reshape(batch, seq, -1)[None, :, seq - (CONV_W - 1):]

    xs2 = x_sample.reshape(nsamp, D_MODEL)
    pos_s = jnp.full((nsamp,), past_len, dtype=jnp.int32)
    qs, ks, vs, mqk_s, mv_s, mo_s, gates_s = _in_projection(xs2, pos_s, nsamp, g1, w_main, wg_hi, wg_lo, qg, kg, bd)
    cache_kt = jnp.transpose(cache_attn_k[li], (0, 2, 3, 1))
    cache_vt = jnp.transpose(cache_attn_v[li], (0, 2, 3, 1))
    att_s = _attn_sample(qs, ks, vs, cache_kt, cache_vt)
    mh_s, c_s, n_s, m_s = _mlstm_sample(mqk_s, mv_s, mo_s, gates_s, state_mlstm_conv[li], state_mlstm_C[li],
                                        state_mlstm_n[li], state_mlstm_m[li], conv_w, conv_b, bg, ng)
    x1_all, hn_all = _out_projection(xs2, att_s, mh_s, wo, g2, nsamp, n_pad, (n_p, n_all),
                                     into=(x1_all, hn_all))
    new_conv_sample = jnp.concatenate([state_mlstm_conv[li][:, 1:], mqk_s[:, None, :]], axis=1)[None]

    scores_t = _peer_scores(hn_all, wq, pqg, sk, 256)
    rank2, cnt1, e1, e2 = _peer_select(scores_t)
    sel = (rank2, e2, cnt1, e1)
    y_main = _peer_dense(hn_all, u_b, vt_b, *sel, x1_all, 0, n_p, PEER_TOK_TILE)
    n_tail = -(-nsamp // PEER_TAIL_TILE) * PEER_TAIL_TILE
    y_tail = _peer_dense(hn_all, u_b, vt_b, *sel, x1_all, n_p, n_tail, PEER_TAIL_TILE)
    y_prompt = y_main.reshape(batch, seq, D_MODEL)
    y_sample = y_tail[:nsamp].reshape(nsamp, 1, D_MODEL)

    return (y_prompt, y_sample, new_k_prompt, new_v_prompt,
            ks.reshape(1, nsamp, 1, H_ATT, HEAD_DIM), vs.reshape(1, nsamp, 1, H_ATT, HEAD_DIM),
            c_p[None], n_p_state[None], m_p[None, :, :, 0], new_conv_prompt,
            c_s[None], n_s[None], m_s[None], new_conv_sample)
```

```python
import functools

import numpy as np
import jax
import jax.numpy as jnp
from jax import lax
from jax.experimental import pallas as pl
from jax.experimental.pallas import tpu as pltpu

F32 = jnp.float32
BF16 = jnp.bfloat16

D_MODEL = 1024
HEAD_DIM = 64
W_ATT = 512
W_MLSTM = 512
H_ATT = 8
H_MLSTM = 8
DILATIONS = ((128, 1), (512, 4), (2048, 16))
ATT_BLOCK = 128
ROPE_THETA = 10000.0
CONV_W = 4
PEER_HEADS = 8
PEER_NKEYS = 128
PEER_DKEY = 256
PEER_TOPK = 16
EPS = 1e-6
NEG = -1e30

LANES = 128
SUBLANES = 8
VMEM_LIMIT = 56 * 1024 * 1024


def _cparams(*sem):
    return pltpu.CompilerParams(dimension_semantics=sem, vmem_limit_bytes=VMEM_LIMIT)


def _split2(x):
    hi = x.astype(BF16)
    lo = (x - hi.astype(F32)).astype(BF16)
    return hi, lo


def _split3(x):
    hi = x.astype(BF16)
    r = x - hi.astype(F32)
    mid = r.astype(BF16)
    lo = (r - mid.astype(F32)).astype(BF16)
    return hi, mid, lo


def _dot(a, b):
    return jnp.dot(a, b, preferred_element_type=F32)


def _dot_nt(a, b):
    return lax.dot_general(a, b, (((1,), (1,)), ((), ())), preferred_element_type=F32)


def _dot_parts(parts, b):
    acc = _dot(parts[0], b)
    for p in parts[1:]:
        acc = acc + _dot(p, b)
    return acc


def _sigmoid(x):
    return 1.0 / (1.0 + jnp.exp(-x))


def _log_sigmoid(x):
    return jnp.minimum(x, 0.0) - jnp.log1p(jnp.exp(-jnp.abs(x)))


def _inproj_kernel(x_ref, g_ref, w_ref, wgh_ref, wgl_ref, qg_ref, kg_ref, cos_ref, sin_ref, bd_ref,
                   q_ref, k_ref, v_ref, mqk_ref, mv_ref, mo_ref, gates_ref, kt_ref=None, vt_ref=None):
    x = x_ref[...]
    ms = jnp.mean(x * x, axis=-1, keepdims=True)
    xn = x * lax.rsqrt(ms + EPS) * g_ref[...]
    xh, xl = _split2(xn)

    def seg(lo, hi):
        return _dot(xh, w_ref[:, lo:hi])

    bd = bd_ref[...]
    cos = cos_ref[...]
    sin = sin_ref[...]
    lane = lax.broadcasted_iota(jnp.int32, cos.shape, 1)
    first_half = (lane % HEAD_DIM) < (HEAD_DIM // 2)

    def head_norm_rope(a, g):
        sq = a * a
        hi, lo = _split2(sq)
        msq = _dot(hi, bd) + _dot(lo, bd)
        y = a * lax.rsqrt(msq + EPS) * g
        rot = jnp.where(first_half, pltpu.roll(y, W_ATT - HEAD_DIM // 2, 1), pltpu.roll(y, HEAD_DIM // 2, 1))
        return y * cos + rot * sin

    q_ref[...] = head_norm_rope(seg(0, 512), qg_ref[...]) * (HEAD_DIM ** -0.5)
    k = head_norm_rope(seg(512, 1024), kg_ref[...])
    v = seg(1024, 1536)
    k_ref[...] = k
    v_ref[...] = v
    if kt_ref is not None:
        kt_ref[0] = k.T.reshape(H_ATT, HEAD_DIM, k.shape[0])
        vt_ref[0] = v.T.reshape(H_ATT, HEAD_DIM, v.shape[0])
    mqk_ref[...] = seg(1536, 2560)
    mv_ref[...] = seg(2560, 3072)
    mo_ref[...] = seg(3072, 3584)
    gates_ref[...] = _dot(xh, wgh_ref[...]) + _dot(xl, wgh_ref[...]) + _dot(xh, wgl_ref[...])


def _rope_tables(pos):
    half = HEAD_DIM // 2
    inv = ROPE_THETA ** (-jnp.arange(half, dtype=F32) / half)
    ang = pos.astype(F32)[:, None] * inv[None, :]
    cos = jnp.cos(ang)
    sin = jnp.sin(ang)
    cos_h = jnp.concatenate([cos, cos], axis=-1)
    sin_h = jnp.concatenate([-sin, sin], axis=-1)
    return jnp.tile(cos_h, (1, H_ATT)), jnp.tile(sin_h, (1, H_ATT))


def _in_projection(x2d, pos, tm, norm_g, w_main, wg_hi, wg_lo, qg, kg, bd, transposed_kv=False):
    n = x2d.shape[0]
    cos, sin = _rope_tables(pos)
    pblocks = pos.shape[0] // tm
    row = lambda i: (i, 0)
    fixed = lambda i: (0, 0)
    tab = lambda i: (i % pblocks, 0)
    widths = (512, 512, 512, 1024, 512, 512, LANES)
    out_specs = [pl.BlockSpec((tm, w), row) for w in widths]
    out_shape = [jax.ShapeDtypeStruct((n, w), F32) for w in widths]
    if transposed_kv:
        tshape = (n // pos.shape[0], H_ATT, HEAD_DIM, pos.shape[0])
        out_specs += [pl.BlockSpec((1, H_ATT, HEAD_DIM, tm), lambda i: (i // pblocks, 0, 0, i % pblocks))] * 2
        out_shape += [jax.ShapeDtypeStruct(tshape, F32)] * 2
    return pl.pallas_call(
        _inproj_kernel,
        grid=(n // tm,),
        in_specs=[
            pl.BlockSpec((tm, D_MODEL), row),
            pl.BlockSpec((1, D_MODEL), fixed),
            pl.BlockSpec(w_main.shape, fixed),
            pl.BlockSpec(wg_hi.shape, fixed),
            pl.BlockSpec(wg_lo.shape, fixed),
            pl.BlockSpec((1, W_ATT), fixed),
            pl.BlockSpec((1, W_ATT), fixed),
            pl.BlockSpec((tm, W_ATT), tab),
            pl.BlockSpec((tm, W_ATT), tab),
            pl.BlockSpec((W_ATT, W_ATT), fixed),
        ],
        out_specs=out_specs,
        out_shape=out_shape,
        compiler_params=_cparams("parallel"),
        name="in_projection",
    )(x2d, norm_g, w_main, wg_hi, wg_lo, qg, kg, cos, sin, bd)


def _merge_branches(outs, lses):
    m = functools.reduce(jnp.maximum, lses)
    es = [jnp.exp(l - m) for l in lses]
    return sum(e * o for e, o in zip(es, outs)) / sum(es)


ATT_GROUP = 8


def _attn_prompt_kernel(q_ref, k_ref, v_ref, att_ref, o_scr, l_scr):
    seq = q_ref.shape[1]
    blk = ATT_BLOCK
    lane = lax.broadcasted_iota(jnp.int32, (1, LANES), 1)
    ii = lax.broadcasted_iota(jnp.int32, (2 * blk, 2 * blk), 0) % blk
    jj = lax.broadcasted_iota(jnp.int32, (2 * blk, 2 * blk), 1)
    mask_prev = jnp.minimum(jj - ii, ii + blk - jj) >= 0
    mask_first = (lax.broadcasted_iota(jnp.int32, (2 * blk, blk), 1)
                  <= lax.broadcasted_iota(jnp.int32, (2 * blk, blk), 0) % blk)

    first_head = (lane // HEAD_DIM) == 0

    def attend(q, k2, v2, mask):
        q2 = jnp.concatenate([jnp.where(first_head, q, 0.0), jnp.where(first_head, 0.0, q)], axis=0).astype(BF16)
        s = jnp.where(mask, _dot_nt(q2, k2), NEG)
        m = jnp.max(s, axis=-1, keepdims=True)
        p = jnp.exp(s - m)
        l = jnp.sum(p, axis=-1, keepdims=True)
        o = _dot((p / l).astype(BF16), v2)
        lse = jnp.broadcast_to(m + jnp.log(l), o.shape)
        return jnp.where(first_head, o[:blk], o[blk:]), jnp.where(first_head, lse[:blk], lse[blk:])

    def group(branch, dil, starts, chained, first_has_prev):
        def rows(at):
            return pl.ds(at, blk, stride=dil) if dil > 1 else pl.ds(at, blk)

        qs = [q_ref[0, rows(at), :] for at in starts]
        ks = [k_ref[0, rows(at), :].astype(BF16) for at in starts]
        vs = [v_ref[0, rows(at), :].astype(BF16) for at in starts]
        k_before = v_before = None
        if chained and first_has_prev:
            before = starts[0] - blk * dil
            k_before = k_ref[0, rows(before), :].astype(BF16)
            v_before = v_ref[0, rows(before), :].astype(BF16)
        results = []
        for i in range(len(starts)):
            kp, vp = (k_before, v_before) if i == 0 else (ks[i - 1], vs[i - 1])
            if chained and kp is not None:
                results.append(attend(qs[i], jnp.concatenate([kp, ks[i]], axis=0),
                                      jnp.concatenate([vp, vs[i]], axis=0), mask_prev))
            else:
                results.append(attend(qs[i], ks[i], vs[i], mask_first))
        for at, (o_pair, l_pair) in zip(starts, results):
            o_scr[branch, rows(at), :] = o_pair
            l_scr[branch, rows(at), :] = l_pair

    for branch, (_, dil) in enumerate(DILATIONS):
        nblocks = seq // dil // blk
        step = blk * dil
        if nblocks == 1:
            def classes(g, carry, branch=branch, dil=dil):
                group(branch, dil, [g * ATT_GROUP + i for i in range(ATT_GROUP)], False, False)
                return carry

            lax.fori_loop(0, dil // ATT_GROUP, classes, 0)
        else:
            per_group = min(ATT_GROUP, nblocks)
            assert nblocks % per_group == 0

            def residue(r, carry, branch=branch, dil=dil, nblocks=nblocks, step=step, per_group=per_group):
                group(branch, dil, [r + i * step for i in range(per_group)], True, False)

                def later(g, c):
                    base = g * (per_group * step)
                    base = pl.multiple_of(base, blk) if dil == 1 else r + base
                    group(branch, dil, [base + i * step for i in range(per_group)], True, True)
                    return c

                if nblocks > per_group:
                    lax.fori_loop(1, nblocks // per_group, later, 0)
                return carry

            if dil == 1:
                residue(0, 0)
            else:
                lax.fori_loop(0, dil, residue, 0)

    nbr = len(DILATIONS)
    att_ref[0] = _merge_branches([o_scr[i] for i in range(nbr)], [l_scr[i] for i in range(nbr)])


def _attn_prompt(q, k, v, batch, seq):
    shape3 = (batch, seq, W_ATT)
    blk = pl.BlockSpec((1, seq, LANES), lambda b, p: (b, 0, p))
    att = pl.pallas_call(
        _attn_prompt_kernel,
        grid=(batch, W_ATT // LANES),
        in_specs=[blk, blk, blk],
        out_specs=blk,
        out_shape=jax.ShapeDtypeStruct(shape3, F32),
        scratch_shapes=[pltpu.VMEM((len(DILATIONS), seq, LANES), F32)] * 2,
        compiler_params=_cparams("parallel", "parallel"),
        name="attn_prompt",
    )(q.reshape(shape3), k.reshape(shape3), v.reshape(shape3))
    return att.reshape(batch * seq, W_ATT)


def _attn_sample_kernel(q_ref, kn_ref, vn_ref, kt_ref, vt_ref, att_ref):
    q = q_ref[0]
    kn = kn_ref[0]
    vn = vn_ref[0]
    kt = kt_ref[0]
    vt = vt_ref[0]
    wb = kt.shape[-1]
    scores = jnp.sum(kt * q, axis=1, keepdims=True)
    s_new = jnp.sum(q * kn, axis=1, keepdims=True)
    row = lax.broadcasted_iota(jnp.int32, (1, 1, wb), 2)
    outs, lses = [], []
    for _, dil in DILATIONS:
        on_grid = jnp.where(row % dil == 0, row, -1)
        s = jnp.where(on_grid >= wb - ATT_BLOCK * dil, scores, NEG)
        m = jnp.maximum(jnp.max(s, axis=2, keepdims=True), s_new)
        p = jnp.exp(s - m)
        p_new = jnp.exp(s_new - m)
        l = jnp.sum(p, axis=2, keepdims=True) + p_new
        outs.append((jnp.sum(vt * p, axis=2, keepdims=True) + p_new * vn) / l)
        lses.append(m + jnp.log(l))
    att_ref[0] = _merge_branches(outs, lses)


def _attn_sample(q, k_new, v_new, cache_kt, cache_vt):
    nb, wb = cache_kt.shape[0], cache_kt.shape[-1]
    assert all(wb % dil == 0 and wb >= ATT_BLOCK * dil for _, dil in DILATIONS)
    col = (nb, H_ATT, HEAD_DIM, 1)
    col_spec = pl.BlockSpec((1, H_ATT, HEAD_DIM, 1), lambda b: (b, 0, 0, 0))
    cache_spec = pl.BlockSpec((1, H_ATT, HEAD_DIM, wb), lambda b: (b, 0, 0, 0))
    att = pl.pallas_call(
        _attn_sample_kernel,
        grid=(nb,),
        in_specs=[col_spec] * 3 + [cache_spec] * 2,
        out_specs=col_spec,
        out_shape=jax.ShapeDtypeStruct(col, F32),
        compiler_params=_cparams("parallel"),
        name="attn_sample",
    )(q.reshape(col), k_new.reshape(col), v_new.reshape(col), cache_kt, cache_vt)
    return att.reshape(nb, W_ATT)


MLSTM_CHUNK = 128


MLSTM_SEQS = 2


def _mlstm_prompt_kernel(mqk_ref, mv_ref, mo_ref, gates_ref, cw_ref, cb_ref, bg_ref, ng_ref, tril_ref, triu_ref,
                         rep_ref, mh_ref, c_out_ref, n_out_ref, m_out_ref,
                         conv_scr, c_scr, n_scr, m_scr):
    @pl.when(pl.program_id(1) == 0)
    def _():
        conv_scr[:, 0:SUBLANES, :] = jnp.zeros((MLSTM_SEQS, SUBLANES, 2 * W_MLSTM), F32)
        c_scr[...] = jnp.zeros(c_scr.shape, F32)
        n_scr[...] = jnp.zeros(n_scr.shape, F32)
        m_scr[...] = jnp.zeros(m_scr.shape, F32)

    for i in range(MLSTM_SEQS):
        _mlstm_chunk(mqk_ref.at[i], mv_ref.at[i], mo_ref.at[i], gates_ref.at[i], cw_ref, cb_ref, bg_ref, ng_ref,
                     tril_ref, triu_ref, rep_ref, mh_ref.at[i], c_out_ref.at[i], n_out_ref.at[i], m_out_ref.at[i],
                     conv_scr.at[i], c_scr.at[i], n_scr.at[i], m_scr.at[i])


def _mlstm_chunk(mqk_ref, mv_ref, mo_ref, gates_ref, cw_ref, cb_ref, bg_ref, ng_ref, tril_ref, triu_ref,
                 rep_ref, mh_ref, c_out_ref, n_out_ref, m_out_ref, conv_scr, c_scr, n_scr, m_scr):
    L = MLSTM_CHUNK
    x = mqk_ref[...]
    conv_scr[SUBLANES:SUBLANES + L, :] = x
    conv = cb_ref[...] + cw_ref[CONV_W - 1:CONV_W, :] * x
    for j in range(CONV_W - 1):
        shift = CONV_W - 1 - j
        conv = conv + cw_ref[j:j + 1, :] * conv_scr[SUBLANES - shift:SUBLANES - shift + L, :]
    conv_scr[0:SUBLANES, :] = x[L - SUBLANES:L, :]
    qk = conv * _sigmoid(conv)
    q_all = qk[:, :W_MLSTM]
    k_all = qk[:, W_MLSTM:] * (HEAD_DIM ** -0.5)
    v_all = mv_ref[...]
    vt_all = v_all.T
    mo = mo_ref[...]

    assert L == LANES
    gb = gates_ref[...] + bg_ref[...]
    gbt = gb.T
    g_rep = _dot_parts(_split3(gb), rep_ref[...])
    ig_rep = g_rep[:, :H_MLSTM * LANES]
    lf_rep = _log_sigmoid(g_rep[:, H_MLSTM * LANES:])
    lf_row = _log_sigmoid(gbt[H_MLSTM:2 * H_MLSTM, :])
    tril = tril_ref[...]
    b_rep = _dot(tril, lf_rep.astype(BF16))
    rest = lf_rep - lf_rep.astype(BF16).astype(F32)
    b_rep = b_rep + _dot(tril, rest.astype(BF16))
    b_rep = b_rep + _dot(tril, (rest - rest.astype(BF16).astype(F32)).astype(BF16))
    b_row_all = _dot_parts(_split3(lf_row), triu_ref[...])

    ti = lax.broadcasted_iota(jnp.int32, (L, L), 0)
    si = lax.broadcasted_iota(jnp.int32, (L, L), 1)
    causal = si <= ti

    for h in range(H_MLSTM):
        hs = slice(h * HEAD_DIM, (h + 1) * HEAD_DIM)
        rep = slice(h * LANES, (h + 1) * LANES)
        qh = q_all[:, hs]
        kh = k_all[:, hs]
        vh = v_all[:, hs]
        bc = b_rep[:, rep]
        igc = ig_rep[:, rep]
        br = b_row_all[h:h + 1, :]
        igr = gbt[h:h + 1, :]
        m_prev = m_scr[h:h + 1, :]
        c_prev = c_scr[h]
        n_prev = n_scr[h:h + 1, :]

        log_d = jnp.where(causal, bc - br + igr, NEG)
        m_inter = bc + m_prev
        m_t = jnp.maximum(m_inter, jnp.max(log_d, axis=-1, keepdims=True))
        qh_b = qh.astype(BF16)
        kh_b = kh.astype(BF16)
        sd = _dot_nt(qh_b, kh_b) * jnp.exp(log_d - m_t)
        scale_inter = jnp.exp(m_inter - m_t)
        num = (scale_inter[:, :HEAD_DIM] * _dot_nt(qh_b, c_prev.astype(BF16))
               + _dot(sd.astype(BF16), vh.astype(BF16)))
        den = scale_inter * jnp.sum(qh * n_prev, axis=-1, keepdims=True) + jnp.sum(sd, axis=-1, keepdims=True)
        hh = num / jnp.maximum(jnp.abs(den), jnp.exp(-m_t))[:, :HEAD_DIM]

        m_new = m_t[L - 1:L, :]
        b_last = bc[L - 1:L, :]
        decay = jnp.exp(b_last + m_prev - m_new)
        w_row = jnp.exp(b_last - br + igr - m_new)
        w_col = jnp.exp(b_last - bc + igc - m_new)
        vtw = (vt_all[hs, :] * w_row).astype(BF16)
        c_new = decay[:, :HEAD_DIM] * c_prev + _dot(vtw, kh_b)
        n_new = decay[:, :HEAD_DIM] * n_prev + jnp.sum(w_col[:, :HEAD_DIM] * kh, axis=0, keepdims=True)
        c_scr[h] = c_new
        n_scr[h:h + 1, :] = n_new
        m_scr[h:h + 1, :] = m_new
        c_out_ref[h] = c_new
        n_out_ref[h:h + 1, :] = n_new
        m_out_ref[h:h + 1, :] = m_new

        y = hh * lax.rsqrt(jnp.mean(hh * hh, axis=-1, keepdims=True) + EPS) * ng_ref[:, hs]
        mh_ref[:, hs] = _sigmoid(mo[:, hs]) * y


def _mlstm_prompt(mqk, mv, mo, gates, batch, seq, conv_w, conv_b, bg, ng, tril, triu, rep):
    L = MLSTM_CHUNK
    nseq = MLSTM_SEQS
    assert batch % nseq == 0
    nchunk = seq // L
    chunk = lambda b, c: (b, c, 0)
    fixed = lambda b, c: (0, 0)
    return pl.pallas_call(
        _mlstm_prompt_kernel,
        grid=(batch // nseq, nchunk),
        in_specs=[
            pl.BlockSpec((nseq, L, 2 * W_MLSTM), chunk),
            pl.BlockSpec((nseq, L, W_MLSTM), chunk),
            pl.BlockSpec((nseq, L, W_MLSTM), chunk),
            pl.BlockSpec((nseq, L, LANES), chunk),
            pl.BlockSpec((CONV_W, 2 * W_MLSTM), fixed),
            pl.BlockSpec((1, 2 * W_MLSTM), fixed),
            pl.BlockSpec((1, LANES), fixed),
            pl.BlockSpec((1, W_MLSTM), fixed),
            pl.BlockSpec((L, L), fixed),
            pl.BlockSpec((L, L), fixed),
            pl.BlockSpec(rep.shape, fixed),
        ],
        out_specs=[
            pl.BlockSpec((nseq, L, W_MLSTM), chunk),
            pl.BlockSpec((nseq, H_MLSTM, HEAD_DIM, HEAD_DIM), lambda b, c: (b, 0, 0, 0)),
            pl.BlockSpec((nseq, H_MLSTM, HEAD_DIM), lambda b, c: (b, 0, 0)),
            pl.BlockSpec((nseq, H_MLSTM, LANES), lambda b, c: (b, 0, 0)),
        ],
        out_shape=[
            jax.ShapeDtypeStruct((batch, seq, W_MLSTM), F32),
            jax.ShapeDtypeStruct((batch, H_MLSTM, HEAD_DIM, HEAD_DIM), F32),
            jax.ShapeDtypeStruct((batch, H_MLSTM, HEAD_DIM), F32),
            jax.ShapeDtypeStruct((batch, H_MLSTM, LANES), F32),
        ],
        scratch_shapes=[
            pltpu.VMEM((nseq, L + SUBLANES, 2 * W_MLSTM), F32),
            pltpu.VMEM((nseq, H_MLSTM, HEAD_DIM, HEAD_DIM), F32),
            pltpu.VMEM((nseq, H_MLSTM, HEAD_DIM), F32),
            pltpu.VMEM((nseq, H_MLSTM, LANES), F32),
        ],
        compiler_params=_cparams("parallel", "arbitrary"),
        name="mlstm_prompt",
    )(mqk.reshape(batch, seq, -1), mv.reshape(batch, seq, -1), mo.reshape(batch, seq, -1),
      gates.reshape(batch, seq, -1), conv_w, conv_b, bg, ng, tril, triu, rep)


def _mlstm_sample_pre_kernel(mqk_ref, buf_ref, gates_ref, cw_ref, cb_ref, bg_ref, qk_ref, g_ref):
    conv = cb_ref[...] + cw_ref[CONV_W - 1:CONV_W, :] * mqk_ref[...]
    for j in range(CONV_W - 1):
        conv = conv + cw_ref[j:j + 1, :] * buf_ref[j]
    qk = conv * _sigmoid(conv)
    lane = lax.broadcasted_iota(jnp.int32, qk.shape, 1)
    qk_ref[...] = jnp.where(lane < W_MLSTM, qk, qk * (HEAD_DIM ** -0.5))
    gb = gates_ref[...] + bg_ref[...]
    glane = lax.broadcasted_iota(jnp.int32, gb.shape, 1)
    g_ref[...] = jnp.where(glane < H_MLSTM, gb, _log_sigmoid(gb))


def _mlstm_sample_step_kernel(q_ref, k_ref, v_ref, mo_ref, ig_ref, lf_ref, c0_ref, n0_ref, m0_ref, ng_ref,
                              mh_ref, c_ref, n_ref, m_ref):
    for h in range(H_MLSTM):
        q = q_ref[0, h]
        k = k_ref[0, h]
        v = v_ref[0, h]
        ig = ig_ref[0, h]
        lf = lf_ref[0, h]
        c0 = c0_ref[0, h]
        n0 = n0_ref[0, h]
        m0 = m0_ref[0, h]
        m_inter = lf + m0
        m_t = jnp.maximum(m_inter, ig)
        w_in = jnp.exp(ig - m_t)
        sd = jnp.sum(q * k, axis=-1, keepdims=True) * w_in
        scale_inter = jnp.exp(m_inter - m_t)
        num = scale_inter * jnp.sum(c0 * q, axis=-1, keepdims=True) + sd * v
        den = scale_inter * jnp.sum(n0 * q, axis=-1, keepdims=True) + sd
        hh = num / jnp.maximum(jnp.abs(den), jnp.exp(-m_t))
        decay = jnp.exp(lf + m0 - m_t)
        c_ref[0, h] = decay * c0 + w_in * (v * k)
        n_ref[0, h] = decay * n0 + w_in * k
        m_ref[0, h] = m_t
        y = hh * lax.rsqrt(jnp.mean(hh * hh, axis=0, keepdims=True) + EPS) * ng_ref[h]
        mh_ref[0, h] = _sigmoid(mo_ref[0, h]) * y


def _mlstm_sample(mqk, mv, mo, gates, conv_buf, c0, n0, m0, conv_w, conv_b, bg, ng):
    nb = mqk.shape[0]
    full = lambda *shape: pl.BlockSpec(shape, lambda: (0,) * len(shape))
    qk, g = pl.pallas_call(
        _mlstm_sample_pre_kernel,
        in_specs=[full(nb, 2 * W_MLSTM), full(CONV_W - 1, nb, 2 * W_MLSTM), full(nb, LANES),
                  full(CONV_W, 2 * W_MLSTM), full(1, 2 * W_MLSTM), full(1, LANES)],
        out_specs=[full(nb, 2 * W_MLSTM), full(nb, LANES)],
        out_shape=[jax.ShapeDtypeStruct((nb, 2 * W_MLSTM), F32), jax.ShapeDtypeStruct((nb, LANES), F32)],
        name="mlstm_sample_pre",
    )(mqk, jnp.swapaxes(conv_buf, 0, 1), gates, conv_w, conv_b, bg)
    row = (nb, H_MLSTM, 1, HEAD_DIM)
    col = (nb, H_MLSTM, HEAD_DIM, 1)
    one = (nb, H_MLSTM, 1, 1)
    mat = (nb, H_MLSTM, HEAD_DIM, HEAD_DIM)
    spec = lambda shape: pl.BlockSpec((1,) + shape[1:], lambda b: (b, 0, 0, 0))
    ins = [
        (qk[:, :W_MLSTM].reshape(row), row), (qk[:, W_MLSTM:].reshape(row), row), (mv.reshape(col), col),
        (mo.reshape(col), col), (g[:, :H_MLSTM].reshape(one), one), (g[:, H_MLSTM:2 * H_MLSTM].reshape(one), one),
        (c0, mat), (n0.reshape(row), row), (m0.reshape(one), one),
    ]
    mh, c, n, m = pl.pallas_call(
        _mlstm_sample_step_kernel,
        grid=(nb,),
        in_specs=[spec(s) for _, s in ins] + [pl.BlockSpec((H_MLSTM, HEAD_DIM, 1), lambda b: (0, 0, 0))],
        out_specs=[spec(col), spec(mat), spec(row), spec(one)],
        out_shape=[jax.ShapeDtypeStruct(s, F32) for s in (col, mat, row, one)],
        compiler_params=_cparams("parallel"),
        name="mlstm_sample_step",
    )(*[a for a, _ in ins], ng.reshape(H_MLSTM, HEAD_DIM, 1))
    return mh.reshape(nb, W_MLSTM), c, n.reshape(nb, H_MLSTM, HEAD_DIM), m.reshape(nb, H_MLSTM)


def _outproj_kernel(x_ref, att_ref, mh_ref, w_ref, g_ref, x1_ref, hn_ref):
    x1 = (x_ref[...] + _dot(att_ref[...].astype(BF16), w_ref[0:W_ATT, :])
          + _dot(mh_ref[...].astype(BF16), w_ref[W_ATT:, :]))
    x1_ref[...] = x1
    ms = jnp.mean(x1 * x1, axis=-1, keepdims=True)
    hn_ref[...] = (x1 * lax.rsqrt(ms + EPS) * g_ref[...]).astype(BF16)


def _outproj_into_kernel(x1_any, hn_any, *refs):
    del x1_any, hn_any
    _outproj_kernel(*refs)


def _out_projection(x2d, att, mh, w_out, g2, tm, n_out, out_rows, into=None):
    n_in = x2d.shape[0]
    in_blocks = n_in // tm
    first = out_rows[0] // tm
    row = lambda i: (i % in_blocks, 0)
    out_row = lambda i: (first + i, 0)
    fixed = lambda i: (0, 0)
    half = pl.BlockSpec((tm, W_ATT), row)
    in_specs = [pl.BlockSpec((tm, D_MODEL), row), half, half,
                pl.BlockSpec((D_MODEL, D_MODEL), fixed), pl.BlockSpec((1, D_MODEL), fixed)]
    args = (x2d, att, mh, w_out, g2)
    body = _outproj_kernel
    aliases = {}
    if into is not None:
        in_specs = [pl.BlockSpec(memory_space=pl.ANY)] * 2 + in_specs
        args = (*into, *args)
        body = _outproj_into_kernel
        aliases = {0: 0, 1: 1}
    return pl.pallas_call(
        body,
        grid=((out_rows[1] - out_rows[0]) // tm,),
        in_specs=in_specs,
        out_specs=[pl.BlockSpec((tm, D_MODEL), out_row), pl.BlockSpec((tm, D_MODEL), out_row)],
        out_shape=[jax.ShapeDtypeStruct((n_out, D_MODEL), F32), jax.ShapeDtypeStruct((n_out, D_MODEL), BF16)],
        input_output_aliases=aliases,
        compiler_params=_cparams("parallel"),
        name="out_projection",
    )(*args)


def _peer_scores_kernel(hn_ref, wq_ref, g_ref, sk_ref, s_ref):
    q = _dot(hn_ref[...], wq_ref[...])
    half = PEER_DKEY // 2
    for h in range(PEER_HEADS):
        qh = q[:, h * PEER_DKEY:(h + 1) * PEER_DKEY]
        qn = qh * lax.rsqrt(jnp.mean(qh * qh, axis=-1, keepdims=True) + EPS) * g_ref[...]
        for part in range(2):
            s = _dot_nt(qn[:, part * half:(part + 1) * half].astype(BF16), sk_ref[2 * h + part])
            r0 = (2 * h + part) * PEER_NKEYS
            s_ref[r0:r0 + PEER_NKEYS, :] = s.T


def _peer_scores(hn, wq, g, sk, tm):
    n = hn.shape[0]
    rows = PEER_HEADS * 2 * PEER_NKEYS
    return pl.pallas_call(
        _peer_scores_kernel,
        grid=(n // tm,),
        in_specs=[pl.BlockSpec((tm, D_MODEL), lambda i: (i, 0)),
                  pl.BlockSpec(wq.shape, lambda i: (0, 0)),
                  pl.BlockSpec((1, PEER_DKEY), lambda i: (0, 0)),
                  pl.BlockSpec(sk.shape, lambda i: (0, 0, 0))],
        out_specs=pl.BlockSpec((rows, tm), lambda i: (0, i)),
        out_shape=jax.ShapeDtypeStruct((rows, n), F32),
        compiler_params=_cparams("parallel"),
        name="peer_scores",
    )(hn, wq, g, sk)


_STAIR = [(a, b) for a in range(PEER_TOPK) for b in range(PEER_TOPK) if (a + 1) * (b + 1) <= PEER_TOPK]


def _peer_select_kernel(s1_ref, s2_ref, rank2_ref, cnt1_ref, e1_ref, e2_ref,
                        work1_ref, work2_ref, rank1_ref, rank2s_ref, vals1_ref, vals2_ref,
                        vals1t_ref, vals2t_ref, cntt_ref, cnt_ref, zt_ref, zrow_ref):
    shape = rank1_ref.shape
    keyf = lax.broadcasted_iota(jnp.int32, shape, 0).astype(F32)
    chunks = [slice(c * LANES, (c + 1) * LANES) for c in range(SUBLANES)]
    halves = ((s1_ref, work1_ref, rank1_ref, vals1_ref), (s2_ref, work2_ref, rank2s_ref, vals2_ref))

    def top16_distinct():
        rank2s_ref[...] = jnp.full(shape, float(PEER_TOPK), F32)

        def one_round(a, previous):
            maxima = []
            for (src_ref, _, rank_ref, vals_ref), prev in zip(halves, previous):
                w = src_ref[...]
                if rank_ref is rank2s_ref:
                    rank_ref[...] = jnp.where(w == prev, jnp.asarray(a - 1, F32), rank_ref[...])
                mx = jnp.max(jnp.where(w < prev, w, -jnp.inf), axis=0, keepdims=True)
                vals_ref[pl.ds(a, 1), :] = mx
                maxima.append(mx)
            return tuple(maxima)

        top = jnp.full((1, shape[1]), jnp.inf, F32)
        last = lax.fori_loop(0, PEER_TOPK, one_round, (top, top))
        rank2s_ref[...] = jnp.where(s2_ref[...] == last[1], float(PEER_TOPK - 1), rank2s_ref[...])

    def top16_ties():
        for src_ref, work_ref, rank_ref, _ in halves:
            work_ref[...] = src_ref[...]
            rank_ref[...] = jnp.full(shape, float(PEER_TOPK), F32)

        def one_round(a, carry):
            for _, work_ref, rank_ref, vals_ref in halves:
                w = work_ref[...]
                mx = jnp.max(w, axis=0, keepdims=True)
                sel = keyf == jnp.min(jnp.where(w == mx, keyf, float(PEER_NKEYS)), axis=0, keepdims=True)
                work_ref[...] = jnp.where(sel, -jnp.inf, w)
                rank_ref[...] = jnp.where(sel, jnp.asarray(a, F32), rank_ref[...])
                vals_ref[pl.ds(a, 1), :] = mx
            return carry

        lax.fori_loop(0, PEER_TOPK, one_round, 0)

    top16_distinct()
    taken = [jnp.sum(jnp.where(src_ref[...] >= vals_ref[PEER_TOPK - 1:PEER_TOPK, :], 1.0, 0.0), axis=0, keepdims=True)
             for src_ref, _, _, vals_ref in halves]
    ties = jnp.max(jnp.maximum(jnp.abs(taken[0] - PEER_TOPK), jnp.abs(taken[1] - PEER_TOPK))) > 0.0

    @pl.when(ties)
    def _():
        top16_ties()

    for c, cs in enumerate(chunks):
        vals1t_ref[:, c, :] = vals1_ref[:, cs]
        vals2t_ref[:, c, :] = vals2_ref[:, cs]
    v1 = [vals1t_ref[a] for a in range(PEER_TOPK)]
    v2 = [vals2t_ref[b] for b in range(PEER_TOPK)]
    cand = [v1[a] + v2[b] for a, b in _STAIR]
    pos = []
    for i, (a, b) in enumerate(_STAIR):
        static = sum(1 for (a2, b2) in _STAIR if a2 <= a and b2 <= b and (a2, b2) != (a, b))
        pos.append(jnp.full(cand[0].shape, float(static), F32))
    for i, (ai, bi) in enumerate(_STAIR):
        for j in range(i + 1, len(_STAIR)):
            aj, bj = _STAIR[j]
            if ai <= aj and bi <= bj:
                continue
            i_first = cand[i] >= cand[j]
            pos[j] = pos[j] + jnp.where(i_first, 1.0, 0.0)
            pos[i] = pos[i] + jnp.where(i_first, 0.0, 1.0)
    e1s = [jnp.exp(v1[a] - v1[0]) for a in range(PEER_TOPK)]
    e2s = [jnp.exp(v2[b] - v2[0]) for b in range(PEER_TOPK)]
    z = jnp.zeros_like(cand[0])
    for a in range(PEER_TOPK):
        cnt_a = jnp.zeros_like(z)
        za = jnp.zeros_like(z)
        for i, (a2, b) in enumerate(_STAIR):
            if a2 != a:
                continue
            chosen = pos[i] < float(PEER_TOPK)
            cnt_a = cnt_a + jnp.where(chosen, 1.0, 0.0)
            za = za + jnp.where(chosen, e2s[b], 0.0)
        cntt_ref[a] = cnt_a
        z = z + e1s[a] * za
    zt_ref[0] = 1.0 / z
    for c, cs in enumerate(chunks):
        cnt_ref[:, cs] = cntt_ref[:, c, :]
        zrow_ref[:, cs] = zt_ref[:, c, :]

    cnt1_ref[...] = jnp.zeros(shape, F32)

    def spread_by_rank(a, carry):
        cnt1_ref[...] = jnp.where(rank1_ref[...] == jnp.asarray(a, F32), cnt_ref[pl.ds(a, 1), :], cnt1_ref[...])
        return carry

    def spread_by_value(a, carry):
        cnt1_ref[...] = jnp.where(s1_ref[...] == vals1_ref[pl.ds(a, 1), :], cnt_ref[pl.ds(a, 1), :], cnt1_ref[...])
        return carry

    @pl.when(ties)
    def _():
        lax.fori_loop(0, PEER_TOPK, spread_by_rank, 0)

    @pl.when(jnp.logical_not(ties))
    def _():
        lax.fori_loop(0, PEER_TOPK, spread_by_value, 0)

    rank2_ref[...] = rank2s_ref[...].astype(BF16)
    e1_ref[...] = jnp.exp(s1_ref[...] - vals1_ref[0:1, :]) * zrow_ref[...]
    e2_ref[...] = jnp.exp(s2_ref[...] - vals2_ref[0:1, :]).astype(BF16)


def _peer_select(scores_t):
    n = scores_t.shape[1]
    tok = LANES * SUBLANES
    blk = (PEER_NKEYS, tok)
    out_rows = PEER_HEADS * PEER_NKEYS
    vm = lambda *shape: pltpu.VMEM(shape, F32)
    return pl.pallas_call(
        _peer_select_kernel,
        grid=(PEER_HEADS, n // tok),
        in_specs=[pl.BlockSpec(blk, lambda h, g: (2 * h, g)), pl.BlockSpec(blk, lambda h, g: (2 * h + 1, g))],
        out_specs=[pl.BlockSpec(blk, lambda h, g: (h, g))] * 4,
        out_shape=[jax.ShapeDtypeStruct((out_rows, n), dt) for dt in (BF16, F32, F32, BF16)],
        scratch_shapes=[vm(*blk)] * 4 + [vm(PEER_TOPK, tok)] * 2 + [vm(PEER_TOPK, SUBLANES, LANES)] * 3
                       + [vm(PEER_TOPK, tok), vm(1, SUBLANES, LANES), vm(1, tok)],
        compiler_params=_cparams("parallel", "parallel"),
        name="peer_select",
    )(scores_t, scores_t)


PEER_TOK_TILE = 512
PEER_TAIL_TILE = 128
PEER_I1_PER_STEP = 16
PEER_MXU_CHUNKS = 2
BF16_ROWS = 16


def _peer_dense_kernel(nblk, hn_ref, u_ref, vt_prev_ref, vt_last_ref, rank2_ref, e2_ref, cnt1_ref, e1_ref,
                       x1_ref, y_ref, ht_ref, p_even_ref, p_odd_ref, acc_ref):
    j = pl.program_id(1)
    tokens = hn_ref.shape[0]
    reps = PEER_NKEYS // BF16_ROWS
    assert PEER_I1_PER_STEP % SUBLANES == 0 and PEER_I1_PER_STEP % PEER_MXU_CHUNKS == 0

    def row_bf16(tile, s):
        x8 = jnp.broadcast_to(tile[s:s + 1, :], (SUBLANES, tokens))
        x16 = jnp.concatenate([x8, x8], axis=0).astype(BF16)
        return jnp.concatenate([x16] * reps, axis=0)

    def evaluate(p_write, p_read):
        first = pl.multiple_of(j * PEER_I1_PER_STEP, SUBLANES)
        cnt_tiles = [cnt1_ref[pl.ds(h * PEER_NKEYS + first, PEER_I1_PER_STEP), :] for h in range(PEER_HEADS)]
        e1_tiles = [e1_ref[pl.ds(h * PEER_NKEYS + first, PEER_I1_PER_STEP), :] for h in range(PEER_HEADS)]
        per_chunk = PEER_I1_PER_STEP // PEER_MXU_CHUNKS
        crow = per_chunk * PEER_NKEYS
        drow = D_MODEL // PEER_MXU_CHUNKS
        for c in range(PEER_MXU_CHUNKS):
            rc = slice(c * crow, (c + 1) * crow)
            rd = slice(c * drow, (c + 1) * drow)
            a_c = _dot(u_ref[rc, :], ht_ref[...])
            if p_read is not None:
                acc_ref[rd, :] += _dot(vt_prev_ref[rd, :], p_read[...])
            for sc in range(per_chunk):
                s = c * per_chunk + sc
                g = None
                for h in range(PEER_HEADS):
                    rows = slice(h * PEER_NKEYS, (h + 1) * PEER_NKEYS)
                    w = jnp.where(rank2_ref[rows, :] < row_bf16(cnt_tiles[h], s),
                                  e2_ref[rows, :] * row_bf16(e1_tiles[h], s), jnp.zeros((), BF16))
                    g = w if g is None else g + w
                a = a_c[sc * PEER_NKEYS:(sc + 1) * PEER_NKEYS, :]
                act = a + a * lax.erf(a * (2.0 ** -0.5))
                p_write[s * PEER_NKEYS:(s + 1) * PEER_NKEYS, :] = g * act.astype(BF16)

    @pl.when(j == 0)
    def _():
        acc_ref[...] = jnp.zeros_like(acc_ref)
        ht_ref[...] = hn_ref[...].astype(F32).T.astype(BF16)
        evaluate(p_even_ref, None)

    @pl.when(j % 2 == 1)
    def _():
        evaluate(p_odd_ref, p_even_ref)

    @pl.when(jnp.logical_and(j % 2 == 0, j > 0))
    def _():
        evaluate(p_even_ref, p_odd_ref)

    @pl.when(j == nblk - 1)
    def _():
        p_last = p_odd_ref if nblk % 2 == 0 else p_even_ref
        acc = acc_ref[...] + _dot(vt_last_ref[...], p_last[...])
        y_ref[...] = x1_ref[...] + acc.T


def _peer_dense(hn, u, vt, rank2, e2, cnt1, e1, x1, first_row, nrows, T):
    assert first_row % T == 0 and nrows % T == 0
    eb = PEER_I1_PER_STEP * PEER_NKEYS
    nexp = u.shape[0]
    sel_rows = PEER_HEADS * PEER_NKEYS
    nblk = nexp // eb
    t0 = first_row // T
    rows = lambda t, j: (t0 + t, 0)
    tok = lambda t, j: (0, t0 + t)
    return pl.pallas_call(
        functools.partial(_peer_dense_kernel, nblk),
        grid=(nrows // T, nblk),
        in_specs=[
            pl.BlockSpec((T, D_MODEL), rows),
            pl.BlockSpec((eb, D_MODEL), lambda t, j: (j, 0)),
            pl.BlockSpec((D_MODEL, eb), lambda t, j: (0, jnp.maximum(j - 1, 0))),
            pl.BlockSpec((D_MODEL, eb), lambda t, j: (0, nblk - 1)),
            pl.BlockSpec((sel_rows, T), tok),
            pl.BlockSpec((sel_rows, T), tok),
            pl.BlockSpec((sel_rows, T), tok),
            pl.BlockSpec((sel_rows, T), tok),
            pl.BlockSpec((T, D_MODEL), rows),
        ],
        out_specs=pl.BlockSpec((T, D_MODEL), lambda t, j: (t, 0)),
        out_shape=jax.ShapeDtypeStruct((nrows, D_MODEL), F32),
        scratch_shapes=[pltpu.VMEM((D_MODEL, T), BF16), pltpu.VMEM((eb, T), BF16), pltpu.VMEM((eb, T), BF16),
                        pltpu.VMEM((D_MODEL, T), F32)],
        compiler_params=_cparams("parallel", "arbitrary"),
        name="peer_dense",
    )(hn, u, vt, vt, rank2, e2, cnt1, e1, x1)


def _tri_constants():
    L = MLSTM_CHUNK
    tril = np.tril(np.ones((L, L), np.float32))
    rep = (np.arange(LANES)[:, None] == np.arange(2 * H_MLSTM * LANES)[None, :] // LANES).astype(np.float32)
    return jnp.asarray(tril, BF16), jnp.asarray(tril.T, BF16), jnp.asarray(rep, BF16)


def _head_constants():
    bd = np.kron(np.eye(H_ATT, dtype=np.float32), np.full((HEAD_DIM, HEAD_DIM), 1.0 / HEAD_DIM, np.float32))
    return jnp.asarray(bd, BF16)


def kernel(x_prompt, x_sample, cache_attn_k, cache_attn_v, state_mlstm_C, state_mlstm_n, state_mlstm_m,
           state_mlstm_conv, norm1_g, w_in, att_qnorm_g, att_knorm_g, b_gates, mlstm_conv_w, mlstm_conv_b,
           mlstm_norm_g, w_out, norm2_g, peer_w_query, peer_qnorm_g, peer_subkeys, peer_u, peer_v):
    batch, seq = x_prompt.shape[:2]
    nsamp = x_sample.shape[0]
    past_len = 16384
    assert norm1_g.shape[0] == 1 and x_sample.shape[1] == 1 and seq % MLSTM_CHUNK == 0
    wb = cache_attn_k.shape[2]
    li = 0
    bd = _head_constants()
    tril, triu, gate_rep = _tri_constants()

    w = w_in[li]
    gate_lo, gate_hi = 3072, 3072 + 2 * H_MLSTM
    w_main = jnp.concatenate([w[:, :gate_lo], w[:, gate_hi:]], axis=1).astype(BF16)
    wg = jnp.pad(w[:, gate_lo:gate_hi], ((0, 0), (0, LANES - 2 * H_MLSTM)))
    wg_hi = wg.astype(BF16)
    wg_lo = (wg - wg_hi.astype(F32)).astype(BF16)
    g1 = norm1_g[li][None, :]
    qg = jnp.tile(att_qnorm_g[li], H_ATT)[None, :]
    kg = jnp.tile(att_knorm_g[li], H_ATT)[None, :]
    bg = jnp.pad(b_gates[li], (0, LANES - 2 * H_MLSTM))[None, :]
    conv_w = mlstm_conv_w[li]
    conv_b = mlstm_conv_b[li][None, :]
    ng = mlstm_norm_g[li][None, :]
    wo = w_out[li].astype(BF16)
    g2 = norm2_g[li][None, :]
    wq = peer_w_query[li].astype(BF16)
    pqg = peer_qnorm_g[li][None, :]
    sk = peer_subkeys[li].reshape(PEER_HEADS * 2, PEER_NKEYS, PEER_DKEY // 2).astype(BF16)
    u_b = peer_u[li].astype(BF16)
    vt_b = (0.5 * peer_v[li]).astype(BF16).T

    n_p = batch * seq
    xp2 = x_prompt.reshape(n_p, D_MODEL)
    pos_p = jnp.arange(seq, dtype=jnp.int32)
    q, k, v, mqk, mv, mo, gates, kt, vt = _in_projection(xp2, pos_p, 512, g1, w_main, wg_hi, wg_lo, qg, kg, bd,
                                                         transposed_kv=True)
    att_p = _attn_prompt(q, k, v, batch, seq)
    mh_p, c_p, n_p_state, m_p = _mlstm_prompt(mqk, mv, mo, gates, batch, seq, conv_w, conv_b, bg, ng, tril, triu,
                                                  gate_rep)
    n_all = n_p + nsamp
    group = LANES * SUBLANES
    n_pad = -(-n_all // group) * group
    x1_all, hn_all = _out_projection(xp2, att_p, mh_p.reshape(n_p, W_MLSTM), wo, g2, 256, n_pad, (0, n_pad))
    wbp = min(wb, seq)
    new_k_prompt = jnp.transpose(kt, (0, 3, 1, 2))[None, :, seq - wbp:]
    new_v_prompt = jnp.transpose(vt, (0, 3, 1, 2))[None, :, seq - wbp:]
    new_conv_prompt = mqk.reshape(batch, seq, -1)[None, :, seq - (CONV_W - 1):]

    xs2 = x_sample.reshape(nsamp, D_MODEL)
    pos_s = jnp.full((nsamp,), past_len, dtype=jnp.int32)
    qs, ks, vs, mqk_s, mv_s, mo_s, gates_s = _in_projection(xs2, pos_s, nsamp, g1, w_main, wg_hi, wg_lo, qg, kg, bd)
    cache_kt = jnp.transpose(cache_attn_k[li], (0, 2, 3, 1))
    cache_vt = jnp.transpose(cache_attn_v[li], (0, 2, 3, 1))
    att_s = _attn_sample(qs, ks, vs, cache_kt, cache_vt)
    mh_s, c_s, n_s, m_s = _mlstm_sample(mqk_s, mv_s, mo_s, gates_s, state_mlstm_conv[li], state_mlstm_C[li],
                                        state_mlstm_n[li], state_mlstm_m[li], conv_w, conv_b, bg, ng)
    x1_all, hn_all = _out_projection(xs2, att_s, mh_s, wo, g2, nsamp, n_pad, (n_p, n_all),
                                     into=(x1_all, hn_all))
    new_conv_sample = jnp.concatenate([state_mlstm_conv[li][:, 1:], mqk_s[:, None, :]], axis=1)[None]

    scores_t = _peer_scores(hn_all, wq, pqg, sk, 256)
    rank2, cnt1, e1, e2 = _peer_select(scores_t)
    sel = (rank2, e2, cnt1, e1)
    y_main = _peer_dense(hn_all, u_b, vt_b, *sel, x1_all, 0, n_p, PEER_TOK_TILE)
    n_tail = -(-nsamp // PEER_TAIL_TILE) * PEER_TAIL_TILE
    y_tail = _peer_dense(hn_all, u_b, vt_b, *sel, x1_all, n_p, n_tail, PEER_TAIL_TILE)
    y_prompt = y_main.reshape(batch, seq, D_MODEL)
    y_sample = y_tail[:nsamp].reshape(nsamp, 1, D_MODEL)

    return (y_prompt, y_sample, new_k_prompt, new_v_prompt,
            ks.reshape(1, nsamp, 1, H_ATT, HEAD_DIM), vs.reshape(1, nsamp, 1, H_ATT, HEAD_DIM),
            c_p[None], n_p_state[None], m_p[None, :, :, 0], new_conv_prompt,
            c_s[None], n_s[None], m_s[None], new_conv_sample)
```

```python
import functools

import numpy as np
import jax
import jax.numpy as jnp
from jax import lax
from jax.experimental import pallas as pl
from jax.experimental.pallas import tpu as pltpu

F32 = jnp.float32
BF16 = jnp.bfloat16

D_MODEL = 1024
HEAD_DIM = 64
W_ATT = 512
W_MLSTM = 512
H_ATT = 8
H_MLSTM = 8
DILATIONS = ((128, 1), (512, 4), (2048, 16))
ATT_BLOCK = 128
ROPE_THETA = 10000.0
CONV_W = 4
PEER_HEADS = 8
PEER_NKEYS = 128
PEER_DKEY = 256
PEER_TOPK = 16
EPS = 1e-6
NEG = -1e30

LANES = 128
SUBLANES = 8
VMEM_LIMIT = 56 * 1024 * 1024


def _cparams(*sem):
    return pltpu.CompilerParams(dimension_semantics=sem, vmem_limit_bytes=VMEM_LIMIT)


def _split2(x):
    hi = x.astype(BF16)
    lo = (x - hi.astype(F32)).astype(BF16)
    return hi, lo


def _split3(x):
    hi = x.astype(BF16)
    r = x - hi.astype(F32)
    mid = r.astype(BF16)
    lo = (r - mid.astype(F32)).astype(BF16)
    return hi, mid, lo


def _dot(a, b):
    return jnp.dot(a, b, preferred_element_type=F32)


def _dot_nt(a, b):
    return lax.dot_general(a, b, (((1,), (1,)), ((), ())), preferred_element_type=F32)


def _dot_parts(parts, b):
    acc = _dot(parts[0], b)
    for p in parts[1:]:
        acc = acc + _dot(p, b)
    return acc


def _sigmoid(x):
    return 1.0 / (1.0 + jnp.exp(-x))


def _log_sigmoid(x):
    return jnp.minimum(x, 0.0) - jnp.log1p(jnp.exp(-jnp.abs(x)))


def _inproj_kernel(x_ref, g_ref, w_ref, wgh_ref, wgl_ref, qg_ref, kg_ref, cos_ref, sin_ref, bd_ref,
                   q_ref, k_ref, v_ref, mqk_ref, mv_ref, mo_ref, gates_ref, kt_ref=None, vt_ref=None):
    x = x_ref[...]
    ms = jnp.mean(x * x, axis=-1, keepdims=True)
    xn = x * lax.rsqrt(ms + EPS) * g_ref[...]
    xh, xl = _split2(xn)

    def seg(lo, hi):
        return _dot(xh, w_ref[:, lo:hi])

    bd = bd_ref[...]
    cos = cos_ref[...]
    sin = sin_ref[...]
    lane = lax.broadcasted_iota(jnp.int32, cos.shape, 1)
    first_half = (lane % HEAD_DIM) < (HEAD_DIM // 2)

    def head_norm_rope(a, g):
        sq = a * a
        hi, lo = _split2(sq)
        msq = _dot(hi, bd) + _dot(lo, bd)
        y = a * lax.rsqrt(msq + EPS) * g
        rot = jnp.where(first_half, pltpu.roll(y, W_ATT - HEAD_DIM // 2, 1), pltpu.roll(y, HEAD_DIM // 2, 1))
        return y * cos + rot * sin

    q_ref[...] = head_norm_rope(seg(0, 512), qg_ref[...]) * (HEAD_DIM ** -0.5)
    k = head_norm_rope(seg(512, 1024), kg_ref[...])
    v = seg(1024, 1536)
    k_ref[...] = k
    v_ref[...] = v
    if kt_ref is not None:
        kt_ref[0] = k.T.reshape(H_ATT, HEAD_DIM, k.shape[0])
        vt_ref[0] = v.T.reshape(H_ATT, HEAD_DIM, v.shape[0])
    mqk_ref[...] = seg(1536, 2560)
    mv_ref[...] = seg(2560, 3072)
    mo_ref[...] = seg(3072, 3584)
    gates_ref[...] = _dot(xh, wgh_ref[...]) + _dot(xl, wgh_ref[...]) + _dot(xh, wgl_ref[...])


def _rope_tables(pos):
    half = HEAD_DIM // 2
    inv = ROPE_THETA ** (-jnp.arange(half, dtype=F32) / half)
    ang = pos.astype(F32)[:, None] * inv[None, :]
    cos = jnp.cos(ang)
    sin = jnp.sin(ang)
    cos_h = jnp.concatenate([cos, cos], axis=-1)
    sin_h = jnp.concatenate([-sin, sin], axis=-1)
    return jnp.tile(cos_h, (1, H_ATT)), jnp.tile(sin_h, (1, H_ATT))


def _in_projection(x2d, pos, tm, norm_g, w_main, wg_hi, wg_lo, qg, kg, bd, transposed_kv=False):
    n = x2d.shape[0]
    cos, sin = _rope_tables(pos)
    pblocks = pos.shape[0] // tm
    row = lambda i: (i, 0)
    fixed = lambda i: (0, 0)
    tab = lambda i: (i % pblocks, 0)
    widths = (512, 512, 512, 1024, 512, 512, LANES)
    out_specs = [pl.BlockSpec((tm, w), row) for w in widths]
    out_shape = [jax.ShapeDtypeStruct((n, w), F32) for w in widths]
    if transposed_kv:
        tshape = (n // pos.shape[0], H_ATT, HEAD_DIM, pos.shape[0])
        out_specs += [pl.BlockSpec((1, H_ATT, HEAD_DIM, tm), lambda i: (i // pblocks, 0, 0, i % pblocks))] * 2
        out_shape += [jax.ShapeDtypeStruct(tshape, F32)] * 2
    return pl.pallas_call(
        _inproj_kernel,
        grid=(n // tm,),
        in_specs=[
            pl.BlockSpec((tm, D_MODEL), row),
            pl.BlockSpec((1, D_MODEL), fixed),
            pl.BlockSpec(w_main.shape, fixed),
            pl.BlockSpec(wg_hi.shape, fixed),
            pl.BlockSpec(wg_lo.shape, fixed),
            pl.BlockSpec((1, W_ATT), fixed),
            pl.BlockSpec((1, W_ATT), fixed),
            pl.BlockSpec((tm, W_ATT), tab),
            pl.BlockSpec((tm, W_ATT), tab),
            pl.BlockSpec((W_ATT, W_ATT), fixed),
        ],
        out_specs=out_specs,
        out_shape=out_shape,
        compiler_params=_cparams("parallel"),
        name="in_projection",
    )(x2d, norm_g, w_main, wg_hi, wg_lo, qg, kg, cos, sin, bd)


def _merge_branches(outs, lses):
    m = functools.reduce(jnp.maximum, lses)
    es = [jnp.exp(l - m) for l in lses]
    return sum(e * o for e, o in zip(es, outs)) / sum(es)


ATT_GROUP = 8


def _attn_prompt_kernel(q_ref, k_ref, v_ref, att_ref, o_scr, l_scr):
    seq = q_ref.shape[1]
    blk = ATT_BLOCK
    lane = lax.broadcasted_iota(jnp.int32, (1, LANES), 1)
    ii = lax.broadcasted_iota(jnp.int32, (2 * blk, 2 * blk), 0) % blk
    jj = lax.broadcasted_iota(jnp.int32, (2 * blk, 2 * blk), 1)
    mask_prev = jnp.minimum(jj - ii, ii + blk - jj) >= 0
    mask_first = (lax.broadcasted_iota(jnp.int32, (2 * blk, blk), 1)
                  <= lax.broadcasted_iota(jnp.int32, (2 * blk, blk), 0) % blk)

    first_head = (lane // HEAD_DIM) == 0

    def attend(q, k2, v2, mask):
        q2 = jnp.concatenate([jnp.where(first_head, q, 0.0), jnp.where(first_head, 0.0, q)], axis=0).astype(BF16)
        s = jnp.where(mask, _dot_nt(q2, k2), NEG)
        m = jnp.max(s, axis=-1, keepdims=True)
        p = jnp.exp(s - m)
        l = jnp.sum(p, axis=-1, keepdims=True)
        o = _dot((p / l).astype(BF16), v2)
        lse = jnp.broadcast_to(m + jnp.log(l), o.shape)
        return jnp.where(first_head, o[:blk], o[blk:]), jnp.where(first_head, lse[:blk], lse[blk:])

    def group(branch, dil, starts, chained, first_has_prev):
        def rows(at):
            return pl.ds(at, blk, stride=dil) if dil > 1 else pl.ds(at, blk)

        qs = [q_ref[0, rows(at), :] for at in starts]
        ks = [k_ref[0, rows(at), :].astype(BF16) for at in starts]
        vs = [v_ref[0, rows(at), :].astype(BF16) for at in starts]
        k_before = v_before = None
        if chained and first_has_prev:
            before = starts[0] - blk * dil
            k_before = k_ref[0, rows(before), :].astype(BF16)
            v_before = v_ref[0, rows(before), :].astype(BF16)
        results = []
        for i in range(len(starts)):
            kp, vp = (k_before, v_before) if i == 0 else (ks[i - 1], vs[i - 1])
            if chained and kp is not None:
                results.append(attend(qs[i], jnp.concatenate([kp, ks[i]], axis=0),
                                      jnp.concatenate([vp, vs[i]], axis=0), mask_prev))
            else:
                results.append(attend(qs[i], ks[i], vs[i], mask_first))
        for at, (o_pair, l_pair) in zip(starts, results):
            o_scr[branch, rows(at), :] = o_pair
            l_scr[branch, rows(at), :] = l_pair

    for branch, (_, dil) in enumerate(DILATIONS):
        nblocks = seq // dil // blk
        step = blk * dil
        if nblocks == 1:
            def classes(g, carry, branch=branch, dil=dil):
                group(branch, dil, [g * ATT_GROUP + i for i in range(ATT_GROUP)], False, False)
                return carry

            lax.fori_loop(0, dil // ATT_GROUP, classes, 0)
        else:
            per_group = min(ATT_GROUP, nblocks)
            assert nblocks % per_group == 0

            def residue(r, carry, branch=branch, dil=dil, nblocks=nblocks, step=step, per_group=per_group):
                group(branch, dil, [r + i * step for i in range(per_group)], True, False)

                def later(g, c):
                    base = g * (per_group * step)
                    base = pl.multiple_of(base, blk) if dil == 1 else r + base
                    group(branch, dil, [base + i * step for i in range(per_group)], True, True)
                    return c

                if nblocks > per_group:
                    lax.fori_loop(1, nblocks // per_group, later, 0)
                return carry

            if dil == 1:
                residue(0, 0)
            else:
                lax.fori_loop(0, dil, residue, 0)

    nbr = len(DILATIONS)
    att_ref[0] = _merge_branches([o_scr[i] for i in range(nbr)], [l_scr[i] for i in range(nbr)])


def _attn_prompt(q, k, v, batch, seq):
    shape3 = (batch, seq, W_ATT)
    blk = pl.BlockSpec((1, seq, LANES), lambda b, p: (b, 0, p))
    att = pl.pallas_call(
        _attn_prompt_kernel,
        grid=(batch, W_ATT // LANES),
        in_specs=[blk, blk, blk],
        out_specs=blk,
        out_shape=jax.ShapeDtypeStruct(shape3, F32),
        scratch_shapes=[pltpu.VMEM((len(DILATIONS), seq, LANES), F32)] * 2,
        compiler_params=_cparams("parallel", "parallel"),
        name="attn_prompt",
    )(q.reshape(shape3), k.reshape(shape3), v.reshape(shape3))
    return att.reshape(batch * seq, W_ATT)


def _attn_sample_kernel(q_ref, kn_ref, vn_ref, kt_ref, vt_ref, att_ref):
    q = q_ref[0]
    kn = kn_ref[0]
    vn = vn_ref[0]
    kt = kt_ref[0]
    vt = vt_ref[0]
    wb = kt.shape[-1]
    scores = jnp.sum(kt * q, axis=1, keepdims=True)
    s_new = jnp.sum(q * kn, axis=1, keepdims=True)
    row = lax.broadcasted_iota(jnp.int32, (1, 1, wb), 2)
    outs, lses = [], []
    for _, dil in DILATIONS:
        on_grid = jnp.where(row % dil == 0, row, -1)
        s = jnp.where(on_grid >= wb - ATT_BLOCK * dil, scores, NEG)
        m = jnp.maximum(jnp.max(s, axis=2, keepdims=True), s_new)
        p = jnp.exp(s - m)
        p_new = jnp.exp(s_new - m)
        l = jnp.sum(p, axis=2, keepdims=True) + p_new
        outs.append((jnp.sum(vt * p, axis=2, keepdims=True) + p_new * vn) / l)
        lses.append(m + jnp.log(l))
    att_ref[0] = _merge_branches(outs, lses)


def _attn_sample(q, k_new, v_new, cache_kt, cache_vt):
    nb, wb = cache_kt.shape[0], cache_kt.shape[-1]
    assert all(wb % dil == 0 and wb >= ATT_BLOCK * dil for _, dil in DILATIONS)
    col = (nb, H_ATT, HEAD_DIM, 1)
    col_spec = pl.BlockSpec((1, H_ATT, HEAD_DIM, 1), lambda b: (b, 0, 0, 0))
    cache_spec = pl.BlockSpec((1, H_ATT, HEAD_DIM, wb), lambda b: (b, 0, 0, 0))
    att = pl.pallas_call(
        _attn_sample_kernel,
        grid=(nb,),
        in_specs=[col_spec] * 3 + [cache_spec] * 2,
        out_specs=col_spec,
        out_shape=jax.ShapeDtypeStruct(col, F32),
        compiler_params=_cparams("parallel"),
        name="attn_sample",
    )(q.reshape(col), k_new.reshape(col), v_new.reshape(col), cache_kt, cache_vt)
    return att.reshape(nb, W_ATT)


MLSTM_CHUNK = 128


MLSTM_SEQS = 2


def _mlstm_prompt_kernel(mqk_ref, mv_ref, mo_ref, gates_ref, cw_ref, cb_ref, bg_ref, ng_ref, tril_ref, triu_ref,
                         rep_ref, mh_ref, c_out_ref, n_out_ref, m_out_ref,
                         conv_scr, c_scr, n_scr, m_scr):
    @pl.when(pl.program_id(1) == 0)
    def _():
        conv_scr[:, 0:SUBLANES, :] = jnp.zeros((MLSTM_SEQS, SUBLANES, 2 * W_MLSTM), F32)
        c_scr[...] = jnp.zeros(c_scr.shape, F32)
        n_scr[...] = jnp.zeros(n_scr.shape, F32)
        m_scr[...] = jnp.zeros(m_scr.shape, F32)

    for i in range(MLSTM_SEQS):
        _mlstm_chunk(mqk_ref.at[i], mv_ref.at[i], mo_ref.at[i], gates_ref.at[i], cw_ref, cb_ref, bg_ref, ng_ref,
                     tril_ref, triu_ref, rep_ref, mh_ref.at[i], c_out_ref.at[i], n_out_ref.at[i], m_out_ref.at[i],
                     conv_scr.at[i], c_scr.at[i], n_scr.at[i], m_scr.at[i])


def _mlstm_chunk(mqk_ref, mv_ref, mo_ref, gates_ref, cw_ref, cb_ref, bg_ref, ng_ref, tril_ref, triu_ref,
                 rep_ref, mh_ref, c_out_ref, n_out_ref, m_out_ref, conv_scr, c_scr, n_scr, m_scr):
    L = MLSTM_CHUNK
    x = mqk_ref[...]
    conv_scr[SUBLANES:SUBLANES + L, :] = x
    conv = cb_ref[...] + cw_ref[CONV_W - 1:CONV_W, :] * x
    for j in range(CONV_W - 1):
        shift = CONV_W - 1 - j
        conv = conv + cw_ref[j:j + 1, :] * conv_scr[SUBLANES - shift:SUBLANES - shift + L, :]
    conv_scr[0:SUBLANES, :] = x[L - SUBLANES:L, :]
    qk = conv * _sigmoid(conv)
    q_all = qk[:, :W_MLSTM]
    k_all = qk[:, W_MLSTM:] * (HEAD_DIM ** -0.5)
    v_all = mv_ref[...]
    vt_all = v_all.T
    mo = mo_ref[...]

    assert L == LANES
    gb = gates_ref[...] + bg_ref[...]
    gbt = gb.T
    g_rep = _dot_parts(_split3(gb), rep_ref[...])
    ig_rep = g_rep[:, :H_MLSTM * LANES]
    lf_rep = _log_sigmoid(g_rep[:, H_MLSTM * LANES:])
    lf_row = _log_sigmoid(gbt[H_MLSTM:2 * H_MLSTM, :])
    tril = tril_ref[...]
    b_rep = _dot(tril, lf_rep.astype(BF16))
    rest = lf_rep - lf_rep.astype(BF16).astype(F32)
    b_rep = b_rep + _dot(tril, rest.astype(BF16))
    b_rep = b_rep + _dot(tril, (rest - rest.astype(BF16).astype(F32)).astype(BF16))
    b_row_all = _dot_parts(_split3(lf_row), triu_ref[...])

    ti = lax.broadcasted_iota(jnp.int32, (L, L), 0)
    si = lax.broadcasted_iota(jnp.int32, (L, L), 1)
    causal = si <= ti

    for h in range(H_MLSTM):
        hs = slice(h * HEAD_DIM, (h + 1) * HEAD_DIM)
        rep = slice(h * LANES, (h + 1) * LANES)
        qh = q_all[:, hs]
        kh = k_all[:, hs]
        vh = v_all[:, hs]
        bc = b_rep[:, rep]
        igc = ig_rep[:, rep]
        br = b_row_all[h:h + 1, :]
        igr = gbt[h:h + 1, :]
        m_prev = m_scr[h:h + 1, :]
        c_prev = c_scr[h]
        n_prev = n_scr[h:h + 1, :]

        log_d = jnp.where(causal, bc - br + igr, NEG)
        m_inter = bc + m_prev
        m_t = jnp.maximum(m_inter, jnp.max(log_d, axis=-1, keepdims=True))
        qh_b = qh.astype(BF16)
        kh_b = kh.astype(BF16)
        sd = _dot_nt(qh_b, kh_b) * jnp.exp(log_d - m_t)
        scale_inter = jnp.exp(m_inter - m_t)
        num = (scale_inter[:, :HEAD_DIM] * _dot_nt(qh_b, c_prev.astype(BF16))
               + _dot(sd.astype(BF16), vh.astype(BF16)))
        den = scale_inter * jnp.sum(qh * n_prev, axis=-1, keepdims=True) + jnp.sum(sd, axis=-1, keepdims=True)
        hh = num / jnp.maximum(jnp.abs(den), jnp.exp(-m_t))[:, :HEAD_DIM]

        m_new = m_t[L - 1:L, :]
        b_last = bc[L - 1:L, :]
        decay = jnp.exp(b_last + m_prev - m_new)
        w_row = jnp.exp(b_last - br + igr - m_new)
        w_col = jnp.exp(b_last - bc + igc - m_new)
        vtw = (vt_all[hs, :] * w_row).astype(BF16)
        c_new = decay[:, :HEAD_DIM] * c_prev + _dot(vtw, kh_b)
        n_new = decay[:, :HEAD_DIM] * n_prev + jnp.sum(w_col[:, :HEAD_DIM] * kh, axis=0, keepdims=True)
        c_scr[h] = c_new
        n_scr[h:h + 1, :] = n_new
        m_scr[h:h + 1, :] = m_new
        c_out_ref[h] = c_new
        n_out_ref[h:h + 1, :] = n_new
        m_out_ref[h:h + 1, :] = m_new

        y = hh * lax.rsqrt(jnp.mean(hh * hh, axis=-1, keepdims=True) + EPS) * ng_ref[:, hs]
        mh_ref[:, hs] = _sigmoid(mo[:, hs]) * y


def _mlstm_prompt(mqk, mv, mo, gates, batch, seq, conv_w, conv_b, bg, ng, tril, triu, rep):
    L = MLSTM_CHUNK
    nseq = MLSTM_SEQS
    assert batch % nseq == 0
    nchunk = seq // L
    chunk = lambda b, c: (b, c, 0)
    fixed = lambda b, c: (0, 0)
    return pl.pallas_call(
        _mlstm_prompt_kernel,
        grid=(batch // nseq, nchunk),
        in_specs=[
            pl.BlockSpec((nseq, L, 2 * W_MLSTM), chunk),
            pl.BlockSpec((nseq, L, W_MLSTM), chunk),
            pl.BlockSpec((nseq, L, W_MLSTM), chunk),
            pl.BlockSpec((nseq, L, LANES), chunk),
            pl.BlockSpec((CONV_W, 2 * W_MLSTM), fixed),
            pl.BlockSpec((1, 2 * W_MLSTM), fixed),
            pl.BlockSpec((1, LANES), fixed),
            pl.BlockSpec((1, W_MLSTM), fixed),
            pl.BlockSpec((L, L), fixed),
            pl.BlockSpec((L, L), fixed),
            pl.BlockSpec(rep.shape, fixed),
        ],
        out_specs=[
            pl.BlockSpec((nseq, L, W_MLSTM), chunk),
            pl.BlockSpec((nseq, H_MLSTM, HEAD_DIM, HEAD_DIM), lambda b, c: (b, 0, 0, 0)),
            pl.BlockSpec((nseq, H_MLSTM, HEAD_DIM), lambda b, c: (b, 0, 0)),
            pl.BlockSpec((nseq, H_MLSTM, LANES), lambda b, c: (b, 0, 0)),
        ],
        out_shape=[
            jax.ShapeDtypeStruct((batch, seq, W_MLSTM), F32),
            jax.ShapeDtypeStruct((batch, H_MLSTM, HEAD_DIM, HEAD_DIM), F32),
            jax.ShapeDtypeStruct((batch, H_MLSTM, HEAD_DIM), F32),
            jax.ShapeDtypeStruct((batch, H_MLSTM, LANES), F32),
        ],
        scratch_shapes=[
            pltpu.VMEM((nseq, L + SUBLANES, 2 * W_MLSTM), F32),
            pltpu.VMEM((nseq, H_MLSTM, HEAD_DIM, HEAD_DIM), F32),
            pltpu.VMEM((nseq, H_MLSTM, HEAD_DIM), F32),
            pltpu.VMEM((nseq, H_MLSTM, LANES), F32),
        ],
        compiler_params=_cparams("parallel", "arbitrary"),
        name="mlstm_prompt",
    )(mqk.reshape(batch, seq, -1), mv.reshape(batch, seq, -1), mo.reshape(batch, seq, -1),
      gates.reshape(batch, seq, -1), conv_w, conv_b, bg, ng, tril, triu, rep)


def _mlstm_sample_pre_kernel(mqk_ref, buf_ref, gates_ref, cw_ref, cb_ref, bg_ref, qk_ref, g_ref):
    conv = cb_ref[...] + cw_ref[CONV_W - 1:CONV_W, :] * mqk_ref[...]
    for j in range(CONV_W - 1):
        conv = conv + cw_ref[j:j + 1, :] * buf_ref[j]
    qk = conv * _sigmoid(conv)
    lane = lax.broadcasted_iota(jnp.int32, qk.shape, 1)
    qk_ref[...] = jnp.where(lane < W_MLSTM, qk, qk * (HEAD_DIM ** -0.5))
    gb = gates_ref[...] + bg_ref[...]
    glane = lax.broadcasted_iota(jnp.int32, gb.shape, 1)
    g_ref[...] = jnp.where(glane < H_MLSTM, gb, _log_sigmoid(gb))


def _mlstm_sample_step_kernel(q_ref, k_ref, v_ref, mo_ref, ig_ref, lf_ref, c0_ref, n0_ref, m0_ref, ng_ref,
                              mh_ref, c_ref, n_ref, m_ref):
    for h in range(H_MLSTM):
        q = q_ref[0, h]
        k = k_ref[0, h]
        v = v_ref[0, h]
        ig = ig_ref[0, h]
        lf = lf_ref[0, h]
        c0 = c0_ref[0, h]
        n0 = n0_ref[0, h]
        m0 = m0_ref[0, h]
        m_inter = lf + m0
        m_t = jnp.maximum(m_inter, ig)
        w_in = jnp.exp(ig - m_t)
        sd = jnp.sum(q * k, axis=-1, keepdims=True) * w_in
        scale_inter = jnp.exp(m_inter - m_t)
        num = scale_inter * jnp.sum(c0 * q, axis=-1, keepdims=True) + sd * v
        den = scale_inter * jnp.sum(n0 * q, axis=-1, keepdims=True) + sd
        hh = num / jnp.maximum(jnp.abs(den), jnp.exp(-m_t))
        decay = jnp.exp(lf + m0 - m_t)
        c_ref[0, h] = decay * c0 + w_in * (v * k)
        n_ref[0, h] = decay * n0 + w_in * k
        m_ref[0, h] = m_t
        y = hh * lax.rsqrt(jnp.mean(hh * hh, axis=0, keepdims=True) + EPS) * ng_ref[h]
        mh_ref[0, h] = _sigmoid(mo_ref[0, h]) * y


def _mlstm_sample(mqk, mv, mo, gates, conv_buf, c0, n0, m0, conv_w, conv_b, bg, ng):
    nb = mqk.shape[0]
    full = lambda *shape: pl.BlockSpec(shape, lambda: (0,) * len(shape))
    qk, g = pl.pallas_call(
        _mlstm_sample_pre_kernel,
        in_specs=[full(nb, 2 * W_MLSTM), full(CONV_W - 1, nb, 2 * W_MLSTM), full(nb, LANES),
                  full(CONV_W, 2 * W_MLSTM), full(1, 2 * W_MLSTM), full(1, LANES)],
        out_specs=[full(nb, 2 * W_MLSTM), full(nb, LANES)],
        out_shape=[jax.ShapeDtypeStruct((nb, 2 * W_MLSTM), F32), jax.ShapeDtypeStruct((nb, LANES), F32)],
        name="mlstm_sample_pre",
    )(mqk, jnp.swapaxes(conv_buf, 0, 1), gates, conv_w, conv_b, bg)
    row = (nb, H_MLSTM, 1, HEAD_DIM)
    col = (nb, H_MLSTM, HEAD_DIM, 1)
    one = (nb, H_MLSTM, 1, 1)
    mat = (nb, H_MLSTM, HEAD_DIM, HEAD_DIM)
    spec = lambda shape: pl.BlockSpec((1,) + shape[1:], lambda b: (b, 0, 0, 0))
    ins = [
        (qk[:, :W_MLSTM].reshape(row), row), (qk[:, W_MLSTM:].reshape(row), row), (mv.reshape(col), col),
        (mo.reshape(col), col), (g[:, :H_MLSTM].reshape(one), one), (g[:, H_MLSTM:2 * H_MLSTM].reshape(one), one),
        (c0, mat), (n0.reshape(row), row), (m0.reshape(one), one),
    ]
    mh, c, n, m = pl.pallas_call(
        _mlstm_sample_step_kernel,
        grid=(nb,),
        in_specs=[spec(s) for _, s in ins] + [pl.BlockSpec((H_MLSTM, HEAD_DIM, 1), lambda b: (0, 0, 0))],
        out_specs=[spec(col), spec(mat), spec(row), spec(one)],
        out_shape=[jax.ShapeDtypeStruct(s, F32) for s in (col, mat, row, one)],
        compiler_params=_cparams("parallel"),
        name="mlstm_sample_step",
    )(*[a for a, _ in ins], ng.reshape(H_MLSTM, HEAD_DIM, 1))
    return mh.reshape(nb, W_MLSTM), c, n.reshape(nb, H_MLSTM, HEAD_DIM), m.reshape(nb, H_MLSTM)


def _outproj_kernel(x_ref, att_ref, mh_ref, w_ref, g_ref, x1_ref, hn_ref):
    x1 = (x_ref[...] + _dot(att_ref[...].astype(BF16), w_ref[0:W_ATT, :])
          + _dot(mh_ref[...].astype(BF16), w_ref[W_ATT:, :]))
    x1_ref[...] = x1
    ms = jnp.mean(x1 * x1, axis=-1, keepdims=True)
    hn_ref[...] = (x1 * lax.rsqrt(ms + EPS) * g_ref[...]).astype(BF16)


def _outproj_into_kernel(x1_any, hn_any, *refs):
    del x1_any, hn_any
    _outproj_kernel(*refs)


def _out_projection(x2d, att, mh, w_out, g2, tm, n_out, out_rows, into=None):
    n_in = x2d.shape[0]
    in_blocks = n_in // tm
    first = out_rows[0] // tm
    row = lambda i: (i % in_blocks, 0)
    out_row = lambda i: (first + i, 0)
    fixed = lambda i: (0, 0)
    half = pl.BlockSpec((tm, W_ATT), row)
    in_specs = [pl.BlockSpec((tm, D_MODEL), row), half, half,
                pl.BlockSpec((D_MODEL, D_MODEL), fixed), pl.BlockSpec((1, D_MODEL), fixed)]
    args = (x2d, att, mh, w_out, g2)
    body = _outproj_kernel
    aliases = {}
    if into is not None:
        in_specs = [pl.BlockSpec(memory_space=pl.ANY)] * 2 + in_specs
        args = (*into, *args)
        body = _outproj_into_kernel
        aliases = {0: 0, 1: 1}
    return pl.pallas_call(
        body,
        grid=((out_rows[1] - out_rows[0]) // tm,),
        in_specs=in_specs,
        out_specs=[pl.BlockSpec((tm, D_MODEL), out_row), pl.BlockSpec((tm, D_MODEL), out_row)],
        out_shape=[jax.ShapeDtypeStruct((n_out, D_MODEL), F32), jax.ShapeDtypeStruct((n_out, D_MODEL), BF16)],
        input_output_aliases=aliases,
        compiler_params=_cparams("parallel"),
        name="out_projection",
    )(*args)


def _peer_scores_kernel(hn_ref, wq_ref, g_ref, sk_ref, s_ref):
    q = _dot(hn_ref[...], wq_ref[...])
    half = PEER_DKEY // 2
    for h in range(PEER_HEADS):
        qh = q[:, h * PEER_DKEY:(h + 1) * PEER_DKEY]
        qn = qh * lax.rsqrt(jnp.mean(qh * qh, axis=-1, keepdims=True) + EPS) * g_ref[...]
        for part in range(2):
            s = _dot_nt(qn[:, part * half:(part + 1) * half].astype(BF16), sk_ref[2 * h + part])
            r0 = (2 * h + part) * PEER_NKEYS
            s_ref[r0:r0 + PEER_NKEYS, :] = s.T


def _peer_scores(hn, wq, g, sk, tm):
    n = hn.shape[0]
    rows = PEER_HEADS * 2 * PEER_NKEYS
    return pl.pallas_call(
        _peer_scores_kernel,
        grid=(n // tm,),
        in_specs=[pl.BlockSpec((tm, D_MODEL), lambda i: (i, 0)),
                  pl.BlockSpec(wq.shape, lambda i: (0, 0)),
                  pl.BlockSpec((1, PEER_DKEY), lambda i: (0, 0)),
                  pl.BlockSpec(sk.shape, lambda i: (0, 0, 0))],
        out_specs=pl.BlockSpec((rows, tm), lambda i: (0, i)),
        out_shape=jax.ShapeDtypeStruct((rows, n), F32),
        compiler_params=_cparams("parallel"),
        name="peer_scores",
    )(hn, wq, g, sk)


_STAIR = [(a, b) for a in range(PEER_TOPK) for b in range(PEER_TOPK) if (a + 1) * (b + 1) <= PEER_TOPK]


def _peer_select_kernel(s1_ref, s2_ref, rank2_ref, cnt1_ref, e1_ref, e2_ref,
                        work1_ref, work2_ref, rank1_ref, rank2s_ref, vals1_ref, vals2_ref,
                        vals1t_ref, vals2t_ref, cntt_ref, cnt_ref, zt_ref, zrow_ref):
    shape = rank1_ref.shape
    keyf = lax.broadcasted_iota(jnp.int32, shape, 0).astype(F32)
    chunks = [slice(c * LANES, (c + 1) * LANES) for c in range(SUBLANES)]
    halves = ((s1_ref, work1_ref, rank1_ref, vals1_ref), (s2_ref, work2_ref, rank2s_ref, vals2_ref))

    def top16_distinct():
        rank2s_ref[...] = jnp.full(shape, float(PEER_TOPK), F32)

        def one_round(a, previous):
            maxima = []
            for (src_ref, _, rank_ref, vals_ref), prev in zip(halves, previous):
                w = src_ref[...]
                if rank_ref is rank2s_ref:
                    rank_ref[...] = jnp.where(w == prev, jnp.asarray(a - 1, F32), rank_ref[...])
                mx = jnp.max(jnp.where(w < prev, w, -jnp.inf), axis=0, keepdims=True)
                vals_ref[pl.ds(a, 1), :] = mx
                maxima.append(mx)
            return tuple(maxima)

        top = jnp.full((1, shape[1]), jnp.inf, F32)
        last = lax.fori_loop(0, PEER_TOPK, one_round, (top, top))
        rank2s_ref[...] = jnp.where(s2_ref[...] == last[1], float(PEER_TOPK - 1), rank2s_ref[...])

    def top16_ties():
        for src_ref, work_ref, rank_ref, _ in halves:
            work_ref[...] = src_ref[...]
            rank_ref[...] = jnp.full(shape, float(PEER_TOPK), F32)

        def one_round(a, carry):
            for _, work_ref, rank_ref, vals_ref in halves:
                w = work_ref[...]
                mx = jnp.max(w, axis=0, keepdims=True)
                sel = keyf == jnp.min(jnp.where(w == mx, keyf, float(PEER_NKEYS)), axis=0, keepdims=True)
                work_ref[...] = jnp.where(sel, -jnp.inf, w)
                rank_ref[...] = jnp.where(sel, jnp.asarray(a, F32), rank_ref[...])
                vals_ref[pl.ds(a, 1), :] = mx
            return carry

        lax.fori_loop(0, PEER_TOPK, one_round, 0)

    top16_distinct()
    taken = [jnp.sum(jnp.where(src_ref[...] >= vals_ref[PEER_TOPK - 1:PEER_TOPK, :], 1.0, 0.0), axis=0, keepdims=True)
             for src_ref, _, _, vals_ref in halves]
    ties = jnp.max(jnp.maximum(jnp.abs(taken[0] - PEER_TOPK), jnp.abs(taken[1] - PEER_TOPK))) > 0.0

    @pl.when(ties)
    def _():
        top16_ties()

    for c, cs in enumerate(chunks):
        vals1t_ref[:, c, :] = vals1_ref[:, cs]
        vals2t_ref[:, c, :] = vals2_ref[:, cs]
    v1 = [vals1t_ref[a] for a in range(PEER_TOPK)]
    v2 = [vals2t_ref[b] for b in range(PEER_TOPK)]
    cand = [v1[a] + v2[b] for a, b in _STAIR]
    pos = []
    for i, (a, b) in enumerate(_STAIR):
        static = sum(1 for (a2, b2) in _STAIR if a2 <= a and b2 <= b and (a2, b2) != (a, b))
        pos.append(jnp.full(cand[0].shape, float(static), F32))
    for i, (ai, bi) in enumerate(_STAIR):
        for j in range(i + 1, len(_STAIR)):
            aj, bj = _STAIR[j]
            if ai <= aj and bi <= bj:
                continue
            i_first = cand[i] >= cand[j]
            pos[j] = pos[j] + jnp.where(i_first, 1.0, 0.0)
            pos[i] = pos[i] + jnp.where(i_first, 0.0, 1.0)
    e1s = [jnp.exp(v1[a] - v1[0]) for a in range(PEER_TOPK)]
    e2s = [jnp.exp(v2[b] - v2[0]) for b in range(PEER_TOPK)]
    z = jnp.zeros_like(cand[0])
    for a in range(PEER_TOPK):
        cnt_a = jnp.zeros_like(z)
        za = jnp.zeros_like(z)
        for i, (a2, b) in enumerate(_STAIR):
            if a2 != a:
                continue
            chosen = pos[i] < float(PEER_TOPK)
            cnt_a = cnt_a + jnp.where(chosen, 1.0, 0.0)
            za = za + jnp.where(chosen, e2s[b], 0.0)
        cntt_ref[a] = cnt_a
        z = z + e1s[a] * za
    zt_ref[0] = 1.0 / z
    for c, cs in enumerate(chunks):
        cnt_ref[:, cs] = cntt_ref[:, c, :]
        zrow_ref[:, cs] = zt_ref[:, c, :]

    cnt1_ref[...] = jnp.zeros(shape, F32)

    def spread_by_rank(a, carry):
        cnt1_ref[...] = jnp.where(rank1_ref[...] == jnp.asarray(a, F32), cnt_ref[pl.ds(a, 1), :], cnt1_ref[...])
        return carry

    def spread_by_value(a, carry):
        cnt1_ref[...] = jnp.where(s1_ref[...] == vals1_ref[pl.ds(a, 1), :], cnt_ref[pl.ds(a, 1), :], cnt1_ref[...])
        return carry

    @pl.when(ties)
    def _():
        lax.fori_loop(0, PEER_TOPK, spread_by_rank, 0)

    @pl.when(jnp.logical_not(ties))
    def _():
        lax.fori_loop(0, PEER_TOPK, spread_by_value, 0)

    rank2_ref[...] = rank2s_ref[...].astype(BF16)
    e1_ref[...] = jnp.exp(s1_ref[...] - vals1_ref[0:1, :]) * zrow_ref[...]
    e2_ref[...] = jnp.exp(s2_ref[...] - vals2_ref[0:1, :]).astype(BF16)


def _peer_select(scores_t):
    n = scores_t.shape[1]
    tok = LANES * SUBLANES
    blk = (PEER_NKEYS, tok)
    out_rows = PEER_HEADS * PEER_NKEYS
    vm = lambda *shape: pltpu.VMEM(shape, F32)
    return pl.pallas_call(
        _peer_select_kernel,
        grid=(PEER_HEADS, n // tok),
        in_specs=[pl.BlockSpec(blk, lambda h, g: (2 * h, g)), pl.BlockSpec(blk, lambda h, g: (2 * h + 1, g))],
        out_specs=[pl.BlockSpec(blk, lambda h, g: (h, g))] * 4,
        out_shape=[jax.ShapeDtypeStruct((out_rows, n), dt) for dt in (BF16, F32, F32, BF16)],
        scratch_shapes=[vm(*blk)] * 4 + [vm(PEER_TOPK, tok)] * 2 + [vm(PEER_TOPK, SUBLANES, LANES)] * 3
                       + [vm(PEER_TOPK, tok), vm(1, SUBLANES, LANES), vm(1, tok)],
        compiler_params=_cparams("parallel", "parallel"),
        name="peer_select",
    )(scores_t, scores_t)


PEER_TOK_TILE = 512
PEER_TAIL_TILE = 128
PEER_I1_PER_STEP = 16
PEER_MXU_CHUNKS = 2
BF16_ROWS = 16


def _peer_dense_kernel(nblk, hn_ref, u_ref, vt_prev_ref, vt_last_ref, rank2_ref, e2_ref, cnt1_ref, e1_ref,
                       x1_ref, y_ref, ht_ref, p_even_ref, p_odd_ref, acc_ref):
    j = pl.program_id(1)
    tokens = hn_ref.shape[0]
    reps = PEER_NKEYS // BF16_ROWS
    assert PEER_I1_PER_STEP % SUBLANES == 0 and PEER_I1_PER_STEP % PEER_MXU_CHUNKS == 0

    def row_bf16(tile, s):
        x8 = jnp.broadcast_to(tile[s:s + 1, :], (SUBLANES, tokens))
        x16 = jnp.concatenate([x8, x8], axis=0).astype(BF16)
        return jnp.concatenate([x16] * reps, axis=0)

    def evaluate(p_write, p_read):
        first = pl.multiple_of(j * PEER_I1_PER_STEP, SUBLANES)
        cnt_tiles = [cnt1_ref[pl.ds(h * PEER_NKEYS + first, PEER_I1_PER_STEP), :] for h in range(PEER_HEADS)]
        e1_tiles = [e1_ref[pl.ds(h * PEER_NKEYS + first, PEER_I1_PER_STEP), :] for h in range(PEER_HEADS)]
        per_chunk = PEER_I1_PER_STEP // PEER_MXU_CHUNKS
        crow = per_chunk * PEER_NKEYS
        drow = D_MODEL // PEER_MXU_CHUNKS
        for c in range(PEER_MXU_CHUNKS):
            rc = slice(c * crow, (c + 1) * crow)
            rd = slice(c * drow, (c + 1) * drow)
            a_c = _dot(u_ref[rc, :], ht_ref[...])
            if p_read is not None:
                acc_ref[rd, :] += _dot(vt_prev_ref[rd, :], p_read[...])
            for sc in range(per_chunk):
                s = c * per_chunk + sc
                g = None
                for h in range(PEER_HEADS):
                    rows = slice(h * PEER_NKEYS, (h + 1) * PEER_NKEYS)
                    w = jnp.where(rank2_ref[rows, :] < row_bf16(cnt_tiles[h], s),
                                  e2_ref[rows, :] * row_bf16(e1_tiles[h], s), jnp.zeros((), BF16))
                    g = w if g is None else g + w
                a = a_c[sc * PEER_NKEYS:(sc + 1) * PEER_NKEYS, :]
                act = a + a * lax.erf(a * (2.0 ** -0.5))
                p_write[s * PEER_NKEYS:(s + 1) * PEER_NKEYS, :] = g * act.astype(BF16)

    @pl.when(j == 0)
    def _():
        acc_ref[...] = jnp.zeros_like(acc_ref)
        ht_ref[...] = hn_ref[...].astype(F32).T.astype(BF16)
        evaluate(p_even_ref, None)

    @pl.when(j % 2 == 1)
    def _():
        evaluate(p_odd_ref, p_even_ref)

    @pl.when(jnp.logical_and(j % 2 == 0, j > 0))
    def _():
        evaluate(p_even_ref, p_odd_ref)

    @pl.when(j == nblk - 1)
    def _():
        p_last = p_odd_ref if nblk % 2 == 0 else p_even_ref
        acc = acc_ref[...] + _dot(vt_last_ref[...], p_last[...])
        y_ref[...] = x1_ref[...] + acc.T


def _peer_dense(hn, u, vt, rank2, e2, cnt1, e1, x1, first_row, nrows, T):
    assert first_row % T == 0 and nrows % T == 0
    eb = PEER_I1_PER_STEP * PEER_NKEYS
    nexp = u.shape[0]
    sel_rows = PEER_HEADS * PEER_NKEYS
    nblk = nexp // eb
    t0 = first_row // T
    rows = lambda t, j: (t0 + t, 0)
    tok = lambda t, j: (0, t0 + t)
    return pl.pallas_call(
        functools.partial(_peer_dense_kernel, nblk),
        grid=(nrows // T, nblk),
        in_specs=[
            pl.BlockSpec((T, D_MODEL), rows),
            pl.BlockSpec((eb, D_MODEL), lambda t, j: (j, 0)),
            pl.BlockSpec((D_MODEL, eb), lambda t, j: (0, jnp.maximum(j - 1, 0))),
            pl.BlockSpec((D_MODEL, eb), lambda t, j: (0, nblk - 1)),
            pl.BlockSpec((sel_rows, T), tok),
            pl.BlockSpec((sel_rows, T), tok),
            pl.BlockSpec((sel_rows, T), tok),
            pl.BlockSpec((sel_rows, T), tok),
            pl.BlockSpec((T, D_MODEL), rows),
        ],
        out_specs=pl.BlockSpec((T, D_MODEL), lambda t, j: (t, 0)),
        out_shape=jax.ShapeDtypeStruct((nrows, D_MODEL), F32),
        scratch_shapes=[pltpu.VMEM((D_MODEL, T), BF16), pltpu.VMEM((eb, T), BF16), pltpu.VMEM((eb, T), BF16),
                        pltpu.VMEM((D_MODEL, T), F32)],
        compiler_params=_cparams("parallel", "arbitrary"),
        name="peer_dense",
    )(hn, u, vt, vt, rank2, e2, cnt1, e1, x1)


def _tri_constants():
    L = MLSTM_CHUNK
    tril = np.tril(np.ones((L, L), np.float32))
    rep = (np.arange(LANES)[:, None] == np.arange(2 * H_MLSTM * LANES)[None, :] // LANES).astype(np.float32)
    return jnp.asarray(tril, BF16), jnp.asarray(tril.T, BF16), jnp.asarray(rep, BF16)


def _head_constants():
    bd = np.kron(np.eye(H_ATT, dtype=np.float32), np.full((HEAD_DIM, HEAD_DIM), 1.0 / HEAD_DIM, np.float32))
    return jnp.asarray(bd, BF16)


def kernel(x_prompt, x_sample, cache_attn_k, cache_attn_v, state_mlstm_C, state_mlstm_n, state_mlstm_m,
           state_mlstm_conv, norm1_g, w_in, att_qnorm_g, att_knorm_g, b_gates, mlstm_conv_w, mlstm_conv_b,
           mlstm_norm_g, w_out, norm2_g, peer_w_query, peer_qnorm_g, peer_subkeys, peer_u, peer_v):
    batch, seq = x_prompt.shape[:2]
    nsamp = x_sample.shape[0]
    past_len = 16384
    assert norm1_g.shape[0] == 1 and x_sample.shape[1] == 1 and seq % MLSTM_CHUNK == 0
    wb = cache_attn_k.shape[2]
    li = 0
    bd = _head_constants()
    tril, triu, gate_rep = _tri_constants()

    w = w_in[li]
    gate_lo, gate_hi = 3072, 3072 + 2 * H_MLSTM
    w_main = jnp.concatenate([w[:, :gate_lo], w[:, gate_hi:]], axis=1).astype(BF16)
    wg = jnp.pad(w[:, gate_lo:gate_hi], ((0, 0), (0, LANES - 2 * H_MLSTM)))
    wg_hi = wg.astype(BF16)
    wg_lo = (wg - wg_hi.astype(F32)).astype(BF16)
    g1 = norm1_g[li][None, :]
    qg = jnp.tile(att_qnorm_g[li], H_ATT)[None, :]
    kg = jnp.tile(att_knorm_g[li], H_ATT)[None, :]
    bg = jnp.pad(b_gates[li], (0, LANES - 2 * H_MLSTM))[None, :]
    conv_w = mlstm_conv_w[li]
    conv_b = mlstm_conv_b[li][None, :]
    ng = mlstm_norm_g[li][None, :]
    wo = w_out[li].astype(BF16)
    g2 = norm2_g[li][None, :]
    wq = peer_w_query[li].astype(BF16)
    pqg = peer_qnorm_g[li][None, :]
    sk = peer_subkeys[li].reshape(PEER_HEADS * 2, PEER_NKEYS, PEER_DKEY // 2).astype(BF16)
    u_b = peer_u[li].astype(BF16)
    vt_b = (0.5 * peer_v[li]).astype(BF16).T

    n_p = batch * seq
    xp2 = x_prompt.reshape(n_p, D_MODEL)
    pos_p = jnp.arange(seq, dtype=jnp.int32)
    q, k, v, mqk, mv, mo, gates, kt, vt = _in_projection(xp2, pos_p, 512, g1, w_main, wg_hi, wg_lo, qg, kg, bd,
                                                         transposed_kv=True)
    att_p = _attn_prompt(q, k, v, batch, seq)
    mh_p, c_p, n_p_state, m_p = _mlstm_prompt(mqk, mv, mo, gates, batch, seq, conv_w, conv_b, bg, ng, tril, triu,
                                                  gate_rep)
    n_all = n_p + nsamp
    group = LANES * SUBLANES
    n_pad = -(-n_all // group) * group
    x1_all, hn_all = _out_projection(xp2, att_p, mh_p.reshape(n_p, W_MLSTM), wo, g2, 512, n_pad, (0, n_pad))
    wbp = min(wb, seq)
    new_k_prompt = jnp.transpose(kt, (0, 3, 1, 2))[None, :, seq - wbp:]
    new_v_prompt = jnp.transpose(vt, (0, 3, 1, 2))[None, :, seq - wbp:]
    new_conv_prompt = mqk.reshape(batch, seq, -1)[None, :, seq - (CONV_W - 1):]

    xs2 = x_sample.reshape(nsamp, D_MODEL)
    pos_s = jnp.full((nsamp,), past_len, dtype=jnp.int32)
    qs, ks, vs, mqk_s, mv_s, mo_s, gates_s = _in_projection(xs2, pos_s, nsamp, g1, w_main, wg_hi, wg_lo, qg, kg, bd)
    cache_kt = jnp.transpose(cache_attn_k[li], (0, 2, 3, 1))
    cache_vt = jnp.transpose(cache_attn_v[li], (0, 2, 3, 1))
    att_s = _attn_sample(qs, ks, vs, cache_kt, cache_vt)
    mh_s, c_s, n_s, m_s = _mlstm_sample(mqk_s, mv_s, mo_s, gates_s, state_mlstm_conv[li], state_mlstm_C[li],
                                        state_mlstm_n[li], state_mlstm_m[li], conv_w, conv_b, bg, ng)
    x1_all, hn_all = _out_projection(xs2, att_s, mh_s, wo, g2, nsamp, n_pad, (n_p, n_all),
                                     into=(x1_all, hn_all))
    new_conv_sample = jnp.concatenate([state_mlstm_conv[li][:, 1:], mqk_s[:, None, :]], axis=1)[None]

    scores_t = _peer_scores(hn_all, wq, pqg, sk, 512)
    rank2, cnt1, e1, e2 = _peer_select(scores_t)
    sel = (rank2, e2, cnt1, e1)
    y_main = _peer_dense(hn_all, u_b, vt_b, *sel, x1_all, 0, n_p, PEER_TOK_TILE)
    n_tail = -(-nsamp // PEER_TAIL_TILE) * PEER_TAIL_TILE
    y_tail = _peer_dense(hn_all, u_b, vt_b, *sel, x1_all, n_p, n_tail, PEER_TAIL_TILE)
    y_prompt = y_main.reshape(batch, seq, D_MODEL)
    y_sample = y_tail[:nsamp].reshape(nsamp, 1, D_MODEL)

    return (y_prompt, y_sample, new_k_prompt, new_v_prompt,
            ks.reshape(1, nsamp, 1, H_ATT, HEAD_DIM), vs.reshape(1, nsamp, 1, H_ATT, HEAD_DIM),
            c_p[None], n_p_state[None], m_p[None, :, :, 0], new_conv_prompt,
            c_s[None], n_s[None], m_s[None], new_conv_sample)
```

```python
import functools

import numpy as np
import jax
import jax.numpy as jnp
from jax import lax
from jax.experimental import pallas as pl
from jax.experimental.pallas import tpu as pltpu

F32 = jnp.float32
BF16 = jnp.bfloat16

D_MODEL = 1024
HEAD_DIM = 64
W_ATT = 512
W_MLSTM = 512
H_ATT = 8
H_MLSTM = 8
DILATIONS = ((128, 1), (512, 4), (2048, 16))
ATT_BLOCK = 128
ROPE_THETA = 10000.0
CONV_W = 4
PEER_HEADS = 8
PEER_NKEYS = 128
PEER_DKEY = 256
PEER_TOPK = 16
EPS = 1e-6
NEG = -1e30

LANES = 128
SUBLANES = 8
VMEM_LIMIT = 56 * 1024 * 1024


def _cparams(*sem):
    return pltpu.CompilerParams(dimension_semantics=sem, vmem_limit_bytes=VMEM_LIMIT)


def _split2(x):
    hi = x.astype(BF16)
    lo = (x - hi.astype(F32)).astype(BF16)
    return hi, lo


def _split3(x):
    hi = x.astype(BF16)
    r = x - hi.astype(F32)
    mid = r.astype(BF16)
    lo = (r - mid.astype(F32)).astype(BF16)
    return hi, mid, lo


def _dot(a, b):
    return jnp.dot(a, b, preferred_element_type=F32)


def _dot_nt(a, b):
    return lax.dot_general(a, b, (((1,), (1,)), ((), ())), preferred_element_type=F32)


def _dot_parts(parts, b):
    acc = _dot(parts[0], b)
    for p in parts[1:]:
        acc = acc + _dot(p, b)
    return acc


def _sigmoid(x):
    return 1.0 / (1.0 + jnp.exp(-x))


def _log_sigmoid(x):
    return jnp.minimum(x, 0.0) - jnp.log1p(jnp.exp(-jnp.abs(x)))


def _inproj_kernel(x_ref, g_ref, w_ref, wgh_ref, wgl_ref, qg_ref, kg_ref, cos_ref, sin_ref, bd_ref,
                   q_ref, k_ref, v_ref, mqk_ref, mv_ref, mo_ref, gates_ref, kt_ref=None, vt_ref=None):
    x = x_ref[...]
    ms = jnp.mean(x * x, axis=-1, keepdims=True)
    xn = x * lax.rsqrt(ms + EPS) * g_ref[...]
    xh, xl = _split2(xn)

    def seg(lo, hi):
        return _dot(xh, w_ref[:, lo:hi])

    bd = bd_ref[...]
    cos = cos_ref[...]
    sin = sin_ref[...]
    lane = lax.broadcasted_iota(jnp.int32, cos.shape, 1)
    first_half = (lane % HEAD_DIM) < (HEAD_DIM // 2)

    def head_norm_rope(a, g):
        sq = a * a
        hi, lo = _split2(sq)
        msq = _dot(hi, bd) + _dot(lo, bd)
        y = a * lax.rsqrt(msq + EPS) * g
        rot = jnp.where(first_half, pltpu.roll(y, W_ATT - HEAD_DIM // 2, 1), pltpu.roll(y, HEAD_DIM // 2, 1))
        return y * cos + rot * sin

    q_ref[...] = head_norm_rope(seg(0, 512), qg_ref[...]) * (HEAD_DIM ** -0.5)
    k = head_norm_rope(seg(512, 1024), kg_ref[...])
    v = seg(1024, 1536)
    k_ref[...] = k
    v_ref[...] = v
    if kt_ref is not None:
        kt_ref[0] = k.T.reshape(H_ATT, HEAD_DIM, k.shape[0])
        vt_ref[0] = v.T.reshape(H_ATT, HEAD_DIM, v.shape[0])
    mqk_ref[...] = seg(1536, 2560)
    mv_ref[...] = seg(2560, 3072)
    mo_ref[...] = seg(3072, 3584)
    gates_ref[...] = _dot(xh, wgh_ref[...]) + _dot(xl, wgh_ref[...]) + _dot(xh, wgl_ref[...])


def _rope_tables(pos):
    half = HEAD_DIM // 2
    inv = ROPE_THETA ** (-jnp.arange(half, dtype=F32) / half)
    ang = pos.astype(F32)[:, None] * inv[None, :]
    cos = jnp.cos(ang)
    sin = jnp.sin(ang)
    cos_h = jnp.concatenate([cos, cos], axis=-1)
    sin_h = jnp.concatenate([-sin, sin], axis=-1)
    return jnp.tile(cos_h, (1, H_ATT)), jnp.tile(sin_h, (1, H_ATT))


def _in_projection(x2d, pos, tm, norm_g, w_main, wg_hi, wg_lo, qg, kg, bd, transposed_kv=False):
    n = x2d.shape[0]
    cos, sin = _rope_tables(pos)
    pblocks = pos.shape[0] // tm
    row = lambda i: (i, 0)
    fixed = lambda i: (0, 0)
    tab = lambda i: (i % pblocks, 0)
    widths = (512, 512, 512, 1024, 512, 512, LANES)
    out_specs = [pl.BlockSpec((tm, w), row) for w in widths]
    out_shape = [jax.ShapeDtypeStruct((n, w), F32) for w in widths]
    if transposed_kv:
        tshape = (n // pos.shape[0], H_ATT, HEAD_DIM, pos.shape[0])
        out_specs += [pl.BlockSpec((1, H_ATT, HEAD_DIM, tm), lambda i: (i // pblocks, 0, 0, i % pblocks))] * 2
        out_shape += [jax.ShapeDtypeStruct(tshape, F32)] * 2
    return pl.pallas_call(
        _inproj_kernel,
        grid=(n // tm,),
        in_specs=[
            pl.BlockSpec((tm, D_MODEL), row),
            pl.BlockSpec((1, D_MODEL), fixed),
            pl.BlockSpec(w_main.shape, fixed),
            pl.BlockSpec(wg_hi.shape, fixed),
            pl.BlockSpec(wg_lo.shape, fixed),
            pl.BlockSpec((1, W_ATT), fixed),
            pl.BlockSpec((1, W_ATT), fixed),
            pl.BlockSpec((tm, W_ATT), tab),
            pl.BlockSpec((tm, W_ATT), tab),
            pl.BlockSpec((W_ATT, W_ATT), fixed),
        ],
        out_specs=out_specs,
        out_shape=out_shape,
        compiler_params=_cparams("parallel"),
        name="in_projection",
    )(x2d, norm_g, w_main, wg_hi, wg_lo, qg, kg, cos, sin, bd)


def _merge_branches(outs, lses):
    m = functools.reduce(jnp.maximum, lses)
    es = [jnp.exp(l - m) for l in lses]
    return sum(e * o for e, o in zip(es, outs)) / sum(es)


ATT_GROUP = 8


def _attn_prompt_kernel(q_ref, k_ref, v_ref, att_ref, o_scr, l_scr):
    seq = q_ref.shape[1]
    blk = ATT_BLOCK
    lane = lax.broadcasted_iota(jnp.int32, (1, LANES), 1)
    ii = lax.broadcasted_iota(jnp.int32, (2 * blk, 2 * blk), 0) % blk
    jj = lax.broadcasted_iota(jnp.int32, (2 * blk, 2 * blk), 1)
    mask_prev = jnp.minimum(jj - ii, ii + blk - jj) >= 0
    mask_first = (lax.broadcasted_iota(jnp.int32, (2 * blk, blk), 1)
                  <= lax.broadcasted_iota(jnp.int32, (2 * blk, blk), 0) % blk)

    first_head = (lane // HEAD_DIM) == 0

    def attend(q, k2, v2, mask):
        q2 = jnp.concatenate([jnp.where(first_head, q, 0.0), jnp.where(first_head, 0.0, q)], axis=0).astype(BF16)
        s = jnp.where(mask, _dot_nt(q2, k2), NEG)
        m = jnp.max(s, axis=-1, keepdims=True)
        p = jnp.exp(s - m)
        l = jnp.sum(p, axis=-1, keepdims=True)
        o = _dot((p / l).astype(BF16), v2)
        lse = jnp.broadcast_to(m + jnp.log(l), o.shape)
        return jnp.where(first_head, o[:blk], o[blk:]), jnp.where(first_head, lse[:blk], lse[blk:])

    def group(branch, dil, starts, chained, first_has_prev):
        def rows(at):
            return pl.ds(at, blk, stride=dil) if dil > 1 else pl.ds(at, blk)

        qs = [q_ref[0, rows(at), :] for at in starts]
        ks = [k_ref[0, rows(at), :].astype(BF16) for at in starts]
        vs = [v_ref[0, rows(at), :].astype(BF16) for at in starts]
        k_before = v_before = None
        if chained and first_has_prev:
            before = starts[0] - blk * dil
            k_before = k_ref[0, rows(before), :].astype(BF16)
            v_before = v_ref[0, rows(before), :].astype(BF16)
        results = []
        for i in range(len(starts)):
            kp, vp = (k_before, v_before) if i == 0 else (ks[i - 1], vs[i - 1])
            if chained and kp is not None:
                results.append(attend(qs[i], jnp.concatenate([kp, ks[i]], axis=0),
                                      jnp.concatenate([vp, vs[i]], axis=0), mask_prev))
            else:
                results.append(attend(qs[i], ks[i], vs[i], mask_first))
        for at, (o_pair, l_pair) in zip(starts, results):
            o_scr[branch, rows(at), :] = o_pair
            l_scr[branch, rows(at), :] = l_pair

    for branch, (_, dil) in enumerate(DILATIONS):
        nblocks = seq // dil // blk
        step = blk * dil
        if nblocks == 1:
            def classes(g, carry, branch=branch, dil=dil):
                group(branch, dil, [g * ATT_GROUP + i for i in range(ATT_GROUP)], False, False)
                return carry

            lax.fori_loop(0, dil // ATT_GROUP, classes, 0)
        else:
            per_group = min(ATT_GROUP, nblocks)
            assert nblocks % per_group == 0

            def residue(r, carry, branch=branch, dil=dil, nblocks=nblocks, step=step, per_group=per_group):
                group(branch, dil, [r + i * step for i in range(per_group)], True, False)

                def later(g, c):
                    base = g * (per_group * step)
                    base = pl.multiple_of(base, blk) if dil == 1 else r + base
                    group(branch, dil, [base + i * step for i in range(per_group)], True, True)
                    return c

                if nblocks > per_group:
                    lax.fori_loop(1, nblocks // per_group, later, 0)
                return carry

            if dil == 1:
                residue(0, 0)
            else:
                lax.fori_loop(0, dil, residue, 0)

    nbr = len(DILATIONS)
    att_ref[0] = _merge_branches([o_scr[i] for i in range(nbr)], [l_scr[i] for i in range(nbr)])


def _attn_prompt(q, k, v, batch, seq):
    shape3 = (batch, seq, W_ATT)
    blk = pl.BlockSpec((1, seq, LANES), lambda b, p: (b, 0, p))
    att = pl.pallas_call(
        _attn_prompt_kernel,
        grid=(batch, W_ATT // LANES),
        in_specs=[blk, blk, blk],
        out_specs=blk,
        out_shape=jax.ShapeDtypeStruct(shape3, F32),
        scratch_shapes=[pltpu.VMEM((len(DILATIONS), seq, LANES), F32)] * 2,
        compiler_params=_cparams("parallel", "parallel"),
        name="attn_prompt",
    )(q.reshape(shape3), k.reshape(shape3), v.reshape(shape3))
    return att.reshape(batch * seq, W_ATT)


def _attn_sample_kernel(q_ref, kn_ref, vn_ref, kt_ref, vt_ref, att_ref):
    mine = lax.broadcasted_iota(jnp.int32, (1, 1, q_ref.shape[-1]), 2) == pl.program_id(0)
    q, kn, vn = (jnp.sum(jnp.where(mine, r[...], 0.0), axis=2, keepdims=True) for r in (q_ref, kn_ref, vn_ref))
    kt = kt_ref[0]
    vt = vt_ref[0]
    wb = kt.shape[-1]
    scores = jnp.sum(kt * q, axis=1, keepdims=True)
    s_new = jnp.sum(q * kn, axis=1, keepdims=True)
    row = lax.broadcasted_iota(jnp.int32, (1, 1, wb), 2)
    outs, lses = [], []
    for _, dil in DILATIONS:
        on_grid = jnp.where(row % dil == 0, row, -1)
        s = jnp.where(on_grid >= wb - ATT_BLOCK * dil, scores, NEG)
        m = jnp.maximum(jnp.max(s, axis=2, keepdims=True), s_new)
        p = jnp.exp(s - m)
        p_new = jnp.exp(s_new - m)
        l = jnp.sum(p, axis=2, keepdims=True) + p_new
        outs.append((jnp.sum(vt * p, axis=2, keepdims=True) + p_new * vn) / l)
        lses.append(m + jnp.log(l))
    att_ref[0] = _merge_branches(outs, lses)


def _attn_sample(q, k_new, v_new, cache_kt, cache_vt):
    nb, wb = cache_kt.shape[0], cache_kt.shape[-1]
    assert all(wb % dil == 0 and wb >= ATT_BLOCK * dil for _, dil in DILATIONS)
    col = (nb, H_ATT, HEAD_DIM, 1)
    col_spec = pl.BlockSpec((1, H_ATT, HEAD_DIM, 1), lambda b: (b, 0, 0, 0))
    cache_spec = pl.BlockSpec((1, H_ATT, HEAD_DIM, wb), lambda b: (b, 0, 0, 0))
    all_seqs = pl.BlockSpec((H_ATT, HEAD_DIM, nb), lambda b: (0, 0, 0))
    by_seq_last = lambda t: jnp.transpose(t.reshape(nb, H_ATT, HEAD_DIM), (1, 2, 0))
    att = pl.pallas_call(
        _attn_sample_kernel,
        grid=(nb,),
        in_specs=[all_seqs] * 3 + [cache_spec] * 2,
        out_specs=col_spec,
        out_shape=jax.ShapeDtypeStruct(col, F32),
        compiler_params=_cparams("parallel"),
        name="attn_sample",
    )(by_seq_last(q), by_seq_last(k_new), by_seq_last(v_new), cache_kt, cache_vt)
    return att.reshape(nb, W_ATT)


MLSTM_CHUNK = 128


MLSTM_SEQS = 2


def _mlstm_prompt_kernel(mqk_ref, mv_ref, mo_ref, gates_ref, cw_ref, cb_ref, bg_ref, ng_ref, tril_ref, triu_ref,
                         rep_ref, mh_ref, c_out_ref, n_out_ref, m_out_ref,
                         conv_scr, c_scr, n_scr, m_scr):
    @pl.when(pl.program_id(1) == 0)
    def _():
        conv_scr[:, 0:SUBLANES, :] = jnp.zeros((MLSTM_SEQS, SUBLANES, 2 * W_MLSTM), F32)
        c_scr[...] = jnp.zeros(c_scr.shape, F32)
        n_scr[...] = jnp.zeros(n_scr.shape, F32)
        m_scr[...] = jnp.zeros(m_scr.shape, F32)

    for i in range(MLSTM_SEQS):
        _mlstm_chunk(mqk_ref.at[i], mv_ref.at[i], mo_ref.at[i], gates_ref.at[i], cw_ref, cb_ref, bg_ref, ng_ref,
                     tril_ref, triu_ref, rep_ref, mh_ref.at[i], c_out_ref.at[i], n_out_ref.at[i], m_out_ref.at[i],
                     conv_scr.at[i], c_scr.at[i], n_scr.at[i], m_scr.at[i])


def _mlstm_chunk(mqk_ref, mv_ref, mo_ref, gates_ref, cw_ref, cb_ref, bg_ref, ng_ref, tril_ref, triu_ref,
                 rep_ref, mh_ref, c_out_ref, n_out_ref, m_out_ref, conv_scr, c_scr, n_scr, m_scr):
    L = MLSTM_CHUNK
    x = mqk_ref[...]
    conv_scr[SUBLANES:SUBLANES + L, :] = x
    conv = cb_ref[...] + cw_ref[CONV_W - 1:CONV_W, :] * x
    for j in range(CONV_W - 1):
        shift = CONV_W - 1 - j
        conv = conv + cw_ref[j:j + 1, :] * conv_scr[SUBLANES - shift:SUBLANES - shift + L, :]
    conv_scr[0:SUBLANES, :] = x[L - SUBLANES:L, :]
    qk = conv * _sigmoid(conv)
    q_all = qk[:, :W_MLSTM]
    k_all = qk[:, W_MLSTM:] * (HEAD_DIM ** -0.5)
    v_all = mv_ref[...]
    vt_all = v_all.T
    mo = mo_ref[...]

    assert L == LANES
    gb = gates_ref[...] + bg_ref[...]
    gbt = gb.T
    g_rep = _dot_parts(_split3(gb), rep_ref[...])
    ig_rep = g_rep[:, :H_MLSTM * LANES]
    lf_rep = _log_sigmoid(g_rep[:, H_MLSTM * LANES:])
    lf_row = _log_sigmoid(gbt[H_MLSTM:2 * H_MLSTM, :])
    tril = tril_ref[...]
    b_rep = _dot(tril, lf_rep.astype(BF16))
    rest = lf_rep - lf_rep.astype(BF16).astype(F32)
    b_rep = b_rep + _dot(tril, rest.astype(BF16))
    b_rep = b_rep + _dot(tril, (rest - rest.astype(BF16).astype(F32)).astype(BF16))
    b_row_all = _dot_parts(_split3(lf_row), triu_ref[...])

    ti = lax.broadcasted_iota(jnp.int32, (L, L), 0)
    si = lax.broadcasted_iota(jnp.int32, (L, L), 1)
    causal = si <= ti

    for h in range(H_MLSTM):
        hs = slice(h * HEAD_DIM, (h + 1) * HEAD_DIM)
        rep = slice(h * LANES, (h + 1) * LANES)
        qh = q_all[:, hs]
        kh = k_all[:, hs]
        vh = v_all[:, hs]
        bc = b_rep[:, rep]
        igc = ig_rep[:, rep]
        br = b_row_all[h:h + 1, :]
        igr = gbt[h:h + 1, :]
        m_prev = m_scr[h:h + 1, :]
        c_prev = c_scr[h]
        n_prev = n_scr[h:h + 1, :]

        log_d = jnp.where(causal, bc - br + igr, NEG)
        m_inter = bc + m_prev
        m_t = jnp.maximum(m_inter, jnp.max(log_d, axis=-1, keepdims=True))
        qh_b = qh.astype(BF16)
        kh_b = kh.astype(BF16)
        sd = _dot_nt(qh_b, kh_b) * jnp.exp(log_d - m_t)
        scale_inter = jnp.exp(m_inter - m_t)
        num = (scale_inter[:, :HEAD_DIM] * _dot_nt(qh_b, c_prev.astype(BF16))
               + _dot(sd.astype(BF16), vh.astype(BF16)))
        den = scale_inter * jnp.sum(qh * n_prev, axis=-1, keepdims=True) + jnp.sum(sd, axis=-1, keepdims=True)
        hh = num / jnp.maximum(jnp.abs(den), jnp.exp(-m_t))[:, :HEAD_DIM]

        m_new = m_t[L - 1:L, :]
        b_last = bc[L - 1:L, :]
        decay = jnp.exp(b_last + m_prev - m_new)
        w_row = jnp.exp(b_last - br + igr - m_new)
        w_col = jnp.exp(b_last - bc + igc - m_new)
        vtw = (vt_all[hs, :] * w_row).astype(BF16)
        c_new = decay[:, :HEAD_DIM] * c_prev + _dot(vtw, kh_b)
        n_new = decay[:, :HEAD_DIM] * n_prev + jnp.sum(w_col[:, :HEAD_DIM] * kh, axis=0, keepdims=True)
        c_scr[h] = c_new
        n_scr[h:h + 1, :] = n_new
        m_scr[h:h + 1, :] = m_new
        c_out_ref[h] = c_new
        n_out_ref[h:h + 1, :] = n_new
        m_out_ref[h:h + 1, :] = m_new

        y = hh * lax.rsqrt(jnp.mean(hh * hh, axis=-1, keepdims=True) + EPS) * ng_ref[:, hs]
        mh_ref[:, hs] = _sigmoid(mo[:, hs]) * y


def _mlstm_prompt(mqk, mv, mo, gates, batch, seq, conv_w, conv_b, bg, ng, tril, triu, rep):
    L = MLSTM_CHUNK
    nseq = MLSTM_SEQS
    assert batch % nseq == 0
    nchunk = seq // L
    chunk = lambda b, c: (b, c, 0)
    fixed = lambda b, c: (0, 0)
    return pl.pallas_call(
        _mlstm_prompt_kernel,
        grid=(batch // nseq, nchunk),
        in_specs=[
            pl.BlockSpec((nseq, L, 2 * W_MLSTM), chunk),
            pl.BlockSpec((nseq, L, W_MLSTM), chunk),
            pl.BlockSpec((nseq, L, W_MLSTM), chunk),
            pl.BlockSpec((nseq, L, LANES), chunk),
            pl.BlockSpec((CONV_W, 2 * W_MLSTM), fixed),
            pl.BlockSpec((1, 2 * W_MLSTM), fixed),
            pl.BlockSpec((1, LANES), fixed),
            pl.BlockSpec((1, W_MLSTM), fixed),
            pl.BlockSpec((L, L), fixed),
            pl.BlockSpec((L, L), fixed),
            pl.BlockSpec(rep.shape, fixed),
        ],
        out_specs=[
            pl.BlockSpec((nseq, L, W_MLSTM), chunk),
            pl.BlockSpec((nseq, H_MLSTM, HEAD_DIM, HEAD_DIM), lambda b, c: (b, 0, 0, 0)),
            pl.BlockSpec((nseq, H_MLSTM, HEAD_DIM), lambda b, c: (b, 0, 0)),
            pl.BlockSpec((nseq, H_MLSTM, LANES), lambda b, c: (b, 0, 0)),
        ],
        out_shape=[
            jax.ShapeDtypeStruct((batch, seq, W_MLSTM), F32),
            jax.ShapeDtypeStruct((batch, H_MLSTM, HEAD_DIM, HEAD_DIM), F32),
            jax.ShapeDtypeStruct((batch, H_MLSTM, HEAD_DIM), F32),
            jax.ShapeDtypeStruct((batch, H_MLSTM, LANES), F32),
        ],
        scratch_shapes=[
            pltpu.VMEM((nseq, L + SUBLANES, 2 * W_MLSTM), F32),
            pltpu.VMEM((nseq, H_MLSTM, HEAD_DIM, HEAD_DIM), F32),
            pltpu.VMEM((nseq, H_MLSTM, HEAD_DIM), F32),
            pltpu.VMEM((nseq, H_MLSTM, LANES), F32),
        ],
        compiler_params=_cparams("parallel", "arbitrary"),
        name="mlstm_prompt",
    )(mqk.reshape(batch, seq, -1), mv.reshape(batch, seq, -1), mo.reshape(batch, seq, -1),
      gates.reshape(batch, seq, -1), conv_w, conv_b, bg, ng, tril, triu, rep)


def _mlstm_sample_pre_kernel(mqk_ref, buf_ref, gates_ref, cw_ref, cb_ref, bg_ref, qk_ref, g_ref):
    conv = cb_ref[...] + cw_ref[CONV_W - 1:CONV_W, :] * mqk_ref[...]
    for j in range(CONV_W - 1):
        conv = conv + cw_ref[j:j + 1, :] * buf_ref[j]
    qk = conv * _sigmoid(conv)
    lane = lax.broadcasted_iota(jnp.int32, qk.shape, 1)
    qk_ref[...] = jnp.where(lane < W_MLSTM, qk, qk * (HEAD_DIM ** -0.5))
    gb = gates_ref[...] + bg_ref[...]
    glane = lax.broadcasted_iota(jnp.int32, gb.shape, 1)
    g_ref[...] = jnp.where(glane < H_MLSTM, gb, _log_sigmoid(gb))


def _mlstm_sample_step_kernel(q_ref, k_ref, v_ref, mo_ref, ig_ref, lf_ref, c0_ref, n0_ref, m0_ref, ng_ref,
                              mh_ref, c_ref, n_ref, m_ref):
    mine = lax.broadcasted_iota(jnp.int32, (1, v_ref.shape[-1]), 1) == pl.program_id(0)
    column = lambda ref, h: jnp.sum(jnp.where(mine, ref[h], 0.0), axis=1, keepdims=True)
    for h in range(H_MLSTM):
        q = q_ref[0, h]
        k = k_ref[0, h]
        v = column(v_ref, h)
        ig = ig_ref[0, h]
        lf = lf_ref[0, h]
        c0 = c0_ref[0, h]
        n0 = n0_ref[0, h]
        m0 = m0_ref[0, h]
        m_inter = lf + m0
        m_t = jnp.maximum(m_inter, ig)
        w_in = jnp.exp(ig - m_t)
        sd = jnp.sum(q * k, axis=-1, keepdims=True) * w_in
        scale_inter = jnp.exp(m_inter - m_t)
        num = scale_inter * jnp.sum(c0 * q, axis=-1, keepdims=True) + sd * v
        den = scale_inter * jnp.sum(n0 * q, axis=-1, keepdims=True) + sd
        hh = num / jnp.maximum(jnp.abs(den), jnp.exp(-m_t))
        decay = jnp.exp(lf + m0 - m_t)
        c_ref[0, h] = decay * c0 + w_in * (v * k)
        n_ref[0, h] = decay * n0 + w_in * k
        m_ref[0, h] = m_t
        y = hh * lax.rsqrt(jnp.mean(hh * hh, axis=0, keepdims=True) + EPS) * ng_ref[h]
        mh_ref[0, h] = _sigmoid(column(mo_ref, h)) * y


def _mlstm_sample(mqk, mv, mo, gates, conv_buf, c0, n0, m0, conv_w, conv_b, bg, ng):
    nb = mqk.shape[0]
    full = lambda *shape: pl.BlockSpec(shape, lambda: (0,) * len(shape))
    qk, g = pl.pallas_call(
        _mlstm_sample_pre_kernel,
        in_specs=[full(nb, 2 * W_MLSTM), full(CONV_W - 1, nb, 2 * W_MLSTM), full(nb, LANES),
                  full(CONV_W, 2 * W_MLSTM), full(1, 2 * W_MLSTM), full(1, LANES)],
        out_specs=[full(nb, 2 * W_MLSTM), full(nb, LANES)],
        out_shape=[jax.ShapeDtypeStruct((nb, 2 * W_MLSTM), F32), jax.ShapeDtypeStruct((nb, LANES), F32)],
        name="mlstm_sample_pre",
    )(mqk, jnp.swapaxes(conv_buf, 0, 1), gates, conv_w, conv_b, bg)
    row = (nb, H_MLSTM, 1, HEAD_DIM)
    col = (nb, H_MLSTM, HEAD_DIM, 1)
    one = (nb, H_MLSTM, 1, 1)
    mat = (nb, H_MLSTM, HEAD_DIM, HEAD_DIM)
    seqs = (H_MLSTM, HEAD_DIM, nb)
    spec = lambda shape: (pl.BlockSpec(seqs, lambda b: (0, 0, 0)) if shape == seqs
                          else pl.BlockSpec((1,) + shape[1:], lambda b: (b, 0, 0, 0)))
    by_seq_last = lambda t: jnp.transpose(t.reshape(nb, H_MLSTM, HEAD_DIM), (1, 2, 0))
    ins = [
        (qk[:, :W_MLSTM].reshape(row), row), (qk[:, W_MLSTM:].reshape(row), row), (by_seq_last(mv), seqs),
        (by_seq_last(mo), seqs), (g[:, :H_MLSTM].reshape(one), one), (g[:, H_MLSTM:2 * H_MLSTM].reshape(one), one),
        (c0, mat), (n0.reshape(row), row), (m0.reshape(one), one),
    ]
    mh, c, n, m = pl.pallas_call(
        _mlstm_sample_step_kernel,
        grid=(nb,),
        in_specs=[spec(s) for _, s in ins] + [pl.BlockSpec((H_MLSTM, HEAD_DIM, 1), lambda b: (0, 0, 0))],
        out_specs=[spec(col), spec(mat), spec(row), spec(one)],
        out_shape=[jax.ShapeDtypeStruct(s, F32) for s in (col, mat, row, one)],
        compiler_params=_cparams("parallel"),
        name="mlstm_sample_step",
    )(*[a for a, _ in ins], ng.reshape(H_MLSTM, HEAD_DIM, 1))
    return mh.reshape(nb, W_MLSTM), c, n.reshape(nb, H_MLSTM, HEAD_DIM), m.reshape(nb, H_MLSTM)


def _outproj_kernel(x_ref, att_ref, mh_ref, w_ref, g_ref, x1_ref, hn_ref):
    x1 = (x_ref[...] + _dot(att_ref[...].astype(BF16), w_ref[0:W_ATT, :])
          + _dot(mh_ref[...].astype(BF16), w_ref[W_ATT:, :]))
    x1_ref[...] = x1
    ms = jnp.mean(x1 * x1, axis=-1, keepdims=True)
    hn_ref[...] = (x1 * lax.rsqrt(ms + EPS) * g_ref[...]).astype(BF16)


def _outproj_into_kernel(x1_any, hn_any, *refs):
    del x1_any, hn_any
    _outproj_kernel(*refs)


def _out_projection(x2d, att, mh, w_out, g2, tm, n_out, out_rows, into=None):
    n_in = x2d.shape[0]
    in_blocks = n_in // tm
    first = out_rows[0] // tm
    row = lambda i: (i % in_blocks, 0)
    out_row = lambda i: (first + i, 0)
    fixed = lambda i: (0, 0)
    half = pl.BlockSpec((tm, W_ATT), row)
    in_specs = [pl.BlockSpec((tm, D_MODEL), row), half, half,
                pl.BlockSpec((D_MODEL, D_MODEL), fixed), pl.BlockSpec((1, D_MODEL), fixed)]
    args = (x2d, att, mh, w_out, g2)
    body = _outproj_kernel
    aliases = {}
    if into is not None:
        in_specs = [pl.BlockSpec(memory_space=pl.ANY)] * 2 + in_specs
        args = (*into, *args)
        body = _outproj_into_kernel
        aliases = {0: 0, 1: 1}
    return pl.pallas_call(
        body,
        grid=((out_rows[1] - out_rows[0]) // tm,),
        in_specs=in_specs,
        out_specs=[pl.BlockSpec((tm, D_MODEL), out_row), pl.BlockSpec((tm, D_MODEL), out_row)],
        out_shape=[jax.ShapeDtypeStruct((n_out, D_MODEL), F32), jax.ShapeDtypeStruct((n_out, D_MODEL), BF16)],
        input_output_aliases=aliases,
        compiler_params=_cparams("parallel"),
        name="out_projection",
    )(*args)


def _peer_scores_kernel(hn_ref, wq_ref, g_ref, sk_ref, s_ref):
    q = _dot(hn_ref[...], wq_ref[...])
    half = PEER_DKEY // 2
    for h in range(PEER_HEADS):
        qh = q[:, h * PEER_DKEY:(h + 1) * PEER_DKEY]
        qn = qh * lax.rsqrt(jnp.mean(qh * qh, axis=-1, keepdims=True) + EPS) * g_ref[...]
        for part in range(2):
            s = _dot_nt(qn[:, part * half:(part + 1) * half].astype(BF16), sk_ref[2 * h + part])
            r0 = (2 * h + part) * PEER_NKEYS
            s_ref[r0:r0 + PEER_NKEYS, :] = s.T


def _peer_scores(hn, wq, g, sk, tm):
    n = hn.shape[0]
    rows = PEER_HEADS * 2 * PEER_NKEYS
    return pl.pallas_call(
        _peer_scores_kernel,
        grid=(n // tm,),
        in_specs=[pl.BlockSpec((tm, D_MODEL), lambda i: (i, 0)),
                  pl.BlockSpec(wq.shape, lambda i: (0, 0)),
                  pl.BlockSpec((1, PEER_DKEY), lambda i: (0, 0)),
                  pl.BlockSpec(sk.shape, lambda i: (0, 0, 0))],
        out_specs=pl.BlockSpec((rows, tm), lambda i: (0, i)),
        out_shape=jax.ShapeDtypeStruct((rows, n), F32),
        compiler_params=_cparams("parallel"),
        name="peer_scores",
    )(hn, wq, g, sk)


_STAIR = [(a, b) for a in range(PEER_TOPK) for b in range(PEER_TOPK) if (a + 1) * (b + 1) <= PEER_TOPK]


def _peer_select_kernel(s1_ref, s2_ref, rank2_ref, cnt1_ref, e1_ref, e2_ref,
                        work1_ref, work2_ref, rank1_ref, rank2s_ref, vals1_ref, vals2_ref,
                        vals1t_ref, vals2t_ref, cntt_ref, cnt_ref, zt_ref, zrow_ref):
    shape = rank1_ref.shape
    keyf = lax.broadcasted_iota(jnp.int32, shape, 0).astype(F32)
    chunks = [slice(c * LANES, (c + 1) * LANES) for c in range(SUBLANES)]
    halves = ((s1_ref, work1_ref, rank1_ref, vals1_ref), (s2_ref, work2_ref, rank2s_ref, vals2_ref))

    def top16_distinct():
        rank2s_ref[...] = jnp.full(shape, float(PEER_TOPK), F32)

        def one_round(a, previous):
            maxima = []
            for (src_ref, _, rank_ref, vals_ref), prev in zip(halves, previous):
                w = src_ref[...]
                if rank_ref is rank2s_ref:
                    rank_ref[...] = jnp.where(w == prev, jnp.asarray(a - 1, F32), rank_ref[...])
                mx = jnp.max(jnp.where(w < prev, w, -jnp.inf), axis=0, keepdims=True)
                vals_ref[pl.ds(a, 1), :] = mx
                maxima.append(mx)
            return tuple(maxima)

        top = jnp.full((1, shape[1]), jnp.inf, F32)
        last = lax.fori_loop(0, PEER_TOPK, one_round, (top, top))
        rank2s_ref[...] = jnp.where(s2_ref[...] == last[1], float(PEER_TOPK - 1), rank2s_ref[...])

    def top16_ties():
        for src_ref, work_ref, rank_ref, _ in halves:
            work_ref[...] = src_ref[...]
            rank_ref[...] = jnp.full(shape, float(PEER_TOPK), F32)

        def one_round(a, carry):
            for _, work_ref, rank_ref, vals_ref in halves:
                w = work_ref[...]
                mx = jnp.max(w, axis=0, keepdims=True)
                sel = keyf == jnp.min(jnp.where(w == mx, keyf, float(PEER_NKEYS)), axis=0, keepdims=True)
                work_ref[...] = jnp.where(sel, -jnp.inf, w)
                rank_ref[...] = jnp.where(sel, jnp.asarray(a, F32), rank_ref[...])
                vals_ref[pl.ds(a, 1), :] = mx
            return carry

        lax.fori_loop(0, PEER_TOPK, one_round, 0)

    top16_distinct()
    taken = [jnp.sum(jnp.where(src_ref[...] >= vals_ref[PEER_TOPK - 1:PEER_TOPK, :], 1.0, 0.0), axis=0, keepdims=True)
             for src_ref, _, _, vals_ref in halves]
    ties = jnp.max(jnp.maximum(jnp.abs(taken[0] - PEER_TOPK), jnp.abs(taken[1] - PEER_TOPK))) > 0.0

    @pl.when(ties)
    def _():
        top16_ties()

    for c, cs in enumerate(chunks):
        vals1t_ref[:, c, :] = vals1_ref[:, cs]
        vals2t_ref[:, c, :] = vals2_ref[:, cs]
    v1 = [vals1t_ref[a] for a in range(PEER_TOPK)]
    v2 = [vals2t_ref[b] for b in range(PEER_TOPK)]
    cand = [v1[a] + v2[b] for a, b in _STAIR]
    pos = []
    for i, (a, b) in enumerate(_STAIR):
        static = sum(1 for (a2, b2) in _STAIR if a2 <= a and b2 <= b and (a2, b2) != (a, b))
        pos.append(jnp.full(cand[0].shape, float(static), F32))
    for i, (ai, bi) in enumerate(_STAIR):
        for j in range(i + 1, len(_STAIR)):
            aj, bj = _STAIR[j]
            if ai <= aj and bi <= bj:
                continue
            i_first = cand[i] >= cand[j]
            pos[j] = pos[j] + jnp.where(i_first, 1.0, 0.0)
            pos[i] = pos[i] + jnp.where(i_first, 0.0, 1.0)
    e1s = [jnp.exp(v1[a] - v1[0]) for a in range(PEER_TOPK)]
    e2s = [jnp.exp(v2[b] - v2[0]) for b in range(PEER_TOPK)]
    z = jnp.zeros_like(cand[0])
    for a in range(PEER_TOPK):
        cnt_a = jnp.zeros_like(z)
        za = jnp.zeros_like(z)
        for i, (a2, b) in enumerate(_STAIR):
            if a2 != a:
                continue
            chosen = pos[i] < float(PEER_TOPK)
            cnt_a = cnt_a + jnp.where(chosen, 1.0, 0.0)
            za = za + jnp.where(chosen, e2s[b], 0.0)
        cntt_ref[a] = cnt_a
        z = z + e1s[a] * za
    zt_ref[0] = 1.0 / z
    for c, cs in enumerate(chunks):
        cnt_ref[:, cs] = cntt_ref[:, c, :]
        zrow_ref[:, cs] = zt_ref[:, c, :]

    cnt1_ref[...] = jnp.zeros(shape, F32)

    def spread_by_rank(a, carry):
        cnt1_ref[...] = jnp.where(rank1_ref[...] == jnp.asarray(a, F32), cnt_ref[pl.ds(a, 1), :], cnt1_ref[...])
        return carry

    def spread_by_value(a, carry):
        cnt1_ref[...] = jnp.where(s1_ref[...] == vals1_ref[pl.ds(a, 1), :], cnt_ref[pl.ds(a, 1), :], cnt1_ref[...])
        return carry

    @pl.when(ties)
    def _():
        lax.fori_loop(0, PEER_TOPK, spread_by_rank, 0)

    @pl.when(jnp.logical_not(ties))
    def _():
        lax.fori_loop(0, PEER_TOPK, spread_by_value, 0)

    rank2_ref[...] = rank2s_ref[...].astype(BF16)
    e1_ref[...] = jnp.exp(s1_ref[...] - vals1_ref[0:1, :]) * zrow_ref[...]
    e2_ref[...] = jnp.exp(s2_ref[...] - vals2_ref[0:1, :]).astype(BF16)


def _peer_select(scores_t):
    n = scores_t.shape[1]
    tok = LANES * SUBLANES
    blk = (PEER_NKEYS, tok)
    out_rows = PEER_HEADS * PEER_NKEYS
    vm = lambda *shape: pltpu.VMEM(shape, F32)
    return pl.pallas_call(
        _peer_select_kernel,
        grid=(PEER_HEADS, n // tok),
        in_specs=[pl.BlockSpec(blk, lambda h, g: (2 * h, g)), pl.BlockSpec(blk, lambda h, g: (2 * h + 1, g))],
        out_specs=[pl.BlockSpec(blk, lambda h, g: (h, g))] * 4,
        out_shape=[jax.ShapeDtypeStruct((out_rows, n), dt) for dt in (BF16, F32, F32, BF16)],
        scratch_shapes=[vm(*blk)] * 4 + [vm(PEER_TOPK, tok)] * 2 + [vm(PEER_TOPK, SUBLANES, LANES)] * 3
                       + [vm(PEER_TOPK, tok), vm(1, SUBLANES, LANES), vm(1, tok)],
        compiler_params=_cparams("parallel", "parallel"),
        name="peer_select",
    )(scores_t, scores_t)


PEER_TOK_TILE = 512
PEER_TAIL_TILE = 128
PEER_I1_PER_STEP = 16
PEER_MXU_CHUNKS = 2
BF16_ROWS = 16


def _peer_dense_kernel(nblk, hn_ref, u_ref, vt_prev_ref, vt_last_ref, rank2_ref, e2_ref, cnt1_ref, e1_ref,
                       x1_ref, y_ref, ht_ref, p_even_ref, p_odd_ref, acc_ref):
    j = pl.program_id(1)
    tokens = hn_ref.shape[0]
    reps = PEER_NKEYS // BF16_ROWS
    assert PEER_I1_PER_STEP % SUBLANES == 0 and PEER_I1_PER_STEP % PEER_MXU_CHUNKS == 0

    def row_bf16(tile, s):
        x8 = jnp.broadcast_to(tile[s:s + 1, :], (SUBLANES, tokens))
        x16 = jnp.concatenate([x8, x8], axis=0).astype(BF16)
        return jnp.concatenate([x16] * reps, axis=0)

    def evaluate(p_write, p_read):
        first = pl.multiple_of(j * PEER_I1_PER_STEP, SUBLANES)
        cnt_tiles = [cnt1_ref[pl.ds(h * PEER_NKEYS + first, PEER_I1_PER_STEP), :] for h in range(PEER_HEADS)]
        e1_tiles = [e1_ref[pl.ds(h * PEER_NKEYS + first, PEER_I1_PER_STEP), :] for h in range(PEER_HEADS)]
        per_chunk = PEER_I1_PER_STEP // PEER_MXU_CHUNKS
        crow = per_chunk * PEER_NKEYS
        drow = D_MODEL // PEER_MXU_CHUNKS
        for c in range(PEER_MXU_CHUNKS):
            rc = slice(c * crow, (c + 1) * crow)
            rd = slice(c * drow, (c + 1) * drow)
            a_c = _dot(u_ref[rc, :], ht_ref[...])
            if p_read is not None:
                acc_ref[rd, :] += _dot(vt_prev_ref[rd, :], p_read[...])
            for sc in range(per_chunk):
                s = c * per_chunk + sc
                g = None
                for h in range(PEER_HEADS):
                    rows = slice(h * PEER_NKEYS, (h + 1) * PEER_NKEYS)
                    w = jnp.where(rank2_ref[rows, :] < row_bf16(cnt_tiles[h], s),
                                  e2_ref[rows, :] * row_bf16(e1_tiles[h], s), jnp.zeros((), BF16))
                    g = w if g is None else g + w
                a = a_c[sc * PEER_NKEYS:(sc + 1) * PEER_NKEYS, :]
                act = a + a * lax.erf(a * (2.0 ** -0.5))
                p_write[s * PEER_NKEYS:(s + 1) * PEER_NKEYS, :] = g * act.astype(BF16)

    @pl.when(j == 0)
    def _():
        acc_ref[...] = jnp.zeros_like(acc_ref)
        ht_ref[...] = hn_ref[...].astype(F32).T.astype(BF16)
        evaluate(p_even_ref, None)

    @pl.when(j % 2 == 1)
    def _():
        evaluate(p_odd_ref, p_even_ref)

    @pl.when(jnp.logical_and(j % 2 == 0, j > 0))
    def _():
        evaluate(p_even_ref, p_odd_ref)

    @pl.when(j == nblk - 1)
    def _():
        p_last = p_odd_ref if nblk % 2 == 0 else p_even_ref
        acc = acc_ref[...] + _dot(vt_last_ref[...], p_last[...])
        y_ref[...] = x1_ref[...] + acc.T


def _peer_dense(hn, u, vt, rank2, e2, cnt1, e1, x1, first_row, nrows, T):
    assert first_row % T == 0 and nrows % T == 0
    eb = PEER_I1_PER_STEP * PEER_NKEYS
    nexp = u.shape[0]
    sel_rows = PEER_HEADS * PEER_NKEYS
    nblk = nexp // eb
    t0 = first_row // T
    rows = lambda t, j: (t0 + t, 0)
    tok = lambda t, j: (0, t0 + t)
    return pl.pallas_call(
        functools.partial(_peer_dense_kernel, nblk),
        grid=(nrows // T, nblk),
        in_specs=[
            pl.BlockSpec((T, D_MODEL), rows),
            pl.BlockSpec((eb, D_MODEL), lambda t, j: (j, 0)),
            pl.BlockSpec((D_MODEL, eb), lambda t, j: (0, jnp.maximum(j - 1, 0))),
            pl.BlockSpec((D_MODEL, eb), lambda t, j: (0, nblk - 1)),
            pl.BlockSpec((sel_rows, T), tok),
            pl.BlockSpec((sel_rows, T), tok),
            pl.BlockSpec((sel_rows, T), tok),
            pl.BlockSpec((sel_rows, T), tok),
            pl.BlockSpec((T, D_MODEL), rows),
        ],
        out_specs=pl.BlockSpec((T, D_MODEL), lambda t, j: (t, 0)),
        out_shape=jax.ShapeDtypeStruct((nrows, D_MODEL), F32),
        scratch_shapes=[pltpu.VMEM((D_MODEL, T), BF16), pltpu.VMEM((eb, T), BF16), pltpu.VMEM((eb, T), BF16),
                        pltpu.VMEM((D_MODEL, T), F32)],
        compiler_params=_cparams("parallel", "arbitrary"),
        name="peer_dense",
    )(hn, u, vt, vt, rank2, e2, cnt1, e1, x1)


def _tri_constants():
    L = MLSTM_CHUNK
    tril = np.tril(np.ones((L, L), np.float32))
    rep = (np.arange(LANES)[:, None] == np.arange(2 * H_MLSTM * LANES)[None, :] // LANES).astype(np.float32)
    return jnp.asarray(tril, BF16), jnp.asarray(tril.T, BF16), jnp.asarray(rep, BF16)


def _head_constants():
    bd = np.kron(np.eye(H_ATT, dtype=np.float32), np.full((HEAD_DIM, HEAD_DIM), 1.0 / HEAD_DIM, np.float32))
    return jnp.asarray(bd, BF16)


def kernel(x_prompt, x_sample, cache_attn_k, cache_attn_v, state_mlstm_C, state_mlstm_n, state_mlstm_m,
           state_mlstm_conv, norm1_g, w_in, att_qnorm_g, att_knorm_g, b_gates, mlstm_conv_w, mlstm_conv_b,
           mlstm_norm_g, w_out, norm2_g, peer_w_query, peer_qnorm_g, peer_subkeys, peer_u, peer_v):
    batch, seq = x_prompt.shape[:2]
    nsamp = x_sample.shape[0]
    past_len = 16384
    assert norm1_g.shape[0] == 1 and x_sample.shape[1] == 1 and seq % MLSTM_CHUNK == 0
    wb = cache_attn_k.shape[2]
    li = 0
    bd = _head_constants()
    tril, triu, gate_rep = _tri_constants()

    w = w_in[li]
    gate_lo, gate_hi = 3072, 3072 + 2 * H_MLSTM
    w_main = jnp.concatenate([w[:, :gate_lo], w[:, gate_hi:]], axis=1).astype(BF16)
    wg = jnp.pad(w[:, gate_lo:gate_hi], ((0, 0), (0, LANES - 2 * H_MLSTM)))
    wg_hi = wg.astype(BF16)
    wg_lo = (wg - wg_hi.astype(F32)).astype(BF16)
    g1 = norm1_g[li][None, :]
    qg = jnp.tile(att_qnorm_g[li], H_ATT)[None, :]
    kg = jnp.tile(att_knorm_g[li], H_ATT)[None, :]
    bg = jnp.pad(b_gates[li], (0, LANES - 2 * H_MLSTM))[None, :]
    conv_w = mlstm_conv_w[li]
    conv_b = mlstm_conv_b[li][None, :]
    ng = mlstm_norm_g[li][None, :]
    wo = w_out[li].astype(BF16)
    g2 = norm2_g[li][None, :]
    wq = peer_w_query[li].astype(BF16)
    pqg = peer_qnorm_g[li][None, :]
    sk = peer_subkeys[li].reshape(PEER_HEADS * 2, PEER_NKEYS, PEER_DKEY // 2).astype(BF16)
    u_b = peer_u[li].astype(BF16)
    vt_b = (0.5 * peer_v[li]).astype(BF16).T

    n_p = batch * seq
    xp2 = x_prompt.reshape(n_p, D_MODEL)
    pos_p = jnp.arange(seq, dtype=jnp.int32)
    q, k, v, mqk, mv, mo, gates, kt, vt = _in_projection(xp2, pos_p, 512, g1, w_main, wg_hi, wg_lo, qg, kg, bd,
                                                         transposed_kv=True)
    att_p = _attn_prompt(q, k, v, batch, seq)
    mh_p, c_p, n_p_state, m_p = _mlstm_prompt(mqk, mv, mo, gates, batch, seq, conv_w, conv_b, bg, ng, tril, triu,
                                                  gate_rep)
    n_all = n_p + nsamp
    group = LANES * SUBLANES
    n_pad = -(-n_all // group) * group
    x1_all, hn_all = _out_projection(xp2, att_p, mh_p.reshape(n_p, W_MLSTM), wo, g2, 512, n_pad, (0, n_pad))
    wbp = min(wb, seq)
    new_k_prompt = jnp.transpose(kt, (0, 3, 1, 2))[None, :, seq - wbp:]
    new_v_prompt = jnp.transpose(vt, (0, 3, 1, 2))[None, :, seq - wbp:]
    new_conv_prompt = mqk.reshape(batch, seq, -1)[None, :, seq - (CONV_W - 1):]

    xs2 = x_sample.reshape(nsamp, D_MODEL)
    pos_s = jnp.full((nsamp,), past_len, dtype=jnp.int32)
    qs, ks, vs, mqk_s, mv_s, mo_s, gates_s = _in_projection(xs2, pos_s, nsamp, g1, w_main, wg_hi, wg_lo, qg, kg, bd)
    cache_kt = jnp.transpose(cache_attn_k[li], (0, 2, 3, 1))
    cache_vt = jnp.transpose(cache_attn_v[li], (0, 2, 3, 1))
    att_s = _attn_sample(qs, ks, vs, cache_kt, cache_vt)
    mh_s, c_s, n_s, m_s = _mlstm_sample(mqk_s, mv_s, mo_s, gates_s, state_mlstm_conv[li], state_mlstm_C[li],
                                        state_mlstm_n[li], state_mlstm_m[li], conv_w, conv_b, bg, ng)
    x1_all, hn_all = _out_projection(xs2, att_s, mh_s, wo, g2, nsamp, n_pad, (n_p, n_all),
                                     into=(x1_all, hn_all))
    new_conv_sample = jnp.concatenate([state_mlstm_conv[li][:, 1:], mqk_s[:, None, :]], axis=1)[None]

    scores_t = _peer_scores(hn_all, wq, pqg, sk, 512)
    rank2, cnt1, e1, e2 = _peer_select(scores_t)
    sel = (rank2, e2, cnt1, e1)
    y_main = _peer_dense(hn_all, u_b, vt_b, *sel, x1_all, 0, n_p, PEER_TOK_TILE)
    n_tail = -(-nsamp // PEER_TAIL_TILE) * PEER_TAIL_TILE
    y_tail = _peer_dense(hn_all, u_b, vt_b, *sel, x1_all, n_p, n_tail, PEER_TAIL_TILE)
    y_prompt = y_main.reshape(batch, seq, D_MODEL)
    y_sample = y_tail[:nsamp].reshape(nsamp, 1, D_MODEL)

    return (y_prompt, y_sample, new_k_prompt, new_v_prompt,
            ks.reshape(1, nsamp, 1, H_ATT, HEAD_DIM), vs.reshape(1, nsamp, 1, H_ATT, HEAD_DIM),
            c_p[None], n_p_state[None], m_p[None, :, :, 0], new_conv_prompt,
            c_s[None], n_s[None], m_s[None], new_conv_sample)
```
